```python
import math
import jax, jax.numpy as jnp
from jax import lax
import numpy as np

D_MODEL = 2048
BATCH = 8
SEQ = 8192
DEPTH = 4

CHUNK = 64
Q_BLOCK = 128

N_HEADS = 16
Q_LORA = 512
KV_LORA = 512
QK_NOPE = 128
QK_ROPE = 64
V_HEAD = 128
MLA_WIDTH = N_HEADS * V_HEAD
ROPE_THETA = 10000.0

LRU_WIDTH = D_MODEL
LRU_BLOCKS = 16
LRU_BLOCK_DIM = LRU_WIDTH // LRU_BLOCKS
CONV_K = 4
LRU_C = 8.0

PLE_DIM = 256

EPS = 1e-6

IN_SIZES = (Q_LORA, KV_LORA, QK_ROPE, MLA_WIDTH, LRU_WIDTH, LRU_WIDTH, D_MODEL, D_MODEL)
IN_TOTAL = sum(IN_SIZES)
IN_SPLITS = tuple(int(s) for s in np.cumsum(IN_SIZES)[:-1])

kernel_name = "hybrid_mla_rglru_gated_merge"


def rms_norm(x, g):
    xf = x.astype(jnp.float32)
    y = xf * lax.rsqrt(jnp.mean(xf * xf, axis=-1, keepdims=True) + EPS)
    return (y * g.astype(jnp.float32)).astype(x.dtype)


def rope_tables(positions):
    inv_freq = ROPE_THETA ** (-jnp.arange(0, QK_ROPE, 2, dtype=jnp.float32) / QK_ROPE)
    ang = positions.astype(jnp.float32)[..., None] * inv_freq
    return jnp.cos(ang), jnp.sin(ang)


def apply_rope(x, cos, sin):
    x1, x2 = jnp.split(x, 2, axis=-1)
    c = cos.astype(x.dtype)
    s = sin.astype(x.dtype)
    return jnp.concatenate([x1 * c - x2 * s, x2 * c + x1 * s], axis=-1)


def mla_branch(q_lat, kv_lat, k_rope_raw, gate, cos, sin, q_a_norm, w_q_b, kv_a_norm, w_kv_b, w_o):
    B, S, _ = q_lat.shape
    q = (rms_norm(q_lat, q_a_norm) @ w_q_b).reshape(B, S, N_HEADS, QK_NOPE + QK_ROPE)
    q_nope, q_rope = q[..., :QK_NOPE], q[..., QK_NOPE:]
    q_rope = apply_rope(q_rope, cos[:, :, None, :], sin[:, :, None, :])
    kv = (rms_norm(kv_lat, kv_a_norm) @ w_kv_b).reshape(B, S, N_HEADS, QK_NOPE + V_HEAD)
    k_nope, v = kv[..., :QK_NOPE], kv[..., QK_NOPE:]
    k_rope = apply_rope(k_rope_raw, cos, sin)
    scale = 1.0 / math.sqrt(QK_NOPE + QK_ROPE)
    n_blk = S // Q_BLOCK
    k_chunk = jnp.arange(S) // CHUNK

    qn_b = q_nope.reshape(B, n_blk, Q_BLOCK, N_HEADS, QK_NOPE).transpose(1, 0, 2, 3, 4)
    qr_b = q_rope.reshape(B, n_blk, Q_BLOCK, N_HEADS, QK_ROPE).transpose(1, 0, 2, 3, 4)

    def attend(args):
        qn, qr, blk = args
        s = (jnp.einsum('bqhd,bkhd->bhqk', qn, k_nope)
             + jnp.einsum('bqhr,bkr->bhqk', qr, k_rope)).astype(jnp.float32) * scale
        q_chunk = (blk * Q_BLOCK + jnp.arange(Q_BLOCK)) // CHUNK
        mask = k_chunk[None, :] <= q_chunk[:, None]
        s = jnp.where(mask[None, None], s, -jnp.inf)
        pr = jax.nn.softmax(s, axis=-1).astype(v.dtype)
        return jnp.einsum('bhqk,bkhd->bqhd', pr, v)

    o = lax.map(attend, (qn_b, qr_b, jnp.arange(n_blk)))
    o = o.transpose(1, 0, 2, 3, 4).reshape(B, S, MLA_WIDTH)
    return (o * jax.nn.silu(gate)) @ w_o


def causal_dwconv(x, w, b):
    S = x.shape[1]
    xp = jnp.pad(x, ((0, 0), (CONV_K - 1, 0), (0, 0)))
    y = xp[:, 0:S, :] * w[0]
    for k in range(1, CONV_K):
        y = y + xp[:, k:k + S, :] * w[k]
    return y + b


def rglru_branch(u, gate, conv_w, conv_b, w_rg, b_rg, w_ig, b_ig, lam, w_o):
    B, S, _ = u.shape
    xc = causal_dwconv(u, conv_w, conv_b)
    xh = xc.reshape(B, S, LRU_BLOCKS, LRU_BLOCK_DIM)
    r = jax.nn.sigmoid(jnp.einsum('bshi,hij->bshj', xh, w_rg).reshape(B, S, LRU_WIDTH) + b_rg)
    i = jax.nn.sigmoid(jnp.einsum('bshi,hij->bshj', xh, w_ig).reshape(B, S, LRU_WIDTH) + b_ig)
    log_a = (-LRU_C * r.astype(jnp.float32)) * jax.nn.softplus(-lam.astype(jnp.float32))
    a = jnp.exp(log_a)
    mult = jnp.sqrt(-jnp.expm1(2.0 * log_a))
    bterm = mult * (i * xc).astype(jnp.float32)

    def combine(lhs, rhs):
        a1, b1 = lhs
        a2, b2 = rhs
        return a1 * a2, a2 * b1 + b2

    _, h = lax.associative_scan(combine, (a, bterm), axis=1)
    h = h.astype(u.dtype)
    return (h * jax.nn.silu(gate)) @ w_o


def _fwd_setup_inputs(seed: int = 0) -> dict:
    key = jax.random.key(seed)
    ks = jax.random.split(key, 32)
    f32 = jnp.float32

    def nrm(k, shape, fan_in):
        return jax.random.normal(k, shape, f32) * (fan_in ** -0.5)

    def gain(k, shape):
        return 1.0 + 0.02 * jax.random.normal(k, shape, f32)

    x = jax.random.normal(ks[0], (BATCH, SEQ, D_MODEL), f32)
    p = jax.random.normal(ks[1], (DEPTH, BATCH, SEQ, PLE_DIM), f32)
    offset = jax.random.randint(ks[2], (BATCH, 1), 0, 4096, dtype=jnp.int32)
    positions = offset + jnp.arange(SEQ, dtype=jnp.int32)[None, :]

    u = jax.random.uniform(ks[3], (DEPTH, LRU_WIDTH), f32, 0.9, 0.999)
    a0 = u ** (1.0 / LRU_C)
    lru_lambda = jnp.log(a0) - jnp.log1p(-a0)

    return {
        "x": x,
        "p": p,
        "positions": positions,
        "attn_norm": gain(ks[4], (DEPTH, D_MODEL)),
        "w_in": nrm(ks[5], (DEPTH, D_MODEL, IN_TOTAL), D_MODEL),
        "q_a_norm": gain(ks[6], (DEPTH, Q_LORA)),
        "w_q_b": nrm(ks[7], (DEPTH, Q_LORA, N_HEADS * (QK_NOPE + QK_ROPE)), Q_LORA),
        "kv_a_norm": gain(ks[8], (DEPTH, KV_LORA)),
        "w_kv_b": nrm(ks[9], (DEPTH, KV_LORA, N_HEADS * (QK_NOPE + V_HEAD)), KV_LORA),
        "conv_w": nrm(ks[10], (DEPTH, CONV_K, LRU_WIDTH), CONV_K),
        "conv_b": 0.01 * jax.random.normal(ks[11], (DEPTH, LRU_WIDTH), f32),
        "w_rg": nrm(ks[12], (DEPTH, LRU_BLOCKS, LRU_BLOCK_DIM, LRU_BLOCK_DIM), LRU_BLOCK_DIM),
        "b_rg": 0.01 * jax.random.normal(ks[13], (DEPTH, LRU_WIDTH), f32),
        "w_ig": nrm(ks[14], (DEPTH, LRU_BLOCKS, LRU_BLOCK_DIM, LRU_BLOCK_DIM), LRU_BLOCK_DIM),
        "b_ig": 0.01 * jax.random.normal(ks[15], (DEPTH, LRU_WIDTH), f32),
        "lru_lambda": lru_lambda,
        "w_o_mla": nrm(ks[16], (DEPTH, MLA_WIDTH, D_MODEL), MLA_WIDTH),
        "w_o_lru": nrm(ks[17], (DEPTH, LRU_WIDTH, D_MODEL), LRU_WIDTH),
        "w_out": nrm(ks[18], (DEPTH, D_MODEL, D_MODEL), D_MODEL),
        "ple_norm": gain(ks[19], (DEPTH, D_MODEL)),
        "w_ple_gate": nrm(ks[20], (DEPTH, D_MODEL, D_MODEL), D_MODEL),
        "w_ple": nrm(ks[21], (DEPTH, PLE_DIM, D_MODEL), PLE_DIM),
        "final_norm": gain(ks[22], (D_MODEL,)),
    }


def _fwd_reference(x, p, positions, attn_norm, w_in, q_a_norm, w_q_b, kv_a_norm, w_kv_b,
              conv_w, conv_b, w_rg, b_rg, w_ig, b_ig, lru_lambda,
              w_o_mla, w_o_lru, w_out, ple_norm, w_ple_gate, w_ple, final_norm):
    cos, sin = rope_tables(positions)
    for l in range(DEPTH):
        h = rms_norm(x, attn_norm[l])
        z = h @ w_in[l]
        q_lat, kv_lat, k_rope_raw, g_mla, u_lru, g_lru, m_mla, m_lru = jnp.split(z, IN_SPLITS, axis=-1)
        y_mla = mla_branch(q_lat, kv_lat, k_rope_raw, g_mla, cos, sin,
                           q_a_norm[l], w_q_b[l], kv_a_norm[l], w_kv_b[l], w_o_mla[l])
        y_lru = rglru_branch(u_lru, g_lru, conv_w[l], conv_b[l], w_rg[l], b_rg[l],
                             w_ig[l], b_ig[l], lru_lambda[l], w_o_lru[l])
        merged = jax.nn.sigmoid(m_mla) * y_mla + jax.nn.sigmoid(m_lru) * y_lru
        x = x + merged @ w_out[l]
        ple_gate = jax.nn.sigmoid(rms_norm(x, ple_norm[l]) @ w_ple_gate[l])
        x = x + (p[l] @ w_ple[l]) * ple_gate
    return rms_norm(x, final_norm)


import jax as _jax
import jax.numpy as _jnp

TWIN_FORMAT = 'train_step'
FWD_PARAMS = ['x', 'p', 'positions', 'attn_norm', 'w_in', 'q_a_norm', 'w_q_b', 'kv_a_norm', 'w_kv_b', 'conv_w', 'conv_b', 'w_rg', 'b_rg', 'w_ig', 'b_ig', 'lru_lambda', 'w_o_mla', 'w_o_lru', 'w_out', 'ple_norm', 'w_ple_gate', 'w_ple', 'final_norm']
TWIN_WEIGHTS = ['attn_norm', 'w_in', 'q_a_norm', 'w_q_b', 'kv_a_norm', 'w_kv_b', 'conv_w', 'conv_b', 'w_rg', 'b_rg', 'w_ig', 'b_ig', 'lru_lambda', 'w_o_mla', 'w_o_lru', 'w_out', 'ple_norm', 'w_ple_gate', 'w_ple', 'final_norm']
TWIN_DIFF_INPUT = 'x'
TWIN_INPUTS = ['x', 'p', 'positions', 'attn_norm', 'w_in', 'q_a_norm', 'w_q_b', 'kv_a_norm', 'w_kv_b', 'conv_w', 'conv_b', 'w_rg', 'b_rg', 'w_ig', 'b_ig', 'lru_lambda', 'w_o_mla', 'w_o_lru', 'w_out', 'ple_norm', 'w_ple_gate', 'w_ple', 'final_norm', 'loss_target', 'm_attn_norm', 'm_w_in', 'm_q_a_norm', 'm_w_q_b', 'm_kv_a_norm', 'm_w_kv_b', 'm_conv_w', 'm_conv_b', 'm_w_rg', 'm_b_rg', 'm_w_ig', 'm_b_ig', 'm_lru_lambda', 'm_w_o_mla', 'm_w_o_lru', 'm_w_out', 'm_ple_norm', 'm_w_ple_gate', 'm_w_ple', 'm_final_norm', 'v_attn_norm', 'v_w_in', 'v_q_a_norm', 'v_w_q_b', 'v_kv_a_norm', 'v_w_kv_b', 'v_conv_w', 'v_conv_b', 'v_w_rg', 'v_b_rg', 'v_w_ig', 'v_b_ig', 'v_lru_lambda', 'v_w_o_mla', 'v_w_o_lru', 'v_w_out', 'v_ple_norm', 'v_w_ple_gate', 'v_w_ple', 'v_final_norm']
TWIN_OUTPUTS = ['loss', 'grad_x', 'grad_attn_norm', 'grad_w_in', 'grad_q_a_norm', 'grad_w_q_b', 'grad_kv_a_norm', 'grad_w_kv_b', 'grad_conv_w', 'grad_conv_b', 'grad_w_rg', 'grad_b_rg', 'grad_w_ig', 'grad_b_ig', 'grad_lru_lambda', 'grad_w_o_mla', 'grad_w_o_lru', 'grad_w_out', 'grad_ple_norm', 'grad_w_ple_gate', 'grad_w_ple', 'grad_final_norm', 'delta_attn_norm', 'delta_w_in', 'delta_q_a_norm', 'delta_w_q_b', 'delta_kv_a_norm', 'delta_w_kv_b', 'delta_conv_w', 'delta_conv_b', 'delta_w_rg', 'delta_b_rg', 'delta_w_ig', 'delta_b_ig', 'delta_lru_lambda', 'delta_w_o_mla', 'delta_w_o_lru', 'delta_w_out', 'delta_ple_norm', 'delta_w_ple_gate', 'delta_w_ple', 'delta_final_norm', 'new_m_attn_norm', 'new_m_w_in', 'new_m_q_a_norm', 'new_m_w_q_b', 'new_m_kv_a_norm', 'new_m_w_kv_b', 'new_m_conv_w', 'new_m_conv_b', 'new_m_w_rg', 'new_m_b_rg', 'new_m_w_ig', 'new_m_b_ig', 'new_m_lru_lambda', 'new_m_w_o_mla', 'new_m_w_o_lru', 'new_m_w_out', 'new_m_ple_norm', 'new_m_w_ple_gate', 'new_m_w_ple', 'new_m_final_norm', 'new_v_attn_norm', 'new_v_w_in', 'new_v_q_a_norm', 'new_v_w_q_b', 'new_v_kv_a_norm', 'new_v_w_kv_b', 'new_v_conv_w', 'new_v_conv_b', 'new_v_w_rg', 'new_v_b_rg', 'new_v_w_ig', 'new_v_b_ig', 'new_v_lru_lambda', 'new_v_w_o_mla', 'new_v_w_o_lru', 'new_v_w_out', 'new_v_ple_norm', 'new_v_w_ple_gate', 'new_v_w_ple', 'new_v_final_norm']
TWIN_LEAF_KINDS = {'loss': 'loss', 'grad_x': 'grad_x', 'grad_attn_norm': 'grad_w', 'grad_w_in': 'grad_w', 'grad_q_a_norm': 'grad_w', 'grad_w_q_b': 'grad_w', 'grad_kv_a_norm': 'grad_w', 'grad_w_kv_b': 'grad_w', 'grad_conv_w': 'grad_w', 'grad_conv_b': 'grad_w', 'grad_w_rg': 'grad_w', 'grad_b_rg': 'grad_w', 'grad_w_ig': 'grad_w', 'grad_b_ig': 'grad_w', 'grad_lru_lambda': 'grad_w', 'grad_w_o_mla': 'grad_w', 'grad_w_o_lru': 'grad_w', 'grad_w_out': 'grad_w', 'grad_ple_norm': 'grad_w', 'grad_w_ple_gate': 'grad_w', 'grad_w_ple': 'grad_w', 'grad_final_norm': 'grad_w', 'delta_attn_norm': 'delta_w', 'delta_w_in': 'delta_w', 'delta_q_a_norm': 'delta_w', 'delta_w_q_b': 'delta_w', 'delta_kv_a_norm': 'delta_w', 'delta_w_kv_b': 'delta_w', 'delta_conv_w': 'delta_w', 'delta_conv_b': 'delta_w', 'delta_w_rg': 'delta_w', 'delta_b_rg': 'delta_w', 'delta_w_ig': 'delta_w', 'delta_b_ig': 'delta_w', 'delta_lru_lambda': 'delta_w', 'delta_w_o_mla': 'delta_w', 'delta_w_o_lru': 'delta_w', 'delta_w_out': 'delta_w', 'delta_ple_norm': 'delta_w', 'delta_w_ple_gate': 'delta_w', 'delta_w_ple': 'delta_w', 'delta_final_norm': 'delta_w', 'new_m_attn_norm': 'new_m', 'new_m_w_in': 'new_m', 'new_m_q_a_norm': 'new_m', 'new_m_w_q_b': 'new_m', 'new_m_kv_a_norm': 'new_m', 'new_m_w_kv_b': 'new_m', 'new_m_conv_w': 'new_m', 'new_m_conv_b': 'new_m', 'new_m_w_rg': 'new_m', 'new_m_b_rg': 'new_m', 'new_m_w_ig': 'new_m', 'new_m_b_ig': 'new_m', 'new_m_lru_lambda': 'new_m', 'new_m_w_o_mla': 'new_m', 'new_m_w_o_lru': 'new_m', 'new_m_w_out': 'new_m', 'new_m_ple_norm': 'new_m', 'new_m_w_ple_gate': 'new_m', 'new_m_w_ple': 'new_m', 'new_m_final_norm': 'new_m', 'new_v_attn_norm': 'new_v', 'new_v_w_in': 'new_v', 'new_v_q_a_norm': 'new_v', 'new_v_w_q_b': 'new_v', 'new_v_kv_a_norm': 'new_v', 'new_v_w_kv_b': 'new_v', 'new_v_conv_w': 'new_v', 'new_v_conv_b': 'new_v', 'new_v_w_rg': 'new_v', 'new_v_b_rg': 'new_v', 'new_v_w_ig': 'new_v', 'new_v_b_ig': 'new_v', 'new_v_lru_lambda': 'new_v', 'new_v_w_o_mla': 'new_v', 'new_v_w_o_lru': 'new_v', 'new_v_w_out': 'new_v', 'new_v_ple_norm': 'new_v', 'new_v_w_ple_gate': 'new_v', 'new_v_w_ple': 'new_v', 'new_v_final_norm': 'new_v'}


def _forward(args):
    return _fwd_reference(*[args[k] for k in FWD_PARAMS])


def _output_shape():
    def fwd():
        inp = _fwd_setup_inputs(0)
        return _fwd_reference(*[inp[k] for k in FWD_PARAMS])
    out = _jax.eval_shape(fwd)
    return out.shape, out.dtype

N_MICROBATCH = 1
ADAM_LR = 0.001
ADAM_B1 = 0.9
ADAM_B2 = 0.999
ADAM_EPS = 1e-08
ADAM_WD = 0.01
ADAM_STEP = 10
PER_EXAMPLE_BATCH_AXIS = {'x': 0, 'p': 1, 'positions': 0, 'loss_target': 0}
SHARED_INPUTS = []
_WEIGHT_DTYPES = {'attn_norm': _jnp.float32, 'w_in': _jnp.float32, 'q_a_norm': _jnp.float32, 'w_q_b': _jnp.float32, 'kv_a_norm': _jnp.float32, 'w_kv_b': _jnp.float32, 'conv_w': _jnp.float32, 'conv_b': _jnp.float32, 'w_rg': _jnp.float32, 'b_rg': _jnp.float32, 'w_ig': _jnp.float32, 'b_ig': _jnp.float32, 'lru_lambda': _jnp.float32, 'w_o_mla': _jnp.float32, 'w_o_lru': _jnp.float32, 'w_out': _jnp.float32, 'ple_norm': _jnp.float32, 'w_ple_gate': _jnp.float32, 'w_ple': _jnp.float32, 'final_norm': _jnp.float32}
MOMENT_SCALE = {'attn_norm': 3.025766e-02, 'w_in': 1.300778e-02, 'q_a_norm': 8.366749e-03, 'w_q_b': 3.356115e-03, 'kv_a_norm': 1.171895e-02, 'w_kv_b': 3.987957e-03, 'conv_w': 2.381285e-02, 'conv_b': 2.833499e-01, 'w_rg': 7.212498e-03, 'b_rg': 6.124204e-03, 'w_ig': 1.304205e-02, 'b_ig': 7.695682e-03, 'lru_lambda': 1.155698e-02, 'w_o_mla': 4.487477e-03, 'w_o_lru': 2.317626e-02, 'w_out': 2.280590e-02, 'ple_norm': 1.891027e-02, 'w_ple_gate': 1.841942e-02, 'w_ple': 4.731632e-02, 'final_norm': 3.194966e+01}


def _to_microbatches(a, axis):
    t = _jnp.moveaxis(a, axis, 0)
    t = t.reshape((N_MICROBATCH, t.shape[0] // N_MICROBATCH) + t.shape[1:])
    return _jnp.moveaxis(t, 1, axis + 1)


def setup_inputs(seed: int = 0) -> dict:
    inp = _fwd_setup_inputs(seed)
    key = _jax.random.fold_in(_jax.random.key(seed), 7919)
    shape, _ = _output_shape()
    out = dict(inp)
    out["loss_target"] = _jax.random.normal(_jax.random.fold_in(key, 0), shape, _jnp.float32)
    for i, name in enumerate(TWIN_WEIGHTS):
        w = inp[name].astype(_jnp.float32)
        if MOMENT_SCALE is None:
            s = _jnp.sqrt(_jnp.mean(_jnp.square(w)) + 1e-30)
        else:
            s = MOMENT_SCALE[name]
        km, kv = _jax.random.split(_jax.random.fold_in(key, i + 1))
        out[name] = w
        out["m_" + name] = s * _jax.random.normal(km, w.shape, _jnp.float32)
        out["v_" + name] = (s * s) * _jax.random.uniform(kv, w.shape, _jnp.float32, 0.5, 1.5)
    if N_MICROBATCH > 1:
        for name, axis in PER_EXAMPLE_BATCH_AXIS.items():
            out[name] = _to_microbatches(out[name], axis)
    return {'x': out['x'], 'p': out['p'], 'positions': out['positions'], 'attn_norm': out['attn_norm'], 'w_in': out['w_in'], 'q_a_norm': out['q_a_norm'], 'w_q_b': out['w_q_b'], 'kv_a_norm': out['kv_a_norm'], 'w_kv_b': out['w_kv_b'], 'conv_w': out['conv_w'], 'conv_b': out['conv_b'], 'w_rg': out['w_rg'], 'b_rg': out['b_rg'], 'w_ig': out['w_ig'], 'b_ig': out['b_ig'], 'lru_lambda': out['lru_lambda'], 'w_o_mla': out['w_o_mla'], 'w_o_lru': out['w_o_lru'], 'w_out': out['w_out'], 'ple_norm': out['ple_norm'], 'w_ple_gate': out['w_ple_gate'], 'w_ple': out['w_ple'], 'final_norm': out['final_norm'], 'loss_target': out['loss_target'], 'm_attn_norm': out['m_attn_norm'], 'm_w_in': out['m_w_in'], 'm_q_a_norm': out['m_q_a_norm'], 'm_w_q_b': out['m_w_q_b'], 'm_kv_a_norm': out['m_kv_a_norm'], 'm_w_kv_b': out['m_w_kv_b'], 'm_conv_w': out['m_conv_w'], 'm_conv_b': out['m_conv_b'], 'm_w_rg': out['m_w_rg'], 'm_b_rg': out['m_b_rg'], 'm_w_ig': out['m_w_ig'], 'm_b_ig': out['m_b_ig'], 'm_lru_lambda': out['m_lru_lambda'], 'm_w_o_mla': out['m_w_o_mla'], 'm_w_o_lru': out['m_w_o_lru'], 'm_w_out': out['m_w_out'], 'm_ple_norm': out['m_ple_norm'], 'm_w_ple_gate': out['m_w_ple_gate'], 'm_w_ple': out['m_w_ple'], 'm_final_norm': out['m_final_norm'], 'v_attn_norm': out['v_attn_norm'], 'v_w_in': out['v_w_in'], 'v_q_a_norm': out['v_q_a_norm'], 'v_w_q_b': out['v_w_q_b'], 'v_kv_a_norm': out['v_kv_a_norm'], 'v_w_kv_b': out['v_w_kv_b'], 'v_conv_w': out['v_conv_w'], 'v_conv_b': out['v_conv_b'], 'v_w_rg': out['v_w_rg'], 'v_b_rg': out['v_b_rg'], 'v_w_ig': out['v_w_ig'], 'v_b_ig': out['v_b_ig'], 'v_lru_lambda': out['v_lru_lambda'], 'v_w_o_mla': out['v_w_o_mla'], 'v_w_o_lru': out['v_w_o_lru'], 'v_w_out': out['v_w_out'], 'v_ple_norm': out['v_ple_norm'], 'v_w_ple_gate': out['v_w_ple_gate'], 'v_w_ple': out['v_w_ple'], 'v_final_norm': out['v_final_norm']}


def _loss(weights, diff, rest, loss_target):
    with _jax.named_scope("forward"):
        args = {**rest, TWIN_DIFF_INPUT: diff, **{k: w.astype(_WEIGHT_DTYPES[k]) for k, w in weights.items()}}
        y = _forward(args)
    with _jax.named_scope("loss_head"):
        err = _jnp.square(y.astype(_jnp.float32) - loss_target)
        return 0.5 * _jnp.sum(_jnp.mean(err, axis=-1)) if err.ndim else 0.5 * err


def _adamw(w, g, m, v):
    m = ADAM_B1 * m + (1.0 - ADAM_B1) * g
    v = ADAM_B2 * v + (1.0 - ADAM_B2) * _jnp.square(g)
    m_hat = m / (1.0 - ADAM_B1 ** ADAM_STEP)
    v_hat = v / (1.0 - ADAM_B2 ** ADAM_STEP)
    delta = -ADAM_LR * (m_hat / (_jnp.sqrt(v_hat) + ADAM_EPS) + ADAM_WD * w)
    return delta, m, v


def reference(x, p, positions, attn_norm, w_in, q_a_norm, w_q_b, kv_a_norm, w_kv_b, conv_w, conv_b, w_rg, b_rg, w_ig, b_ig, lru_lambda, w_o_mla, w_o_lru, w_out, ple_norm, w_ple_gate, w_ple, final_norm, loss_target, m_attn_norm, m_w_in, m_q_a_norm, m_w_q_b, m_kv_a_norm, m_w_kv_b, m_conv_w, m_conv_b, m_w_rg, m_b_rg, m_w_ig, m_b_ig, m_lru_lambda, m_w_o_mla, m_w_o_lru, m_w_out, m_ple_norm, m_w_ple_gate, m_w_ple, m_final_norm, v_attn_norm, v_w_in, v_q_a_norm, v_w_q_b, v_kv_a_norm, v_w_kv_b, v_conv_w, v_conv_b, v_w_rg, v_b_rg, v_w_ig, v_b_ig, v_lru_lambda, v_w_o_mla, v_w_o_lru, v_w_out, v_ple_norm, v_w_ple_gate, v_w_ple, v_final_norm):
    given = dict(x=x, p=p, positions=positions, attn_norm=attn_norm, w_in=w_in, q_a_norm=q_a_norm, w_q_b=w_q_b, kv_a_norm=kv_a_norm, w_kv_b=w_kv_b, conv_w=conv_w, conv_b=conv_b, w_rg=w_rg, b_rg=b_rg, w_ig=w_ig, b_ig=b_ig, lru_lambda=lru_lambda, w_o_mla=w_o_mla, w_o_lru=w_o_lru, w_out=w_out, ple_norm=ple_norm, w_ple_gate=w_ple_gate, w_ple=w_ple, final_norm=final_norm, loss_target=loss_target, m_attn_norm=m_attn_norm, m_w_in=m_w_in, m_q_a_norm=m_q_a_norm, m_w_q_b=m_w_q_b, m_kv_a_norm=m_kv_a_norm, m_w_kv_b=m_w_kv_b, m_conv_w=m_conv_w, m_conv_b=m_conv_b, m_w_rg=m_w_rg, m_b_rg=m_b_rg, m_w_ig=m_w_ig, m_b_ig=m_b_ig, m_lru_lambda=m_lru_lambda, m_w_o_mla=m_w_o_mla, m_w_o_lru=m_w_o_lru, m_w_out=m_w_out, m_ple_norm=m_ple_norm, m_w_ple_gate=m_w_ple_gate, m_w_ple=m_w_ple, m_final_norm=m_final_norm, v_attn_norm=v_attn_norm, v_w_in=v_w_in, v_q_a_norm=v_q_a_norm, v_w_q_b=v_w_q_b, v_kv_a_norm=v_kv_a_norm, v_w_kv_b=v_w_kv_b, v_conv_w=v_conv_w, v_conv_b=v_conv_b, v_w_rg=v_w_rg, v_b_rg=v_b_rg, v_w_ig=v_w_ig, v_b_ig=v_b_ig, v_lru_lambda=v_lru_lambda, v_w_o_mla=v_w_o_mla, v_w_o_lru=v_w_o_lru, v_w_out=v_w_out, v_ple_norm=v_ple_norm, v_w_ple_gate=v_w_ple_gate, v_w_ple=v_w_ple, v_final_norm=v_final_norm)
    weights = {n: given[n] for n in TWIN_WEIGHTS}
    shared = {n: given[n] for n in SHARED_INPUTS}
    per_example = {n: given[n] for n in ['x', 'p', 'positions']}
    grad_fn = _jax.value_and_grad(_loss, argnums=(0, 1))

    def one_microbatch(ex, loss_target):
        ex = dict(ex)
        diff = ex.pop(TWIN_DIFF_INPUT)
        return grad_fn(weights, diff, {**shared, **ex}, loss_target)

    if N_MICROBATCH == 1:
        loss, (grad_w, grad_x) = one_microbatch(per_example, given["loss_target"])
    else:
        def body(carry, xs):
            loss_sum, grad_sum = carry
            l_k, (gw_k, gx_k) = one_microbatch(xs[0], xs[1])
            with _jax.named_scope("update"):
                return (loss_sum + l_k, _jax.tree.map(_jnp.add, grad_sum, gw_k)), gx_k

        init = (_jnp.zeros((), _jnp.float32), _jax.tree.map(_jnp.zeros_like, weights))
        (loss, grad_w), grad_x = _jax.lax.scan(body, init, (per_example, given["loss_target"]))
    with _jax.named_scope("update"):
        delta_w, new_m, new_v = {}, {}, {}
        for n in TWIN_WEIGHTS:
            delta_w[n], new_m[n], new_v[n] = _adamw(weights[n], grad_w[n], given["m_" + n], given["v_" + n])
    return (loss, grad_x, *[grad_w[n] for n in TWIN_WEIGHTS], *[delta_w[n] for n in TWIN_WEIGHTS],
            *[new_m[n] for n in TWIN_WEIGHTS], *[new_v[n] for n in TWIN_WEIGHTS])
```

```python
import jax
import jax.numpy as jnp
from jax import lax
from jax.experimental import pallas as pl
from jax.experimental.pallas import tpu as pltpu

F32 = jnp.float32
BF16 = jnp.bfloat16
MESH = pl.DeviceIdType.MESH

CHUNK = 64
QK_NOPE = 128
QK_ROPE = 64
V_HEAD = 128
ROPE_THETA = 10000.0
CONV_K = 4
LRU_C = 8.0
LRU_BLOCK_DIM = 128
EPS = 1e-6
ADAM_LR = 0.001
ADAM_B1 = 0.9
ADAM_B2 = 0.999
ADAM_EPS = 1e-08
ADAM_WD = 0.01
ADAM_STEP = 10

N_CHIPS = 4
N_DEV = 8
LANES = 128
HEAD_PAD = 256
VMEM_LIMIT = 48 * 1024 * 1024

ATT_T = 512
ROW_T = 256
LRU_TT = 512
LRU_CB = 512
PACK_C = 512
PACK_ROWS = 1024
SMALL_ROWS = 512

NT_DIMS = (((1,), (1,)), ((), ()))


def _cp(sem):
    return pltpu.CompilerParams(dimension_semantics=sem, vmem_limit_bytes=VMEM_LIMIT)


def _pick(n, pref, align=LANES):
    if n <= pref:
        return n
    t = pref - pref % align
    while t >= align:
        if n % t == 0:
            return t
        t -= align
    return n


def _sig(x):
    return 1.0 / (1.0 + jnp.exp(-x))


def _mm(a, b, mode, name, out_dtype=F32, add=None, tm=512, tn=1024, tk=512):
    if mode == "nn":
        (M, K), (_, N) = a.shape, b.shape
    elif mode == "nt":
        (M, K), (N, _) = a.shape, b.shape
    else:
        (K, M), (_, N) = a.shape, b.shape
    tm, tn = _pick(M, tm), _pick(N, tn)
    tk = K if (mode != "tn" and K <= 2048) else _pick(K, tk)
    nm, nn, nk = M // tm, N // tn, K // tk
    i_outer = nm * b.size * b.dtype.itemsize <= nn * a.size * a.dtype.itemsize

    def ij(g0, g1):
        return (g0, g1) if i_outer else (g1, g0)

    def amap(g0, g1, k):
        i, _ = ij(g0, g1)
        return (k, i) if mode == "tn" else (i, k)

    def bmap(g0, g1, k):
        _, j = ij(g0, g1)
        return (j, k) if mode == "nt" else (k, j)

    def omap(g0, g1, k):
        return ij(g0, g1)

    ablk = (tk, tm) if mode == "tn" else (tm, tk)
    bblk = (tn, tk) if mode == "nt" else (tk, tn)

    def body(*refs):
        if add is None:
            a_ref, b_ref, o_ref = refs[:3]
            add_ref = None
        else:
            a_ref, b_ref, add_ref, o_ref = refs[:4]
        x = a_ref[...].astype(BF16)
        y = b_ref[...].astype(BF16)
        if mode == "nn":
            p = jnp.dot(x, y, preferred_element_type=F32)
        elif mode == "nt":
            p = lax.dot_general(x, y, NT_DIMS, preferred_element_type=F32)
        else:
            p = jnp.dot(x.T, y, preferred_element_type=F32)
        if nk == 1:
            if add_ref is not None:
                p = p + add_ref[...]
            o_ref[...] = p.astype(out_dtype)
        else:
            acc = refs[-1]
            k = pl.program_id(2)

            @pl.when(k == 0)
            def _():
                acc[...] = p if add_ref is None else p + add_ref[...]

            @pl.when(k > 0)
            def _():
                acc[...] += p

            @pl.when(k == nk - 1)
            def _():
                o_ref[...] = acc[...].astype(out_dtype)

    in_specs = [pl.BlockSpec(ablk, amap), pl.BlockSpec(bblk, bmap)]
    args = [a, b]
    if add is not None:
        in_specs.append(pl.BlockSpec((tm, tn), omap))
        args.append(add)
    grid = (nm, nn, nk) if i_outer else (nn, nm, nk)
    return pl.pallas_call(
        body, grid=grid, in_specs=in_specs, out_specs=pl.BlockSpec((tm, tn), omap),
        out_shape=jax.ShapeDtypeStruct((M, N), out_dtype),
        scratch_shapes=[pltpu.VMEM((tm, tn), F32)] if nk > 1 else [],
        compiler_params=_cp(("parallel", "parallel", "arbitrary")), name=name,
    )(*args)


def _rows(cols, i=0):
    def make(tm):
        return pl.BlockSpec((tm, cols), lambda r: (r, i))
    return make


def _par(cols):
    return pl.BlockSpec((1, cols), lambda r: (0, 0))


def _rms_fwd(x, g, name):
    S, D = x.shape
    tm = _pick(S, ROW_T, 8)

    def body(x_ref, g_ref, o_ref):
        xv = x_ref[...]
        rs = lax.rsqrt(jnp.mean(xv * xv, axis=-1, keepdims=True) + EPS)
        o_ref[...] = (xv * rs * g_ref[...]).astype(BF16)

    return pl.pallas_call(
        body, grid=(S // tm,), in_specs=[_rows(D)(tm), _par(D)], out_specs=_rows(D)(tm),
        out_shape=jax.ShapeDtypeStruct((S, D), BF16), compiler_params=_cp(("parallel",)), name=name,
    )(x, g.reshape(1, D))


def _rms_bwd_math(xv, g, dy):
    rs = lax.rsqrt(jnp.mean(xv * xv, axis=-1, keepdims=True) + EPS)
    xh = xv * rs
    dg = jnp.sum(dy * xh, axis=0, keepdims=True)
    dyg = dy * g
    dx = rs * (dyg - xh * jnp.mean(dyg * xh, axis=-1, keepdims=True))
    return dx, dg


def _accum(ref, val, first):
    @pl.when(first)
    def _():
        ref[...] = val

    @pl.when(jnp.logical_not(first))
    def _():
        ref[...] += val


def _rms_bwd(x, g, dy, dres, name):
    S, D = x.shape
    tm = _pick(S, ROW_T, 8)

    def body(x_ref, g_ref, dy_ref, dres_ref, dx_ref, dg_ref):
        dx, dg = _rms_bwd_math(x_ref[...], g_ref[...], dy_ref[...])
        dx_ref[...] = dres_ref[...] + dx
        _accum(dg_ref, dg, pl.program_id(0) == 0)

    return pl.pallas_call(
        body, grid=(S // tm,), in_specs=[_rows(D)(tm), _par(D), _rows(D)(tm), _rows(D)(tm)],
        out_specs=[_rows(D)(tm), _par(D)],
        out_shape=[jax.ShapeDtypeStruct((S, D), F32), jax.ShapeDtypeStruct((1, D), F32)],
        compiler_params=_cp(("arbitrary",)), name=name,
    )(x, g.reshape(1, D), dy, dres)


def _rope_tables(pos, inv_freq):
    S = pos.shape[0]
    tm = _pick(S, 512, 8)
    half = QK_ROPE // 2
    invf = jnp.concatenate([inv_freq, inv_freq, jnp.zeros((LANES - QK_ROPE,), F32)]).reshape(1, LANES)

    def body(pos_ref, f_ref, c_ref, sa_ref, sb_ref):
        ang = pos_ref[...].astype(F32) * f_ref[...]
        lane = lax.broadcasted_iota(jnp.int32, ang.shape, 1)
        c, s = jnp.cos(ang), jnp.sin(ang)
        c_ref[...] = jnp.where(lane < QK_ROPE, c, 0.0)
        sa_ref[...] = jnp.where(lane < half, -s, 0.0)
        sb_ref[...] = jnp.where((lane >= half) & (lane < QK_ROPE), s, 0.0)

    tab = jax.ShapeDtypeStruct((S, LANES), F32)
    return pl.pallas_call(
        body, grid=(S // tm,), in_specs=[pl.BlockSpec((tm, 1), lambda r: (r, 0)), _par(LANES)],
        out_specs=[_rows(LANES)(tm)] * 3, out_shape=[tab] * 3, compiler_params=_cp(("parallel",)), name="rope_tables",
    )(pos.reshape(S, 1), invf)


def _rope(x, c, sa, sb):
    return x * c + pltpu.roll(x, LANES - QK_ROPE // 2, 1) * sa + pltpu.roll(x, QK_ROPE // 2, 1) * sb


def _rope_t(d, c, sa, sb):
    return d * c + pltpu.roll(d * sa, QK_ROPE // 2, 1) + pltpu.roll(d * sb, LANES - QK_ROPE // 2, 1)


def _latent_fwd(zsm, gq, gkv, tabs, ql, kvl):
    S = zsm.shape[0]
    tm = _pick(S, ROW_T, 8)
    kr_blk = (ql + kvl) // LANES

    def body(q_ref, kv_ref, kr_ref, gq_ref, gkv_ref, c_ref, sa_ref, sb_ref, qn_ref, kvn_ref, kro_ref):
        for src, g_ref, dst in ((q_ref, gq_ref, qn_ref), (kv_ref, gkv_ref, kvn_ref)):
            v = src[...]
            rs = lax.rsqrt(jnp.mean(v * v, axis=-1, keepdims=True) + EPS)
            dst[...] = (v * rs * g_ref[...]).astype(BF16)
        kro_ref[...] = _rope(kr_ref[...], c_ref[...], sa_ref[...], sb_ref[...]).astype(BF16)

    return pl.pallas_call(
        body, grid=(S // tm,),
        in_specs=[_rows(ql, 0)(tm), _rows(kvl, 1)(tm), _rows(LANES, kr_blk)(tm), _par(ql), _par(kvl)] + [_rows(LANES)(tm)] * 3,
        out_specs=[_rows(ql)(tm), _rows(kvl)(tm), _rows(LANES)(tm)],
        out_shape=[jax.ShapeDtypeStruct((S, ql), BF16), jax.ShapeDtypeStruct((S, kvl), BF16), jax.ShapeDtypeStruct((S, LANES), BF16)],
        compiler_params=_cp(("parallel",)), name="latent_fwd",
    )(zsm, zsm, zsm, gq.reshape(1, ql), gkv.reshape(1, kvl), *tabs)


def _latent_bwd(zsm, gq, gkv, tabs, d_qn, d_kvn, dkr, ql, kvl):
    S, W = zsm.shape
    H = dkr.shape[0]
    tm = _pick(S, ROW_T, 8)
    kr_blk = (ql + kvl) // LANES

    def body(q_ref, kv_ref, gq_ref, gkv_ref, c_ref, sa_ref, sb_ref, dqn_ref, dkvn_ref, dkr_ref, dz_ref, dgq_ref, dgkv_ref):
        first = pl.program_id(0) == 0
        dq, dgq = _rms_bwd_math(q_ref[...], gq_ref[...], dqn_ref[...])
        dkv, dgkv = _rms_bwd_math(kv_ref[...], gkv_ref[...], dkvn_ref[...])
        dk = dkr_ref[0]
        for h in range(1, H):
            dk = dk + dkr_ref[h]
        dz_ref[:, 0:ql] = dq.astype(BF16)
        dz_ref[:, ql:ql + kvl] = dkv.astype(BF16)
        dz_ref[:, ql + kvl:] = _rope_t(dk, c_ref[...], sa_ref[...], sb_ref[...]).astype(BF16)
        _accum(dgq_ref, dgq, first)
        _accum(dgkv_ref, dgkv, first)

    return pl.pallas_call(
        body, grid=(S // tm,),
        in_specs=[_rows(ql, 0)(tm), _rows(kvl, 1)(tm), _par(ql), _par(kvl)] + [_rows(LANES)(tm)] * 3
        + [_rows(ql)(tm), _rows(kvl)(tm), pl.BlockSpec((H, tm, LANES), lambda r: (0, r, 0))],
        out_specs=[_rows(W)(tm), _par(ql), _par(kvl)],
        out_shape=[jax.ShapeDtypeStruct((S, W), BF16), jax.ShapeDtypeStruct((1, ql), F32), jax.ShapeDtypeStruct((1, kvl), F32)],
        compiler_params=_cp(("arbitrary",)), name="latent_bwd",
    )(zsm, zsm, gq.reshape(1, ql), gkv.reshape(1, kvl), *tabs, d_qn, d_kvn, dkr)


def _q_rope(q, tabs, transpose, name):
    S, W = q.shape
    H = W // HEAD_PAD
    tm = _pick(S, ROW_T, 8)
    fn = _rope_t if transpose else _rope

    def body(q_ref, c_ref, sa_ref, sb_ref, o_ref):
        c, sa, sb = c_ref[...], sa_ref[...], sb_ref[...]
        for h in range(H):
            lo = h * HEAD_PAD
            o_ref[:, lo:lo + QK_NOPE] = q_ref[:, lo:lo + QK_NOPE].astype(BF16)
            o_ref[:, lo + QK_NOPE:lo + HEAD_PAD] = fn(q_ref[:, lo + QK_NOPE:lo + HEAD_PAD], c, sa, sb).astype(BF16)

    return pl.pallas_call(
        body, grid=(S // tm,), in_specs=[_rows(W)(tm)] + [_rows(LANES)(tm)] * 3, out_specs=_rows(W)(tm),
        out_shape=jax.ShapeDtypeStruct((S, W), BF16), compiler_params=_cp(("parallel",)), name=name,
    )(q, *tabs)


def _merge_fwd(zbig, y_mla, y_lru, D):
    S = y_mla.shape[0]
    tm = _pick(S, ROW_T, 8)

    def body(mm_ref, ml_ref, ym_ref, yl_ref, o_ref):
        o_ref[...] = (_sig(mm_ref[...]) * ym_ref[...] + _sig(ml_ref[...]) * yl_ref[...]).astype(BF16)

    return pl.pallas_call(
        body, grid=(S // tm,), in_specs=[_rows(D, 3)(tm), _rows(D, 4)(tm), _rows(D)(tm), _rows(D)(tm)], out_specs=_rows(D)(tm),
        out_shape=jax.ShapeDtypeStruct((S, D), BF16), compiler_params=_cp(("parallel",)), name="merge_fwd",
    )(zbig, zbig, y_mla, y_lru)


def _merge_bwd(zbig, y_mla, y_lru, d_merged, D):
    S = y_mla.shape[0]
    tm = _pick(S, ROW_T, 8)

    def body(mm_ref, ml_ref, ym_ref, yl_ref, d_ref, dym_ref, dyl_ref, dmm_ref, dml_ref):
        d = d_ref[...]
        sm, sl = _sig(mm_ref[...]), _sig(ml_ref[...])
        dym_ref[...] = (d * sm).astype(BF16)
        dyl_ref[...] = (d * sl).astype(BF16)
        dmm_ref[...] = (d * ym_ref[...] * sm * (1.0 - sm)).astype(BF16)
        dml_ref[...] = (d * yl_ref[...] * sl * (1.0 - sl)).astype(BF16)

    o = jax.ShapeDtypeStruct((S, D), BF16)
    return pl.pallas_call(
        body, grid=(S // tm,), in_specs=[_rows(D, 3)(tm), _rows(D, 4)(tm)] + [_rows(D)(tm)] * 3, out_specs=[_rows(D)(tm)] * 4,
        out_shape=[o] * 4, compiler_params=_cp(("parallel",)), name="merge_bwd",
    )(zbig, zbig, y_mla, y_lru, d_merged)


def _ple_fwd(x1, pe, pg):
    S, D = x1.shape
    tm = _pick(S, ROW_T, 8)

    def body(x_ref, pe_ref, pg_ref, o_ref):
        o_ref[...] = x_ref[...] + pe_ref[...] * _sig(pg_ref[...])

    return pl.pallas_call(
        body, grid=(S // tm,), in_specs=[_rows(D)(tm)] * 3, out_specs=_rows(D)(tm),
        out_shape=jax.ShapeDtypeStruct((S, D), F32), compiler_params=_cp(("parallel",)), name="ple_fwd",
    )(x1, pe, pg)


def _ple_bwd(dx2, pe, pg):
    S, D = dx2.shape
    tm = _pick(S, ROW_T, 8)

    def body(d_ref, pe_ref, pg_ref, dpe_ref, dpg_ref):
        d = d_ref[...]
        s = _sig(pg_ref[...])
        dpe_ref[...] = (d * s).astype(BF16)
        dpg_ref[...] = (d * pe_ref[...] * s * (1.0 - s)).astype(BF16)

    o = jax.ShapeDtypeStruct((S, D), BF16)
    return pl.pallas_call(
        body, grid=(S // tm,), in_specs=[_rows(D)(tm)] * 3, out_specs=[_rows(D)(tm)] * 2, out_shape=[o] * 2,
        compiler_params=_cp(("parallel",)), name="ple_bwd",
    )(dx2, pe, pg)


def _loss_head(x, g, target):
    S, D = x.shape
    tm = _pick(S, ROW_T, 8)

    def body(x_ref, g_ref, t_ref, dx_ref, dg_ref, loss_ref):
        first = pl.program_id(0) == 0
        xv, gv = x_ref[...], g_ref[...]
        rs = lax.rsqrt(jnp.mean(xv * xv, axis=-1, keepdims=True) + EPS)
        e = xv * rs * gv - t_ref[...]
        part = 0.5 * jnp.sum(jnp.mean(e * e, axis=-1, keepdims=True), axis=0, keepdims=True)
        dx, dg = _rms_bwd_math(xv, gv, e * (1.0 / D))
        dx_ref[...] = dx
        _accum(dg_ref, dg, first)
        _accum(loss_ref, jnp.broadcast_to(part, (1, LANES)), first)

    return pl.pallas_call(
        body, grid=(S // tm,), in_specs=[_rows(D)(tm), _par(D), _rows(D)(tm)], out_specs=[_rows(D)(tm), _par(D), _par(LANES)],
        out_shape=[jax.ShapeDtypeStruct((S, D), F32), jax.ShapeDtypeStruct((1, D), F32), jax.ShapeDtypeStruct((1, LANES), F32)],
        compiler_params=_cp(("arbitrary",)), name="loss_head",
    )(x, g.reshape(1, D), target)


def _attn_gate_bwd(d_a, o, zbig, H):
    S, W = o.shape
    tm = _pick(S, ROW_T, 8)

    def body(da_ref, o_ref, g_ref, do_ref, dg_ref, dl_ref):
        da, ov, g = da_ref[...], o_ref[...], g_ref[...]
        s = _sig(g)
        do = da * g * s
        do_ref[...] = do.astype(BF16)
        dg_ref[...] = (da * ov * s * (1.0 + g * (1.0 - s))).astype(BF16)
        prod = do * ov
        for h in range(H):
            r = jnp.sum(prod[:, h * V_HEAD:(h + 1) * V_HEAD], axis=-1, keepdims=True)
            dl_ref[h] = jnp.broadcast_to(r, (tm, LANES))

    return pl.pallas_call(
        body, grid=(S // tm,), in_specs=[_rows(W)(tm), _rows(W)(tm), _rows(W, 0)(tm)],
        out_specs=[_rows(W)(tm), _rows(W)(tm), pl.BlockSpec((H, tm, LANES), lambda r: (0, r, 0))],
        out_shape=[jax.ShapeDtypeStruct((S, W), BF16), jax.ShapeDtypeStruct((S, W), BF16), jax.ShapeDtypeStruct((H, S, LANES), F32)],
        compiler_params=_cp(("parallel",)), name="attn_gate_bwd",
    )(d_a, o, zbig)


def _chunk_mask(t, q_rows):
    r = lax.broadcasted_iota(jnp.int32, (t, t), 0) // CHUNK
    c = lax.broadcasted_iota(jnp.int32, (t, t), 1) // CHUNK
    return (c <= r) if q_rows else (r <= c)


def _attn_fwd(q, kv, kr, zbig, H):
    S = q.shape[0]
    t = _pick(S, ATT_T)
    nq = S // t
    scale = 1.0 / (QK_NOPE + QK_ROPE) ** 0.5

    def body(q_ref, kn_ref, v_ref, kr_ref, g_ref, o_ref, a_ref, lse_ref, m_scr, l_scr, acc_scr):
        i = pl.program_id(1)
        qv = q_ref[...]
        m_scr[...] = jnp.full((t, LANES), -1e30, F32)
        l_scr[...] = jnp.zeros((t, LANES), F32)
        acc_scr[...] = jnp.zeros((t, V_HEAD), F32)

        def block(j, masked):
            ks = pl.multiple_of(j * t, t)
            k = jnp.concatenate([kn_ref[pl.ds(ks, t), :], kr_ref[pl.ds(ks, t), :]], axis=1)
            s = lax.dot_general(qv, k, NT_DIMS, preferred_element_type=F32) * scale
            if masked:
                s = jnp.where(_chunk_mask(t, True), s, -1e30)
            m_prev = m_scr[...]
            m_next = jnp.maximum(m_prev, jnp.max(s, axis=1, keepdims=True))
            p = jnp.exp(s - jnp.tile(m_next, (1, t // LANES)))
            alpha = jnp.exp(m_prev - m_next)
            l_scr[...] = alpha * l_scr[...] + jnp.sum(p, axis=1, keepdims=True)
            m_scr[...] = m_next
            acc_scr[...] = acc_scr[...] * alpha + jnp.dot(p.astype(BF16), v_ref[pl.ds(ks, t), :], preferred_element_type=F32)

        def loop(j, carry):
            block(j, False)
            return carry

        lax.fori_loop(0, i, loop, 0)
        block(i, True)
        l = l_scr[...]
        ov = acc_scr[...] / l
        g = g_ref[...]
        o_ref[...] = ov
        a_ref[...] = (ov * g * _sig(g)).astype(BF16)
        lse_ref[0] = m_scr[...] + jnp.log(l)

    head_col = lambda w, off: pl.BlockSpec((S, w), lambda h, i: (0, 2 * h + off))
    return pl.pallas_call(
        body, grid=(H, nq),
        in_specs=[pl.BlockSpec((t, HEAD_PAD), lambda h, i: (i, h)), head_col(QK_NOPE, 0), head_col(V_HEAD, 1),
                  pl.BlockSpec((S, LANES), lambda h, i: (0, 0)), pl.BlockSpec((t, V_HEAD), lambda h, i: (i, h))],
        out_specs=[pl.BlockSpec((t, V_HEAD), lambda h, i: (i, h)), pl.BlockSpec((t, V_HEAD), lambda h, i: (i, h)),
                   pl.BlockSpec((1, t, LANES), lambda h, i: (h, i, 0))],
        out_shape=[jax.ShapeDtypeStruct((S, H * V_HEAD), F32), jax.ShapeDtypeStruct((S, H * V_HEAD), BF16),
                   jax.ShapeDtypeStruct((H, S, LANES), F32)],
        scratch_shapes=[pltpu.VMEM((t, LANES), F32), pltpu.VMEM((t, LANES), F32), pltpu.VMEM((t, V_HEAD), F32)],
        compiler_params=_cp(("parallel", "arbitrary")), name="attn_fwd",
    )(q, kv, kv, kr, zbig)


def _attn_dq(q, kv, kr, do, lse, delta, H):
    S = q.shape[0]
    t = _pick(S, ATT_T)
    nq = S // t
    scale = 1.0 / (QK_NOPE + QK_ROPE) ** 0.5

    def body(q_ref, kn_ref, v_ref, kr_ref, do_ref, lse_ref, dl_ref, dq_ref, acc_scr):
        i = pl.program_id(1)
        qv, dov = q_ref[...], do_ref[...]
        lse_t = jnp.tile(lse_ref[0], (1, t // LANES))
        dl_t = jnp.tile(dl_ref[0], (1, t // LANES))
        acc_scr[...] = jnp.zeros((t, HEAD_PAD), F32)

        def block(j, masked):
            ks = pl.multiple_of(j * t, t)
            k = jnp.concatenate([kn_ref[pl.ds(ks, t), :], kr_ref[pl.ds(ks, t), :]], axis=1)
            s = lax.dot_general(qv, k, NT_DIMS, preferred_element_type=F32) * scale
            p = jnp.exp(s - lse_t)
            if masked:
                p = jnp.where(_chunk_mask(t, True), p, 0.0)
            dp = lax.dot_general(dov, v_ref[pl.ds(ks, t), :], NT_DIMS, preferred_element_type=F32)
            ds = p * (dp - dl_t) * scale
            acc_scr[...] += jnp.dot(ds.astype(BF16), k, preferred_element_type=F32)

        def loop(j, carry):
            block(j, False)
            return carry

        lax.fori_loop(0, i, loop, 0)
        block(i, True)
        dq_ref[...] = acc_scr[...]

    head_col = lambda w, off: pl.BlockSpec((S, w), lambda h, i: (0, 2 * h + off))
    stat = pl.BlockSpec((1, t, LANES), lambda h, i: (h, i, 0))
    return pl.pallas_call(
        body, grid=(H, nq),
        in_specs=[pl.BlockSpec((t, HEAD_PAD), lambda h, i: (i, h)), head_col(QK_NOPE, 0), head_col(V_HEAD, 1),
                  pl.BlockSpec((S, LANES), lambda h, i: (0, 0)), pl.BlockSpec((t, V_HEAD), lambda h, i: (i, h)), stat, stat],
        out_specs=pl.BlockSpec((t, HEAD_PAD), lambda h, i: (i, h)),
        out_shape=jax.ShapeDtypeStruct((S, H * HEAD_PAD), F32),
        scratch_shapes=[pltpu.VMEM((t, HEAD_PAD), F32)],
        compiler_params=_cp(("parallel", "arbitrary")), name="attn_dq",
    )(q, kv, kv, kr, do, lse, delta)


def _attn_dkv(q, kv, kr, do, lse_row, delta_row, H):
    S = q.shape[0]
    t = _pick(S, ATT_T)
    nk = S // t
    scale = 1.0 / (QK_NOPE + QK_ROPE) ** 0.5

    def body(kn_ref, v_ref, kr_ref, q_ref, do_ref, lse_ref, dl_ref, dkv_ref, dkr_ref, dk_scr, dv_scr):
        j = pl.program_id(1)
        k = jnp.concatenate([kn_ref[...], kr_ref[...]], axis=1)
        vv = v_ref[...]
        dk_scr[...] = jnp.zeros((t, HEAD_PAD), F32)
        dv_scr[...] = jnp.zeros((t, V_HEAD), F32)

        def block(i, masked):
            qs = pl.multiple_of(i * t, t)
            qv = q_ref[pl.ds(qs, t), :]
            dov = do_ref[pl.ds(qs, t), :]
            st = lax.dot_general(k, qv, NT_DIMS, preferred_element_type=F32) * scale
            pt = jnp.exp(st - lse_ref[0, :, pl.ds(qs, t)])
            if masked:
                pt = jnp.where(_chunk_mask(t, False), pt, 0.0)
            dv_scr[...] += jnp.dot(pt.astype(BF16), dov, preferred_element_type=F32)
            dpt = lax.dot_general(vv, dov, NT_DIMS, preferred_element_type=F32)
            dst = pt * (dpt - dl_ref[0, :, pl.ds(qs, t)]) * scale
            dk_scr[...] += jnp.dot(dst.astype(BF16), qv, preferred_element_type=F32)

        block(j, True)

        def loop(i, carry):
            block(i, False)
            return carry

        lax.fori_loop(j + 1, nk, loop, 0)
        dk = dk_scr[...]
        dkv_ref[:, 0:QK_NOPE] = dk[:, 0:QK_NOPE].astype(BF16)
        dkv_ref[:, QK_NOPE:] = dv_scr[...].astype(BF16)
        dkr_ref[0] = dk[:, QK_NOPE:]

    tile_col = lambda w, off: pl.BlockSpec((t, w), lambda h, j: (j, 2 * h + off))
    row = pl.BlockSpec((1, 1, S), lambda h, j: (h, 0, 0))
    return pl.pallas_call(
        body, grid=(H, nk),
        in_specs=[tile_col(QK_NOPE, 0), tile_col(V_HEAD, 1), pl.BlockSpec((t, LANES), lambda h, j: (j, 0)),
                  pl.BlockSpec((S, HEAD_PAD), lambda h, j: (0, h)), pl.BlockSpec((S, V_HEAD), lambda h, j: (0, h)), row, row],
        out_specs=[pl.BlockSpec((t, HEAD_PAD), lambda h, j: (j, h)), pl.BlockSpec((1, t, LANES), lambda h, j: (h, j, 0))],
        out_shape=[jax.ShapeDtypeStruct((S, H * HEAD_PAD), BF16), jax.ShapeDtypeStruct((H, S, LANES), F32)],
        scratch_shapes=[pltpu.VMEM((t, HEAD_PAD), F32), pltpu.VMEM((t, V_HEAD), F32)],
        compiler_params=_cp(("parallel", "arbitrary")), name="attn_dkv",
    )(kv, kv, kr, q, do, lse_row, delta_row)


def _shift_down(x, prev8, s):
    rx = pltpu.roll(x, s, 0)
    rp = pltpu.roll(prev8, s, 0)
    rows = lax.broadcasted_iota(jnp.int32, rp.shape, 0)
    return jnp.concatenate([jnp.where(rows < s, rp, rx[:8]), rx[8:]], axis=0)


def _shift_up(x, next8, s):
    n = x.shape[0]
    rx = pltpu.roll(x, n - s, 0)
    rn = pltpu.roll(next8, 8 - s, 0)
    rows = lax.broadcasted_iota(jnp.int32, rn.shape, 0)
    return jnp.concatenate([rx[:n - 8], jnp.where(rows >= 8 - s, rn, rx[n - 8:])], axis=0)


def _scan_rows(a, b, up):
    n = a.shape[0]
    rows = lax.broadcasted_iota(jnp.int32, a.shape, 0)
    d = 1
    while d < n:
        keep = (rows < n - d) if up else (rows >= d)
        sh = n - d if up else d
        a_s = jnp.where(keep, pltpu.roll(a, sh, 0), 1.0)
        b_s = jnp.where(keep, pltpu.roll(b, sh, 0), 0.0)
        b = a * b_s + b
        a = a * a_s
        d *= 2
    return a, b


def _log1p(e):
    u = 1.0 + e
    return jnp.where(u == 1.0, e, jnp.log(u) * (e / (u - 1.0)))


def _lru_pre(u, prev8, cw_ref, cb_ref, wr_ref, br_ref, wi_ref, bi_ref, lam_ref):
    us = [u, _shift_down(u, prev8, 1), _shift_down(u, prev8, 2), _shift_down(u, prev8, 3)]
    xc = cb_ref[...] + cw_ref[3:4, :] * us[0] + cw_ref[2:3, :] * us[1] + cw_ref[1:2, :] * us[2] + cw_ref[0:1, :] * us[3]
    x16 = xc.astype(BF16)
    nb = xc.shape[1] // LRU_BLOCK_DIM
    blk = lambda k: slice(k * LRU_BLOCK_DIM, (k + 1) * LRU_BLOCK_DIM)
    pr = jnp.concatenate([jnp.dot(x16[:, blk(k)], wr_ref[k].astype(BF16), preferred_element_type=F32) for k in range(nb)], axis=1)
    pi = jnp.concatenate([jnp.dot(x16[:, blk(k)], wi_ref[k].astype(BF16), preferred_element_type=F32) for k in range(nb)], axis=1)
    r = _sig(pr + br_ref[...])
    i = _sig(pi + bi_ref[...])
    nlam = -lam_ref[...]
    sp = jnp.maximum(nlam, 0.0) + _log1p(jnp.exp(-jnp.abs(nlam)))
    log_a = (-LRU_C * r) * sp
    a = jnp.exp(log_a)
    mult = jnp.sqrt(-jnp.tanh(log_a) * (a * a + 1.0))
    return us, xc, x16, r, i, sp, a, mult


def _lru_specs(D, cb, tt, nT, rev):
    nb = cb // LRU_BLOCK_DIM
    tmap = (lambda t: nT - 1 - t) if rev else (lambda t: t)
    ncb = D // cb

    def tile(piece):
        return pl.BlockSpec((tt, cb), lambda c, t: (tmap(t), piece * ncb + c))

    def halo(piece):
        return pl.BlockSpec((8, cb), lambda c, t: (jnp.maximum(tmap(t) * (tt // 8) - 1, 0), piece * ncb + c))

    par = lambda rows: pl.BlockSpec((rows, cb), lambda c, t: (0, c))
    wblk = pl.BlockSpec((nb, LRU_BLOCK_DIM, LRU_BLOCK_DIM), lambda c, t: (c, 0, 0))
    return tile, halo, par, wblk, tmap


def _lru_fwd(zbig, cw, cbias, wr, br, wi, bi, lam, D):
    S = zbig.shape[0]
    tt, cb = _pick(S, LRU_TT, 8), _pick(D, LRU_CB)
    nT = S // tt
    tile, halo, par, wblk, _ = _lru_specs(D, cb, tt, nT, False)

    def body(u_ref, up_ref, g_ref, cw_ref, cb_ref, wr_ref, br_ref, wi_ref, bi_ref, lam_ref, h_ref, al_ref, carry):
        t = pl.program_id(1)
        prev8 = jnp.where(t > 0, up_ref[...], 0.0)
        _, xc, _, _, i, _, a, mult = _lru_pre(u_ref[...], prev8, cw_ref, cb_ref, wr_ref, br_ref, wi_ref, bi_ref, lam_ref)
        pa, hb = _scan_rows(a, mult * (i * xc), False)
        h0 = jnp.where(t > 0, carry[7:8, :], 0.0)
        h = hb + pa * h0
        h_ref[...] = h
        carry[...] = h[tt - 8:, :]
        g = g_ref[...]
        al_ref[...] = (h * g * _sig(g)).astype(BF16)

    return pl.pallas_call(
        body, grid=(D // cb, nT),
        in_specs=[tile(1), halo(1), tile(2), par(CONV_K), par(1), wblk, par(1), wblk, par(1), par(1)],
        out_specs=[tile(0), tile(0)],
        out_shape=[jax.ShapeDtypeStruct((S, D), F32), jax.ShapeDtypeStruct((S, D), BF16)],
        scratch_shapes=[pltpu.VMEM((8, cb), F32)],
        compiler_params=_cp(("parallel", "arbitrary")), name="lru_fwd",
    )(zbig, zbig, zbig, cw, cbias.reshape(1, D), wr, br.reshape(1, D), wi, bi.reshape(1, D), lam.reshape(1, D))


def _lru_bwd(zbig, h, d_al, cw, cbias, wr, br, wi, bi, lam, D):
    S = zbig.shape[0]
    tt, cb = _pick(S, LRU_TT, 8), _pick(D, LRU_CB)
    nT = S // tt
    nb = cb // LRU_BLOCK_DIM
    tile, halo, par, wblk, tmap = _lru_specs(D, cb, tt, nT, True)

    def body(u_ref, up_ref, g_ref, h_ref, hp_ref, dal_ref, cw_ref, cb_ref, wr_ref, br_ref, wi_ref, bi_ref, lam_ref,
             du_ref, dg_ref, dcw_ref, dcb_ref, dwr_ref, dbr_ref, dwi_ref, dbi_ref, dlam_ref, g_car, a_car, x_car):
        step = pl.program_id(1)
        first = step == 0
        t = nT - 1 - step
        prev8 = jnp.where(t > 0, up_ref[...], 0.0)
        us, xc, x16, r, i, sp, a, mult = _lru_pre(u_ref[...], prev8, cw_ref, cb_ref, wr_ref, br_ref, wi_ref, bi_ref, lam_ref)
        hv = h_ref[...]
        h_m1 = _shift_down(hv, jnp.where(t > 0, hp_ref[...], 0.0), 1)
        g, dal = g_ref[...], dal_ref[...]
        sg = _sig(g)
        dg_ref[...] = (dal * hv * sg * (1.0 + g * (1.0 - sg))).astype(BF16)
        dh = dal * g * sg
        coef = _shift_up(a, jnp.where(first, 0.0, a_car[...]), 1)
        pa, gb = _scan_rows(coef, dh, True)
        G = gb + pa * jnp.where(first, 0.0, g_car[0:1, :])
        g_car[...] = G[:8]
        a_car[...] = a[:8]
        da = G * h_m1
        ixc = i * xc
        dixc = G * mult
        dlog = da * a - (G * ixc) * (a * a) / mult
        dpr = dlog * (-LRU_C * sp) * r * (1.0 - r)
        dpi = dixc * xc * i * (1.0 - i)
        dxc = dixc * i
        dsp = jnp.sum(dlog * (-LRU_C) * r, axis=0, keepdims=True)
        dpr16, dpi16 = dpr.astype(BF16), dpi.astype(BF16)
        blk = lambda k: slice(k * LRU_BLOCK_DIM, (k + 1) * LRU_BLOCK_DIM)
        back = []
        for k in range(nb):
            xk = x16[:, blk(k)].T
            dwr_k = jnp.dot(xk, dpr16[:, blk(k)], preferred_element_type=F32)
            dwi_k = jnp.dot(xk, dpi16[:, blk(k)], preferred_element_type=F32)

            @pl.when(first)
            def _():
                dwr_ref[k] = dwr_k
                dwi_ref[k] = dwi_k

            @pl.when(jnp.logical_not(first))
            def _():
                dwr_ref[k] += dwr_k
                dwi_ref[k] += dwi_k

            back.append(lax.dot_general(dpr16[:, blk(k)], wr_ref[k].astype(BF16), NT_DIMS, preferred_element_type=F32)
                        + lax.dot_general(dpi16[:, blk(k)], wi_ref[k].astype(BF16), NT_DIMS, preferred_element_type=F32))
        dxc = dxc + jnp.concatenate(back, axis=1)
        _accum(dbr_ref, jnp.sum(dpr, axis=0, keepdims=True), first)
        _accum(dbi_ref, jnp.sum(dpi, axis=0, keepdims=True), first)
        _accum(dlam_ref, dsp * (-_sig(-lam_ref[...])), first)
        _accum(dcb_ref, jnp.sum(dxc, axis=0, keepdims=True), first)
        _accum(dcw_ref, jnp.concatenate([jnp.sum(dxc * us[3 - k], axis=0, keepdims=True) for k in range(CONV_K)], axis=0), first)
        nxt = jnp.where(first, 0.0, x_car[...])
        du = cw_ref[3:4, :] * dxc
        for s in range(1, CONV_K):
            du = du + cw_ref[3 - s:4 - s, :] * _shift_up(dxc, nxt, s)
        x_car[...] = dxc[:8]
        du_ref[...] = du.astype(BF16)

    act = jax.ShapeDtypeStruct((S, D), BF16)
    vec = jax.ShapeDtypeStruct((1, D), F32)
    wsh = jax.ShapeDtypeStruct(wr.shape, F32)
    rtile = pl.BlockSpec((tt, cb), lambda c, t: (tmap(t), c))
    rhalo = pl.BlockSpec((8, cb), lambda c, t: (jnp.maximum(tmap(t) * (tt // 8) - 1, 0), c))
    return pl.pallas_call(
        body, grid=(D // cb, nT),
        in_specs=[tile(1), halo(1), tile(2), rtile, rhalo, rtile, par(CONV_K), par(1), wblk, par(1), wblk, par(1), par(1)],
        out_specs=[rtile, rtile, par(CONV_K), par(1), wblk, par(1), wblk, par(1), par(1)],
        out_shape=[act, act, jax.ShapeDtypeStruct((CONV_K, D), F32), vec, wsh, vec, wsh, vec, vec],
        scratch_shapes=[pltpu.VMEM((8, cb), F32)] * 3,
        compiler_params=_cp(("parallel", "arbitrary")), name="lru_bwd",
    )(zbig, zbig, zbig, h, h, d_al, cw, cbias.reshape(1, D), wr, br.reshape(1, D), wi, bi.reshape(1, D), lam.reshape(1, D))


def _adamw(w, g, m, v, name):
    R, C = w.shape
    tm = _pick(R, max(8, ((1 << 18) // C) // 8 * 8), 8)
    c1 = 1.0 - ADAM_B1 ** ADAM_STEP
    c2 = 1.0 - ADAM_B2 ** ADAM_STEP

    def body(w_ref, g_ref, m_ref, v_ref, d_ref, mo_ref, vo_ref):
        gv = g_ref[...]
        mn = ADAM_B1 * m_ref[...] + (1.0 - ADAM_B1) * gv
        vn = ADAM_B2 * v_ref[...] + (1.0 - ADAM_B2) * (gv * gv)
        d_ref[...] = -ADAM_LR * ((mn / c1) / (jnp.sqrt(vn / c2) + ADAM_EPS) + ADAM_WD * w_ref[...])
        mo_ref[...] = mn
        vo_ref[...] = vn

    o = jax.ShapeDtypeStruct((R, C), F32)
    return pl.pallas_call(
        body, grid=(R // tm,), in_specs=[_rows(C)(tm)] * 4, out_specs=[_rows(C)(tm)] * 3, out_shape=[o] * 3,
        compiler_params=_cp(("parallel",)), name=name,
    )(w, g, m, v)


def _sum_slabs(x, name, out_dtype=F32):
    n, R, C = x.shape
    tm = _pick(R, max(8, ((1 << 18) // C) // 8 * 8), 16)

    def body(x_ref, o_ref):
        s = x_ref[0].astype(F32)
        for k in range(1, n):
            s = s + x_ref[k].astype(F32)
        o_ref[...] = s.astype(out_dtype)

    return pl.pallas_call(
        body, grid=(R // tm,), in_specs=[pl.BlockSpec((n, tm, C), lambda r: (0, r, 0))], out_specs=_rows(C)(tm),
        out_shape=jax.ShapeDtypeStruct((R, C), out_dtype), compiler_params=_cp(("parallel",)), name=name,
    )(x)


def _sum_core_halves(g, recv, core):
    n, _, Rh, C = g.shape
    tm = _pick(Rh, max(16, ((1 << 18) // C) // 16 * 16), 16)

    def body(c_ref, g_ref, r_ref, o_ref):
        o_ref[0] = (g_ref[0, 0].astype(F32) + r_ref[0, 0].astype(F32)).astype(BF16)

    return pl.pallas_call(
        body,
        grid_spec=pltpu.PrefetchScalarGridSpec(
            num_scalar_prefetch=1, grid=(n, Rh // tm),
            in_specs=[pl.BlockSpec((1, 1, tm, C), lambda k, r, c_ref: (k, c_ref[0], r, 0)),
                      pl.BlockSpec((1, 1, tm, C), lambda k, r, c_ref: (k, 0, r, 0))],
            out_specs=pl.BlockSpec((1, tm, C), lambda k, r, c_ref: (k, r, 0)),
        ),
        out_shape=jax.ShapeDtypeStruct((n, Rh, C), BF16),
        compiler_params=_cp(("parallel", "parallel")), name="grad_sum_cores",
    )(core.reshape(1).astype(jnp.int32), g, recv)


ANY = pl.BlockSpec(memory_space=pl.ANY)


def _place():
    x, y, c = lax.axis_index("x"), lax.axis_index("y"), lax.axis_index("c")
    chips = [(1 - x, y), (x, 1 - y), (1 - x, 1 - y)]
    return x, y, c, chips


def _allgather_chips(shard, name):
    R, C = shard.shape
    Rh = R // 2

    def body(x_ref, out_ref, send_sems, recv_sems, local_sem):
        x, y, c, chips = _place()
        me = 2 * x + y
        sibling = (x, y, 1 - c)

        def half(k, hc):
            return out_ref.at[k, pl.ds(hc * Rh, Rh), :]

        mine = pltpu.make_async_copy(x_ref, out_ref.at[me], local_sem)
        mine.start()
        first = [pltpu.make_async_remote_copy(
            src_ref=x_ref.at[pl.ds(c * Rh, Rh), :], dst_ref=half(me, c), send_sem=send_sems.at[j], recv_sem=recv_sems.at[j],
            device_id=(*chip, c), device_id_type=MESH) for j, chip in enumerate(chips)]
        for cp in first:
            cp.start()

        def landed(j, chip, hc):
            k = 2 * chip[0] + chip[1]
            return pltpu.make_async_remote_copy(
                src_ref=half(k, hc), dst_ref=half(k, hc), send_sem=send_sems.at[j], recv_sem=recv_sems.at[j],
                device_id=sibling, device_id_type=MESH)

        passed = []
        for j, chip in enumerate(chips):
            landed(j, chip, c).wait_recv()
            cp = landed(3 + j, chip, c)
            cp.start()
            passed.append(cp)
        for j, chip in enumerate(chips):
            landed(3 + j, chip, 1 - c).wait_recv()
        for cp in first + passed:
            cp.wait_send()
        mine.wait()

    return pl.pallas_call(
        body, in_specs=[ANY], out_specs=ANY, out_shape=jax.ShapeDtypeStruct((N_CHIPS, R, C), shard.dtype),
        scratch_shapes=[pltpu.SemaphoreType.DMA((6,)), pltpu.SemaphoreType.DMA((6,)), pltpu.SemaphoreType.DMA],
        name=name,
    )(shard)


def _allgather_all(blockv, name):
    R, C = blockv.shape

    def body(x_ref, out_ref, send_sems, recv_sems, local_sem):
        x, y, c, chips = _place()
        sibling = (x, y, 1 - c)

        def slab(px, py, pc):
            return out_ref.at[4 * px + 2 * py + pc]

        def copy(k, block, to, src=None):
            return pltpu.make_async_remote_copy(
                src_ref=slab(*block) if src is None else src, dst_ref=slab(*block), send_sem=send_sems.at[k],
                recv_sem=recv_sems.at[k], device_id=to, device_id_type=MESH)

        mine = pltpu.make_async_copy(x_ref, slab(x, y, c), local_sem)
        mine.start()
        first = [copy(0, (x, y, c), sibling, src=x_ref)]
        first += [copy(1 + j, (x, y, c), (*chip, c), src=x_ref) for j, chip in enumerate(chips)]
        for cp in first:
            cp.start()
        passed = [copy(4 + j, (*chip, c), sibling) for j, chip in enumerate(chips)]
        for j, chip in enumerate(chips):
            copy(1 + j, (*chip, c), (x, y, c)).wait_recv()
            passed[j].start()
        copy(0, (x, y, 1 - c), (x, y, c)).wait_recv()
        for j, chip in enumerate(chips):
            copy(4 + j, (*chip, 1 - c), (x, y, c)).wait_recv()
        for cp in first + passed:
            cp.wait_send()
        mine.wait()

    return pl.pallas_call(
        body, in_specs=[ANY], out_specs=ANY, out_shape=jax.ShapeDtypeStruct((N_DEV, R, C), blockv.dtype),
        scratch_shapes=[pltpu.SemaphoreType.DMA((7,)), pltpu.SemaphoreType.DMA((7,)), pltpu.SemaphoreType.DMA],
        name=name,
    )(blockv)


def _swap_cores_half(g):
    n, _, Rh, C = g.shape

    def body(g_ref, out_ref, send_sem, recv_sem):
        x, y, c, _ = _place()
        cp = pltpu.make_async_remote_copy(
            src_ref=g_ref.at[:, pl.ds(1 - c, 1)], dst_ref=out_ref, send_sem=send_sem, recv_sem=recv_sem,
            device_id=(x, y, 1 - c), device_id_type=MESH)
        cp.start()
        cp.wait()

    return pl.pallas_call(
        body, in_specs=[ANY], out_specs=ANY, out_shape=jax.ShapeDtypeStruct((n, 1, Rh, C), g.dtype),
        scratch_shapes=[pltpu.SemaphoreType.DMA, pltpu.SemaphoreType.DMA], name="grad_swap_cores",
    )(g)


def _alltoall_chips(s):
    n, Rh, C = s.shape

    def body(s_ref, out_ref, send_sems, recv_sems, local_sem):
        x, y, c, chips = _place()
        me = 2 * x + y
        mine = pltpu.make_async_copy(s_ref.at[me], out_ref.at[me], local_sem)
        mine.start()
        sent = []
        for j, chip in enumerate(chips):
            k = 2 * chip[0] + chip[1]
            cp = pltpu.make_async_remote_copy(
                src_ref=s_ref.at[k], dst_ref=out_ref.at[me], send_sem=send_sems.at[j], recv_sem=recv_sems.at[j],
                device_id=(*chip, c), device_id_type=MESH)
            cp.start()
            sent.append(cp)
        for j, chip in enumerate(chips):
            k = 2 * chip[0] + chip[1]
            pltpu.make_async_remote_copy(
                src_ref=s_ref.at[k], dst_ref=out_ref.at[k], send_sem=send_sems.at[j], recv_sem=recv_sems.at[j],
                device_id=(*chip, c), device_id_type=MESH).wait_recv()
        for cp in sent:
            cp.wait_send()
        mine.wait()

    return pl.pallas_call(
        body, in_specs=[ANY], out_specs=ANY, out_shape=jax.ShapeDtypeStruct((n, Rh, C), s.dtype),
        scratch_shapes=[pltpu.SemaphoreType.DMA((3,)), pltpu.SemaphoreType.DMA((3,)), pltpu.SemaphoreType.DMA],
        name="grad_alltoall_chips",
    )(s)


def _join_core_halves(half):
    Rh, C = half.shape

    def body(h_ref, out_ref, send_sem, recv_sem, local_sem):
        x, y, c, _ = _place()
        mine = pltpu.make_async_copy(h_ref, out_ref.at[c], local_sem)
        mine.start()
        cp = pltpu.make_async_remote_copy(
            src_ref=h_ref, dst_ref=out_ref.at[c], send_sem=send_sem, recv_sem=recv_sem,
            device_id=(x, y, 1 - c), device_id_type=MESH)
        cp.start()
        pltpu.make_async_remote_copy(
            src_ref=h_ref, dst_ref=out_ref.at[1 - c], send_sem=send_sem, recv_sem=recv_sem,
            device_id=(x, y, 1 - c), device_id_type=MESH).wait_recv()
        cp.wait_send()
        mine.wait()

    return pl.pallas_call(
        body, in_specs=[ANY], out_specs=ANY, out_shape=jax.ShapeDtypeStruct((2, Rh, C), half.dtype),
        scratch_shapes=[pltpu.SemaphoreType.DMA, pltpu.SemaphoreType.DMA, pltpu.SemaphoreType.DMA], name="grad_join_cores",
    )(half)


def _pack(arrays, cols, row_align):
    flat = jnp.concatenate([a.reshape(-1) for a in arrays])
    unit = cols * row_align
    total = -(-flat.size // unit) * unit
    return jnp.pad(flat, (0, total - flat.size)).reshape(total // cols, cols)


def _unpack(buf, shapes):
    flat = buf.reshape(-1)
    out, off = [], 0
    for shp in shapes:
        n = 1
        for d in shp:
            n *= d
        out.append(flat[off:off + n].reshape(shp))
        off += n
    return out


def _layer_fwd(x, p_l, w, tabs, dm):
    D, H, ql, kvl = dm["D"], dm["H"], dm["ql"], dm["kvl"]
    h = _rms_fwd(x, w["attn_norm"], "attn_norm_fwd")
    zbig = _mm(h, w["w_big"], "nn", "in_proj_big")
    zsm = _mm(h, w["w_sm"], "nn", "in_proj_small")
    qn, kvn, kr = _latent_fwd(zsm, w["q_a_norm"], w["kv_a_norm"], tabs, ql, kvl)
    q = _q_rope(_mm(qn, w["w_q"], "nn", "q_proj"), tabs, False, "q_rope_fwd")
    kv = _mm(kvn, w["w_kv"], "nn", "kv_proj", out_dtype=BF16)
    o, a_mla, lse = _attn_fwd(q, kv, kr, zbig, H)
    y_mla = _mm(a_mla, w["w_o_mla"], "nn", "o_mla_proj")
    h_lru, a_lru = _lru_fwd(zbig, w["conv_w"], w["conv_b"], w["w_rg"], w["b_rg"], w["w_ig"], w["b_ig"], w["lru_lambda"], D)
    y_lru = _mm(a_lru, w["w_o_lru"], "nn", "o_lru_proj")
    merged = _merge_fwd(zbig, y_mla, y_lru, D)
    x1 = _mm(merged, w["w_out"], "nn", "out_proj", add=x)
    hp = _rms_fwd(x1, w["ple_norm"], "ple_norm_fwd")
    pg = _mm(hp, w["w_ple_gate"], "nn", "ple_gate_proj")
    pe = _mm(p_l, w["w_ple"], "nn", "ple_proj")
    x2 = _ple_fwd(x1, pe, pg)
    res = dict(x=x, h=h, zbig=zbig, zsm=zsm, qn=qn, kvn=kvn, kr=kr, q=q, kv=kv, o=o, a_mla=a_mla, lse=lse, y_mla=y_mla,
               h_lru=h_lru, a_lru=a_lru, y_lru=y_lru, merged=merged, x1=x1, hp=hp, pg=pg, pe=pe, p=p_l)
    return x2, res


def _layer_bwd(dx2, r, w, tabs, dm):
    D, H, ql, kvl = dm["D"], dm["H"], dm["ql"], dm["kvl"]
    S = dx2.shape[0]
    g = {}
    d_pe, d_pg = _ple_bwd(dx2, r["pe"], r["pg"])
    g["w_ple"] = _mm(r["p"], d_pe, "tn", "ple_proj_dw", out_dtype=BF16)
    g["w_ple_gate"] = _mm(r["hp"], d_pg, "tn", "ple_gate_dw", out_dtype=BF16)
    d_hp = _mm(d_pg, w["w_ple_gate"], "nt", "ple_gate_dx")
    dx1, g["ple_norm"] = _rms_bwd(r["x1"], w["ple_norm"], d_hp, dx2, "ple_norm_bwd")
    g["w_out"] = _mm(r["merged"], dx1, "tn", "out_proj_dw", out_dtype=BF16)
    d_merged = _mm(dx1, w["w_out"], "nt", "out_proj_dx")
    d_ym, d_yl, d_mm, d_ml = _merge_bwd(r["zbig"], r["y_mla"], r["y_lru"], d_merged, D)
    g["w_o_mla"] = _mm(r["a_mla"], d_ym, "tn", "o_mla_dw", out_dtype=BF16)
    d_a_mla = _mm(d_ym, w["w_o_mla"], "nt", "o_mla_dx")
    g["w_o_lru"] = _mm(r["a_lru"], d_yl, "tn", "o_lru_dw", out_dtype=BF16)
    d_a_lru = _mm(d_yl, w["w_o_lru"], "nt", "o_lru_dx")
    d_o, d_gm, delta = _attn_gate_bwd(d_a_mla, r["o"], r["zbig"], H)
    dq = _attn_dq(r["q"], r["kv"], r["kr"], d_o, r["lse"], delta, H)
    dkv, dkr = _attn_dkv(r["q"], r["kv"], r["kr"], d_o, r["lse"][:, :, 0].reshape(H, 1, S), delta[:, :, 0].reshape(H, 1, S), H)
    dq_pre = _q_rope(dq, tabs, True, "q_rope_bwd")
    g["w_q"] = _mm(r["qn"], dq_pre, "tn", "q_proj_dw", out_dtype=BF16)
    d_qn = _mm(dq_pre, w["w_q"], "nt", "q_proj_dx")
    g["w_kv"] = _mm(r["kvn"], dkv, "tn", "kv_proj_dw", out_dtype=BF16)
    d_kvn = _mm(dkv, w["w_kv"], "nt", "kv_proj_dx")
    dzsm, g["q_a_norm"], g["kv_a_norm"] = _latent_bwd(r["zsm"], w["q_a_norm"], w["kv_a_norm"], tabs, d_qn, d_kvn, dkr, ql, kvl)
    (d_u, d_gl, g["conv_w"], g["conv_b"], g["w_rg"], g["b_rg"], g["w_ig"], g["b_ig"], g["lru_lambda"]) = _lru_bwd(
        r["zbig"], r["h_lru"], d_a_lru, w["conv_w"], w["conv_b"], w["w_rg"], w["b_rg"], w["w_ig"], w["b_ig"], w["lru_lambda"], D)
    dzbig = jnp.concatenate([d_gm, d_u, d_gl, d_mm, d_ml], axis=1)
    g["w_big"] = _mm(r["h"], dzbig, "tn", "in_proj_big_dw", out_dtype=BF16)
    g["w_sm"] = _mm(r["h"], dzsm, "tn", "in_proj_small_dw", out_dtype=BF16)
    dh = _mm(dzbig, w["w_big"], "nt", "in_proj_big_dx")
    dh = _mm(dzsm, w["w_sm"], "nt", "in_proj_small_dx", add=dh)
    dx, g["attn_norm"] = _rms_bwd(r["x"], w["attn_norm"], dh, dx1, "attn_norm_bwd")
    return dx, g


SHARDED = ("w_in", "w_q_b", "w_kv_b", "w_o_mla", "w_o_lru", "w_out", "w_ple_gate", "w_ple")
COL_SHARDED = ("w_in", "w_q_b", "w_kv_b", "w_ple")
REPLICATED = ("attn_norm", "q_a_norm", "kv_a_norm", "conv_b", "w_rg", "b_rg", "w_ig", "b_ig", "lru_lambda", "ple_norm", "final_norm")
WEIGHTS = ("attn_norm", "w_in", "q_a_norm", "w_q_b", "kv_a_norm", "w_kv_b", "conv_w", "conv_b", "w_rg", "b_rg", "w_ig", "b_ig",
           "lru_lambda", "w_o_mla", "w_o_lru", "w_out", "ple_norm", "w_ple_gate", "w_ple", "final_norm")


def _chip_concat(slabs, name):
    return jnp.concatenate(slabs, axis=-1 if name in COL_SHARDED else 1)


def _chip_slice(full, name, k):
    n = full.shape[-1 if name in COL_SHARDED else 1] // N_CHIPS
    return full[..., k * n:(k + 1) * n] if name in COL_SHARDED else full[:, k * n:(k + 1) * n, :]


def kernel(x, p, positions, attn_norm, w_in, q_a_norm, w_q_b, kv_a_norm, w_kv_b, conv_w, conv_b, w_rg, b_rg, w_ig, b_ig, lru_lambda, w_o_mla, w_o_lru, w_out, ple_norm, w_ple_gate, w_ple, final_norm, loss_target, m_attn_norm, m_w_in, m_q_a_norm, m_w_q_b, m_kv_a_norm, m_w_kv_b, m_conv_w, m_conv_b, m_w_rg, m_b_rg, m_w_ig, m_b_ig, m_lru_lambda, m_w_o_mla, m_w_o_lru, m_w_out, m_ple_norm, m_w_ple_gate, m_w_ple, m_final_norm, v_attn_norm, v_w_in, v_q_a_norm, v_w_q_b, v_kv_a_norm, v_w_kv_b, v_conv_w, v_conv_b, v_w_rg, v_b_rg, v_w_ig, v_b_ig, v_lru_lambda, v_w_o_mla, v_w_o_lru, v_w_out, v_ple_norm, v_w_ple_gate, v_w_ple, v_final_norm):
    W = dict(attn_norm=attn_norm, w_in=w_in, q_a_norm=q_a_norm, w_q_b=w_q_b, kv_a_norm=kv_a_norm, w_kv_b=w_kv_b, conv_w=conv_w,
             conv_b=conv_b, w_rg=w_rg, b_rg=b_rg, w_ig=w_ig, b_ig=b_ig, lru_lambda=lru_lambda, w_o_mla=w_o_mla, w_o_lru=w_o_lru,
             w_out=w_out, ple_norm=ple_norm, w_ple_gate=w_ple_gate, w_ple=w_ple, final_norm=final_norm)
    M = dict(attn_norm=m_attn_norm, w_in=m_w_in, q_a_norm=m_q_a_norm, w_q_b=m_w_q_b, kv_a_norm=m_kv_a_norm, w_kv_b=m_w_kv_b,
             conv_w=m_conv_w, conv_b=m_conv_b, w_rg=m_w_rg, b_rg=m_b_rg, w_ig=m_w_ig, b_ig=m_b_ig, lru_lambda=m_lru_lambda,
             w_o_mla=m_w_o_mla, w_o_lru=m_w_o_lru, w_out=m_w_out, ple_norm=m_ple_norm, w_ple_gate=m_w_ple_gate, w_ple=m_w_ple,
             final_norm=m_final_norm)
    V = dict(attn_norm=v_attn_norm, w_in=v_w_in, q_a_norm=v_q_a_norm, w_q_b=v_w_q_b, kv_a_norm=v_kv_a_norm, w_kv_b=v_w_kv_b,
             conv_w=v_conv_w, conv_b=v_conv_b, w_rg=v_w_rg, b_rg=v_b_rg, w_ig=v_w_ig, b_ig=v_b_ig, lru_lambda=v_lru_lambda,
             w_o_mla=v_w_o_mla, w_o_lru=v_w_o_lru, w_out=v_w_out, ple_norm=v_ple_norm, w_ple_gate=v_w_ple_gate, w_ple=v_w_ple,
             final_norm=v_final_norm)
    depth = attn_norm.shape[0]
    S, D = x.shape[1], x.shape[2]
    ql, kvl = q_a_norm.shape[1], kv_a_norm.shape[1]
    H = w_q_b.shape[2] * N_CHIPS // (QK_NOPE + QK_ROPE)
    dm = dict(D=D, H=H, ql=ql, kvl=kvl)
    chip = 2 * lax.axis_index("x") + lax.axis_index("y")
    core = lax.axis_index("c")

    shard_shapes = [W[n].shape for n in SHARDED]
    gathered = _allgather_chips(_pack([W[n].astype(BF16) for n in SHARDED], PACK_C, PACK_ROWS), "weights_allgather")
    slabs = [_unpack(gathered[k], shard_shapes) for k in range(N_CHIPS)]
    full = {n: _chip_concat([slabs[k][i] for k in range(N_CHIPS)], n) for i, n in enumerate(SHARDED)}
    cw_all = _allgather_chips(_pack([conv_w], LANES, 16), "conv_w_allgather")
    conv_w_full = jnp.concatenate([_unpack(cw_all[k], [conv_w.shape])[0] for k in range(N_CHIPS)], axis=-1)

    n_small = ql + kvl + QK_ROPE
    layers = []
    for l in range(depth):
        wq = full["w_q_b"][l].reshape(ql, H, QK_NOPE + QK_ROPE)
        wq = jnp.pad(wq, ((0, 0), (0, 0), (0, HEAD_PAD - QK_NOPE - QK_ROPE))).reshape(ql, H * HEAD_PAD)
        layers.append(dict(
            w_big=full["w_in"][l][:, n_small:], w_sm=jnp.pad(full["w_in"][l][:, :n_small], ((0, 0), (0, LANES - QK_ROPE))),
            w_q=wq, w_kv=full["w_kv_b"][l], w_o_mla=full["w_o_mla"][l], w_o_lru=full["w_o_lru"][l], w_out=full["w_out"][l],
            w_ple_gate=full["w_ple_gate"][l], w_ple=full["w_ple"][l], conv_w=conv_w_full[l],
            **{n: W[n][l] for n in REPLICATED if n != "final_norm"}))

    inv_freq = ROPE_THETA ** (-jnp.arange(0, QK_ROPE, 2, dtype=F32) / QK_ROPE)
    tabs = _rope_tables(positions[0], inv_freq)

    xs = x[0]
    saved = []
    for l in range(depth):
        xs, res = _layer_fwd(xs, p[l, 0], layers[l], tabs, dm)
        saved.append(res)
    dx, g_final_norm, loss_part = _loss_head(xs, final_norm, loss_target[0])
    grads = [None] * depth
    for l in reversed(range(depth)):
        dx, grads[l] = _layer_bwd(dx, saved[l], layers[l], tabs, dm)

    def stack(name):
        return jnp.stack([grads[l][name] for l in range(depth)])

    gfull = dict(
        w_in=jnp.concatenate([stack("w_sm")[:, :, :n_small], stack("w_big")], axis=-1),
        w_q_b=stack("w_q").reshape(depth, ql, H, HEAD_PAD)[..., :QK_NOPE + QK_ROPE].reshape(depth, ql, H * (QK_NOPE + QK_ROPE)),
        w_kv_b=stack("w_kv"), w_o_mla=stack("w_o_mla"), w_o_lru=stack("w_o_lru"), w_out=stack("w_out"),
        w_ple_gate=stack("w_ple_gate"), w_ple=stack("w_ple"))
    gpack = jnp.stack([_pack([_chip_slice(gfull[n], n, k) for n in SHARDED], PACK_C, PACK_ROWS) for k in range(N_CHIPS)])
    R = gpack.shape[1]
    gpack = gpack.reshape(N_CHIPS, 2, R // 2, PACK_C)
    core_sum = _sum_core_halves(gpack, _swap_cores_half(gpack), core)
    chip_sum = _sum_slabs(_alltoall_chips(core_sum), "grad_sum_chips")
    gshard = _unpack(_join_core_halves(chip_sum).reshape(R, PACK_C), shard_shapes)
    G = dict(zip(SHARDED, gshard))

    rep_shapes = [W[n].shape for n in REPLICATED] + [(depth, CONV_K, D), (LANES,)]
    rep = [stack(n).reshape(W[n].shape) for n in REPLICATED if n != "final_norm"]
    rep += [g_final_norm.reshape(D), stack("conv_w"), loss_part.reshape(LANES)]
    rep_sum = _unpack(_sum_slabs(_allgather_all(_pack(rep, LANES, SMALL_ROWS), "small_grads_allgather"), "small_grads_sum"), rep_shapes)
    for n, gv in zip(REPLICATED, rep_sum):
        G[n] = gv
    cshard = D // N_CHIPS
    G["conv_w"] = lax.dynamic_slice_in_dim(rep_sum[-2], chip * cshard, cshard, axis=2)
    loss = rep_sum[-1][0]

    small = REPLICATED + ("conv_w",)
    small_shapes = [W[n].shape for n in small]
    pk = lambda src: _pack([src[n] for n in small], LANES, SMALL_ROWS)
    upd = _adamw(pk(W), pk(G), pk(M), pk(V), "adamw_small")
    delta, new_m, new_v = ({n: a for n, a in zip(small, _unpack(u, small_shapes))} for u in upd)
    for n in SHARDED:
        shp = W[n].shape
        two_d = lambda a: a.reshape(-1, shp[-1])
        d_, m_, v_ = _adamw(two_d(W[n]), two_d(G[n]), two_d(M[n]), two_d(V[n]), "adamw_" + n)
        delta[n], new_m[n], new_v[n] = d_.reshape(shp), m_.reshape(shp), v_.reshape(shp)

    return (loss, dx.reshape(x.shape), *[G[n] for n in WEIGHTS], *[delta[n] for n in WEIGHTS],
            *[new_m[n] for n in WEIGHTS], *[new_v[n] for n in WEIGHTS])
```

```python
import jax
import jax.numpy as jnp
from jax import lax
from jax.experimental import pallas as pl
from jax.experimental.pallas import tpu as pltpu

F32 = jnp.float32
BF16 = jnp.bfloat16
MESH = pl.DeviceIdType.MESH

CHUNK = 64
QK_NOPE = 128
QK_ROPE = 64
V_HEAD = 128
ROPE_THETA = 10000.0
CONV_K = 4
LRU_C = 8.0
LRU_BLOCK_DIM = 128
EPS = 1e-6
ADAM_LR = 0.001
ADAM_B1 = 0.9
ADAM_B2 = 0.999
ADAM_EPS = 1e-08
ADAM_WD = 0.01
ADAM_STEP = 10

N_CHIPS = 4
N_DEV = 8
LANES = 128
HEAD_PAD = 256
VMEM_LIMIT = 48 * 1024 * 1024

MM_TILE_BYTES = 8 * 1024 * 1024
ATT_T = 512
ATT_SUB = 512
ROW_T = 256
LRU_TT = 512
LRU_CB = 512
PACK_C = 512
PACK_ROWS = 1024
SMALL_ROWS = 512

NT_DIMS = (((1,), (1,)), ((), ()))


def _cp(sem):
    return pltpu.CompilerParams(dimension_semantics=sem, vmem_limit_bytes=VMEM_LIMIT)


def _pick(n, pref, align=LANES):
    if n <= pref:
        return n
    t = pref - pref % align
    while t >= align:
        if n % t == 0:
            return t
        t -= align
    return n


def _sig(x):
    return 1.0 / (1.0 + jnp.exp(-x))


def _mm(a, b, mode, name, out_dtype=F32, add=None, tm=512, tn=1024, tk=2048):
    if mode == "nn":
        (M, K), (_, N) = a.shape, b.shape
    elif mode == "nt":
        (M, K), (N, _) = a.shape, b.shape
    else:
        (K, M), (_, N) = a.shape, b.shape
    tm, tn = _pick(M, tm), _pick(N, tn)
    while tk > 512 and tk * (tm * a.dtype.itemsize + tn * b.dtype.itemsize) > MM_TILE_BYTES:
        tk //= 2
    tk = _pick(K, tk)
    nm, nn, nk = M // tm, N // tn, K // tk
    i_outer = nm * b.size * b.dtype.itemsize <= nn * a.size * a.dtype.itemsize

    def ij(g0, g1):
        return (g0, g1) if i_outer else (g1, g0)

    def amap(g0, g1, k):
        i, _ = ij(g0, g1)
        return (k, i) if mode == "tn" else (i, k)

    def bmap(g0, g1, k):
        _, j = ij(g0, g1)
        return (j, k) if mode == "nt" else (k, j)

    def omap(g0, g1, k):
        return ij(g0, g1)

    ablk = (tk, tm) if mode == "tn" else (tm, tk)
    bblk = (tn, tk) if mode == "nt" else (tk, tn)

    def body(*refs):
        if add is None:
            a_ref, b_ref, o_ref = refs[:3]
            add_ref = None
        else:
            a_ref, b_ref, add_ref, o_ref = refs[:4]
        x = a_ref[...].astype(BF16)
        y = b_ref[...].astype(BF16)
        if mode == "nn":
            p = jnp.dot(x, y, preferred_element_type=F32)
        elif mode == "nt":
            p = lax.dot_general(x, y, NT_DIMS, preferred_element_type=F32)
        else:
            p = jnp.dot(x.T, y, preferred_element_type=F32)
        if nk == 1:
            if add_ref is not None:
                p = p + add_ref[...]
            o_ref[...] = p.astype(out_dtype)
        else:
            acc = refs[-1]
            k = pl.program_id(2)

            @pl.when(k == 0)
            def _():
                acc[...] = p if add_ref is None else p + add_ref[...]

            @pl.when(k > 0)
            def _():
                acc[...] += p

            @pl.when(k == nk - 1)
            def _():
                o_ref[...] = acc[...].astype(out_dtype)

    in_specs = [pl.BlockSpec(ablk, amap), pl.BlockSpec(bblk, bmap)]
    args = [a, b]
    if add is not None:
        in_specs.append(pl.BlockSpec((tm, tn), omap))
        args.append(add)
    grid = (nm, nn, nk) if i_outer else (nn, nm, nk)
    return pl.pallas_call(
        body, grid=grid, in_specs=in_specs, out_specs=pl.BlockSpec((tm, tn), omap),
        out_shape=jax.ShapeDtypeStruct((M, N), out_dtype),
        scratch_shapes=[pltpu.VMEM((tm, tn), F32)] if nk > 1 else [],
        compiler_params=_cp(("parallel", "parallel", "arbitrary")), name=name,
    )(*args)


def _rows(cols, i=0):
    def make(tm):
        return pl.BlockSpec((tm, cols), lambda r: (r, i))
    return make


def _par(cols):
    return pl.BlockSpec((1, cols), lambda r: (0, 0))


def _rms_fwd(x, g, name):
    S, D = x.shape
    tm = _pick(S, ROW_T, 8)

    def body(x_ref, g_ref, o_ref):
        xv = x_ref[...]
        rs = lax.rsqrt(jnp.mean(xv * xv, axis=-1, keepdims=True) + EPS)
        o_ref[...] = (xv * rs * g_ref[...]).astype(BF16)

    return pl.pallas_call(
        body, grid=(S // tm,), in_specs=[_rows(D)(tm), _par(D)], out_specs=_rows(D)(tm),
        out_shape=jax.ShapeDtypeStruct((S, D), BF16), compiler_params=_cp(("parallel",)), name=name,
    )(x, g.reshape(1, D))


def _rms_bwd_math(xv, g, dy):
    rs = lax.rsqrt(jnp.mean(xv * xv, axis=-1, keepdims=True) + EPS)
    xh = xv * rs
    dg = jnp.sum(dy * xh, axis=0, keepdims=True)
    dyg = dy * g
    dx = rs * (dyg - xh * jnp.mean(dyg * xh, axis=-1, keepdims=True))
    return dx, dg


def _accum(ref, val, first):
    @pl.when(first)
    def _():
        ref[...] = val

    @pl.when(jnp.logical_not(first))
    def _():
        ref[...] += val


def _rms_bwd(x, g, dy, dres, name):
    S, D = x.shape
    tm = _pick(S, ROW_T, 8)

    def body(x_ref, g_ref, dy_ref, dres_ref, dx_ref, dg_ref):
        dx, dg = _rms_bwd_math(x_ref[...], g_ref[...], dy_ref[...])
        dx_ref[...] = dres_ref[...] + dx
        _accum(dg_ref, dg, pl.program_id(0) == 0)

    return pl.pallas_call(
        body, grid=(S // tm,), in_specs=[_rows(D)(tm), _par(D), _rows(D)(tm), _rows(D)(tm)],
        out_specs=[_rows(D)(tm), _par(D)],
        out_shape=[jax.ShapeDtypeStruct((S, D), F32), jax.ShapeDtypeStruct((1, D), F32)],
        compiler_params=_cp(("arbitrary",)), name=name,
    )(x, g.reshape(1, D), dy, dres)


def _rope_tables(pos, inv_freq):
    S = pos.shape[0]
    tm = _pick(S, 512, 8)
    half = QK_ROPE // 2
    invf = jnp.concatenate([inv_freq, inv_freq, jnp.zeros((LANES - QK_ROPE,), F32)]).reshape(1, LANES)

    def body(pos_ref, f_ref, c_ref, sa_ref, sb_ref):
        ang = pos_ref[...].astype(F32) * f_ref[...]
        lane = lax.broadcasted_iota(jnp.int32, ang.shape, 1)
        c, s = jnp.cos(ang), jnp.sin(ang)
        c_ref[...] = jnp.where(lane < QK_ROPE, c, 0.0)
        sa_ref[...] = jnp.where(lane < half, -s, 0.0)
        sb_ref[...] = jnp.where((lane >= half) & (lane < QK_ROPE), s, 0.0)

    tab = jax.ShapeDtypeStruct((S, LANES), F32)
    return pl.pallas_call(
        body, grid=(S // tm,), in_specs=[pl.BlockSpec((tm, 1), lambda r: (r, 0)), _par(LANES)],
        out_specs=[_rows(LANES)(tm)] * 3, out_shape=[tab] * 3, compiler_params=_cp(("parallel",)), name="rope_tables",
    )(pos.reshape(S, 1), invf)


def _rope(x, c, sa, sb):
    return x * c + pltpu.roll(x, LANES - QK_ROPE // 2, 1) * sa + pltpu.roll(x, QK_ROPE // 2, 1) * sb


def _rope_t(d, c, sa, sb):
    return d * c + pltpu.roll(d * sa, QK_ROPE // 2, 1) + pltpu.roll(d * sb, LANES - QK_ROPE // 2, 1)


def _latent_fwd(zsm, gq, gkv, tabs, ql, kvl):
    S = zsm.shape[0]
    tm = _pick(S, ROW_T, 8)
    kr_blk = (ql + kvl) // LANES

    def body(q_ref, kv_ref, kr_ref, gq_ref, gkv_ref, c_ref, sa_ref, sb_ref, qn_ref, kvn_ref, kro_ref):
        for src, g_ref, dst in ((q_ref, gq_ref, qn_ref), (kv_ref, gkv_ref, kvn_ref)):
            v = src[...]
            rs = lax.rsqrt(jnp.mean(v * v, axis=-1, keepdims=True) + EPS)
            dst[...] = (v * rs * g_ref[...]).astype(BF16)
        kro_ref[...] = _rope(kr_ref[...], c_ref[...], sa_ref[...], sb_ref[...]).astype(BF16)

    return pl.pallas_call(
        body, grid=(S // tm,),
        in_specs=[_rows(ql, 0)(tm), _rows(kvl, 1)(tm), _rows(LANES, kr_blk)(tm), _par(ql), _par(kvl)] + [_rows(LANES)(tm)] * 3,
        out_specs=[_rows(ql)(tm), _rows(kvl)(tm), _rows(LANES)(tm)],
        out_shape=[jax.ShapeDtypeStruct((S, ql), BF16), jax.ShapeDtypeStruct((S, kvl), BF16), jax.ShapeDtypeStruct((S, LANES), BF16)],
        compiler_params=_cp(("parallel",)), name="latent_fwd",
    )(zsm, zsm, zsm, gq.reshape(1, ql), gkv.reshape(1, kvl), *tabs)


def _latent_bwd(zsm, gq, gkv, tabs, d_qn, d_kvn, dkr, ql, kvl):
    S, W = zsm.shape
    H = dkr.shape[0]
    tm = _pick(S, ROW_T, 8)
    kr_blk = (ql + kvl) // LANES

    def body(q_ref, kv_ref, gq_ref, gkv_ref, c_ref, sa_ref, sb_ref, dqn_ref, dkvn_ref, dkr_ref, dz_ref, dgq_ref, dgkv_ref):
        first = pl.program_id(0) == 0
        dq, dgq = _rms_bwd_math(q_ref[...], gq_ref[...], dqn_ref[...])
        dkv, dgkv = _rms_bwd_math(kv_ref[...], gkv_ref[...], dkvn_ref[...])
        dk = dkr_ref[0]
        for h in range(1, H):
            dk = dk + dkr_ref[h]
        dz_ref[:, 0:ql] = dq.astype(BF16)
        dz_ref[:, ql:ql + kvl] = dkv.astype(BF16)
        dz_ref[:, ql + kvl:] = _rope_t(dk, c_ref[...], sa_ref[...], sb_ref[...]).astype(BF16)
        _accum(dgq_ref, dgq, first)
        _accum(dgkv_ref, dgkv, first)

    return pl.pallas_call(
        body, grid=(S // tm,),
        in_specs=[_rows(ql, 0)(tm), _rows(kvl, 1)(tm), _par(ql), _par(kvl)] + [_rows(LANES)(tm)] * 3
        + [_rows(ql)(tm), _rows(kvl)(tm), pl.BlockSpec((H, tm, LANES), lambda r: (0, r, 0))],
        out_specs=[_rows(W)(tm), _par(ql), _par(kvl)],
        out_shape=[jax.ShapeDtypeStruct((S, W), BF16), jax.ShapeDtypeStruct((1, ql), F32), jax.ShapeDtypeStruct((1, kvl), F32)],
        compiler_params=_cp(("arbitrary",)), name="latent_bwd",
    )(zsm, zsm, gq.reshape(1, ql), gkv.reshape(1, kvl), *tabs, d_qn, d_kvn, dkr)


def _q_rope(q, tabs, transpose, name, gain=1.0):
    S, W = q.shape
    H = W // HEAD_PAD
    tm = _pick(S, ROW_T, 8)
    fn = _rope_t if transpose else _rope

    def body(q_ref, c_ref, sa_ref, sb_ref, o_ref):
        c, sa, sb = c_ref[...], sa_ref[...], sb_ref[...]
        if gain != 1.0:
            c, sa, sb = c * gain, sa * gain, sb * gain
        for h in range(H):
            lo = h * HEAD_PAD
            nope = q_ref[:, lo:lo + QK_NOPE]
            o_ref[:, lo:lo + QK_NOPE] = (nope if gain == 1.0 else nope * gain).astype(BF16)
            o_ref[:, lo + QK_NOPE:lo + HEAD_PAD] = fn(q_ref[:, lo + QK_NOPE:lo + HEAD_PAD], c, sa, sb).astype(BF16)

    return pl.pallas_call(
        body, grid=(S // tm,), in_specs=[_rows(W)(tm)] + [_rows(LANES)(tm)] * 3, out_specs=_rows(W)(tm),
        out_shape=jax.ShapeDtypeStruct((S, W), BF16), compiler_params=_cp(("parallel",)), name=name,
    )(q, *tabs)


def _merge_fwd(zbig, y_mla, y_lru, D):
    S = y_mla.shape[0]
    tm = _pick(S, ROW_T, 8)

    def body(mm_ref, ml_ref, ym_ref, yl_ref, o_ref):
        o_ref[...] = (_sig(mm_ref[...]) * ym_ref[...] + _sig(ml_ref[...]) * yl_ref[...]).astype(BF16)

    return pl.pallas_call(
        body, grid=(S // tm,), in_specs=[_rows(D, 3)(tm), _rows(D, 4)(tm), _rows(D)(tm), _rows(D)(tm)], out_specs=_rows(D)(tm),
        out_shape=jax.ShapeDtypeStruct((S, D), BF16), compiler_params=_cp(("parallel",)), name="merge_fwd",
    )(zbig, zbig, y_mla, y_lru)


def _merge_bwd(zbig, y_mla, y_lru, d_merged, D):
    S = y_mla.shape[0]
    tm = _pick(S, ROW_T, 8)

    def body(mm_ref, ml_ref, ym_ref, yl_ref, d_ref, dym_ref, dyl_ref, dmm_ref, dml_ref):
        d = d_ref[...]
        sm, sl = _sig(mm_ref[...]), _sig(ml_ref[...])
        dym_ref[...] = (d * sm).astype(BF16)
        dyl_ref[...] = (d * sl).astype(BF16)
        dmm_ref[...] = (d * ym_ref[...] * sm * (1.0 - sm)).astype(BF16)
        dml_ref[...] = (d * yl_ref[...] * sl * (1.0 - sl)).astype(BF16)

    o = jax.ShapeDtypeStruct((S, D), BF16)
    return pl.pallas_call(
        body, grid=(S // tm,), in_specs=[_rows(D, 3)(tm), _rows(D, 4)(tm)] + [_rows(D)(tm)] * 3, out_specs=[_rows(D)(tm)] * 4,
        out_shape=[o] * 4, compiler_params=_cp(("parallel",)), name="merge_bwd",
    )(zbig, zbig, y_mla, y_lru, d_merged)


def _ple_fwd(x1, pe, pg):
    S, D = x1.shape
    tm = _pick(S, ROW_T, 8)

    def body(x_ref, pe_ref, pg_ref, o_ref):
        o_ref[...] = x_ref[...] + pe_ref[...] * _sig(pg_ref[...])

    return pl.pallas_call(
        body, grid=(S // tm,), in_specs=[_rows(D)(tm)] * 3, out_specs=_rows(D)(tm),
        out_shape=jax.ShapeDtypeStruct((S, D), F32), compiler_params=_cp(("parallel",)), name="ple_fwd",
    )(x1, pe, pg)


def _ple_bwd(dx2, pe, pg):
    S, D = dx2.shape
    tm = _pick(S, ROW_T, 8)

    def body(d_ref, pe_ref, pg_ref, dpe_ref, dpg_ref):
        d = d_ref[...]
        s = _sig(pg_ref[...])
        dpe_ref[...] = (d * s).astype(BF16)
        dpg_ref[...] = (d * pe_ref[...] * s * (1.0 - s)).astype(BF16)

    o = jax.ShapeDtypeStruct((S, D), BF16)
    return pl.pallas_call(
        body, grid=(S // tm,), in_specs=[_rows(D)(tm)] * 3, out_specs=[_rows(D)(tm)] * 2, out_shape=[o] * 2,
        compiler_params=_cp(("parallel",)), name="ple_bwd",
    )(dx2, pe, pg)


def _loss_head(x, g, target):
    S, D = x.shape
    tm = _pick(S, ROW_T, 8)

    def body(x_ref, g_ref, t_ref, dx_ref, dg_ref, loss_ref):
        first = pl.program_id(0) == 0
        xv, gv = x_ref[...], g_ref[...]
        rs = lax.rsqrt(jnp.mean(xv * xv, axis=-1, keepdims=True) + EPS)
        e = xv * rs * gv - t_ref[...]
        part = 0.5 * jnp.sum(jnp.mean(e * e, axis=-1, keepdims=True), axis=0, keepdims=True)
        dx, dg = _rms_bwd_math(xv, gv, e * (1.0 / D))
        dx_ref[...] = dx
        _accum(dg_ref, dg, first)
        _accum(loss_ref, jnp.broadcast_to(part, (1, LANES)), first)

    return pl.pallas_call(
        body, grid=(S // tm,), in_specs=[_rows(D)(tm), _par(D), _rows(D)(tm)], out_specs=[_rows(D)(tm), _par(D), _par(LANES)],
        out_shape=[jax.ShapeDtypeStruct((S, D), F32), jax.ShapeDtypeStruct((1, D), F32), jax.ShapeDtypeStruct((1, LANES), F32)],
        compiler_params=_cp(("arbitrary",)), name="loss_head",
    )(x, g.reshape(1, D), target)


def _attn_gate_bwd(d_a, o, zbig, H):
    S, W = o.shape
    tm = _pick(S, ROW_T, 8)

    def body(da_ref, o_ref, g_ref, do_ref, dg_ref, dl_ref):
        da, ov, g = da_ref[...], o_ref[...], g_ref[...]
        s = _sig(g)
        do = da * g * s
        do_ref[...] = do.astype(BF16)
        dg_ref[...] = (da * ov * s * (1.0 + g * (1.0 - s))).astype(BF16)
        prod = do * ov
        for h in range(H):
            r = jnp.sum(prod[:, h * V_HEAD:(h + 1) * V_HEAD], axis=-1, keepdims=True)
            dl_ref[h] = jnp.broadcast_to(r, (tm, LANES))

    return pl.pallas_call(
        body, grid=(S // tm,), in_specs=[_rows(W)(tm), _rows(W)(tm), _rows(W, 0)(tm)],
        out_specs=[_rows(W)(tm), _rows(W)(tm), pl.BlockSpec((H, tm, LANES), lambda r: (0, r, 0))],
        out_shape=[jax.ShapeDtypeStruct((S, W), BF16), jax.ShapeDtypeStruct((S, W), BF16), jax.ShapeDtypeStruct((H, S, LANES), F32)],
        compiler_params=_cp(("parallel",)), name="attn_gate_bwd",
    )(d_a, o, zbig)


def _chunk_mask(ts, t, lo, q_rows):
    r = (lax.broadcasted_iota(jnp.int32, (ts, t), 0) + lo) // CHUNK
    c = lax.broadcasted_iota(jnp.int32, (ts, t), 1) // CHUNK
    return (c <= r) if q_rows else (r <= c)


def _attn_tiles(S):
    t = _pick(S, ATT_T)
    ts = _pick(t, ATT_SUB, 8)
    return t, ts, [r * ts for r in range(t // ts)]


QK_SCALE = 1.0 / (QK_NOPE + QK_ROPE) ** 0.5
LOG2E = 1.4426950408889634


def _attn_fwd(q, kv, kr, zbig, H):
    S = q.shape[0]
    t, ts, subs = _attn_tiles(S)
    nq = S // t

    def body(q_ref, kn_ref, v_ref, kr_ref, g_ref, o_ref, a_ref, lse_ref, m_scr, l_scr, acc_scr, s0_scr, s1_scr, p0_scr, p1_scr):
        i = pl.program_id(1)
        m_scr[...] = jnp.full((t, LANES), -1e30, F32)
        l_scr[...] = jnp.zeros((t, LANES), F32)
        acc_scr[...] = jnp.zeros((t, V_HEAD), F32)

        s_buf, p_buf = (s0_scr, s1_scr), (p0_scr, p1_scr)

        def logits(j, slot):
            ks = pl.multiple_of(j * t, t)
            k = jnp.concatenate([kn_ref[pl.ds(ks, t), :], kr_ref[pl.ds(ks, t), :]], axis=1)
            s_buf[slot][...] = lax.dot_general(q_ref[...], k, NT_DIMS, preferred_element_type=F32)

        def values(j, slot):
            ks = pl.multiple_of(jnp.maximum(j, 0) * t, t)
            acc_scr[...] += jnp.dot(p_buf[slot][...], v_ref[pl.ds(ks, t), :], preferred_element_type=F32)

        def softmax(slot, masked):
            s = s_buf[slot][...]
            if masked:
                s = jnp.where(_chunk_mask(t, t, 0, True), s, -1e30)
            m_prev = m_scr[...]
            m_next = jnp.maximum(m_prev, jnp.max(s, axis=1, keepdims=True))
            p = jnp.exp2(s - jnp.tile(m_next, (1, t // LANES)))
            alpha = jnp.exp2(m_prev - m_next)
            l_scr[...] = alpha * l_scr[...] + jnp.sum(p, axis=1, keepdims=True)
            m_scr[...] = m_next
            acc_scr[...] = acc_scr[...] * alpha
            p_buf[slot][...] = p.astype(BF16)

        def step(j, slot, masked=False, more=True):
            if more:
                logits(j + 1, 1 - slot)
            values(j - 1, 1 - slot)
            softmax(slot, masked)

        logits(0, 0)
        p1_scr[...] = jnp.zeros((t, t), BF16)

        def loop(a, carry):
            step(2 * a, 0)
            step(2 * a + 1, 1)
            return carry

        lax.fori_loop(0, i // 2, loop, 0)

        @pl.when(i % 2 == 0)
        def _():
            step(i, 0, masked=True, more=False)
            values(i, 0)

        @pl.when(i % 2 == 1)
        def _():
            step(i - 1, 0)
            step(i, 1, masked=True, more=False)
            values(i, 1)

        l = l_scr[...]
        ov = acc_scr[...] / l
        g = g_ref[...]
        o_ref[...] = ov
        a_ref[...] = (ov * g * _sig(g)).astype(BF16)
        lse_ref[0] = m_scr[...] + jnp.log(l) * LOG2E

    head_col = lambda w, off: pl.BlockSpec((S, w), lambda h, i: (0, 2 * h + off))
    return pl.pallas_call(
        body, grid=(H, nq),
        in_specs=[pl.BlockSpec((t, HEAD_PAD), lambda h, i: (i, h)), head_col(QK_NOPE, 0), head_col(V_HEAD, 1),
                  pl.BlockSpec((S, LANES), lambda h, i: (0, 0)), pl.BlockSpec((t, V_HEAD), lambda h, i: (i, h))],
        out_specs=[pl.BlockSpec((t, V_HEAD), lambda h, i: (i, h)), pl.BlockSpec((t, V_HEAD), lambda h, i: (i, h)),
                   pl.BlockSpec((1, t, LANES), lambda h, i: (h, i, 0))],
        out_shape=[jax.ShapeDtypeStruct((S, H * V_HEAD), F32), jax.ShapeDtypeStruct((S, H * V_HEAD), BF16),
                   jax.ShapeDtypeStruct((H, S, LANES), F32)],
        scratch_shapes=[pltpu.VMEM((t, LANES), F32), pltpu.VMEM((t, LANES), F32), pltpu.VMEM((t, V_HEAD), F32),
                        pltpu.VMEM((t, t), F32), pltpu.VMEM((t, t), F32), pltpu.VMEM((t, t), BF16), pltpu.VMEM((t, t), BF16)],
        compiler_params=_cp(("parallel", "arbitrary")), name="attn_fwd",
    )(q, kv, kv, kr, zbig)


TN_DIMS = (((0,), (0,)), ((), ()))


def _attn_bwd(q, kv, kr, do, lse_row, delta_row, H):
    S = q.shape[0]
    t, _, _ = _attn_tiles(S)
    nk = S // t

    def body(kn_ref, v_ref, kr_ref, q_ref, do_ref, lse_ref, dl_ref, dq_ref, dkv_ref, dkr_ref, dk_scr, dv_scr):
        j = pl.program_id(1)
        dk_scr[...] = jnp.zeros((t, HEAD_PAD), F32)
        dv_scr[...] = jnp.zeros((t, V_HEAD), F32)

        @pl.when(j == 0)
        def _():
            dq_ref[...] = jnp.zeros((S, HEAD_PAD), F32)

        k = jnp.concatenate([kn_ref[...], kr_ref[...]], axis=1)

        def block(i, masked):
            qs = pl.multiple_of(i * t, t)
            qv = q_ref[pl.ds(qs, t), :]
            dov = do_ref[pl.ds(qs, t), :]
            st = lax.dot_general(k, qv, NT_DIMS, preferred_element_type=F32)
            pt = jnp.exp2(st - lse_ref[0, :, pl.ds(qs, t)])
            if masked:
                pt = jnp.where(_chunk_mask(t, t, 0, False), pt, 0.0)
            dv_scr[...] += jnp.dot(pt.astype(BF16), dov, preferred_element_type=F32)
            dpt = lax.dot_general(v_ref[...], dov, NT_DIMS, preferred_element_type=F32)
            dst = (pt * (dpt - dl_ref[0, :, pl.ds(qs, t)])).astype(BF16)
            dk_scr[...] += jnp.dot(dst, qv, preferred_element_type=F32)
            dq_ref[pl.ds(qs, t), :] += lax.dot_general(dst, k, TN_DIMS, preferred_element_type=F32)

        block(j, True)

        def loop(i, carry):
            block(i, False)
            return carry

        lax.fori_loop(j + 1, nk, loop, 0)
        dk = dk_scr[...] * (1.0 / LOG2E)
        dkv_ref[:, 0:QK_NOPE] = dk[:, 0:QK_NOPE].astype(BF16)
        dkv_ref[:, QK_NOPE:] = dv_scr[...].astype(BF16)
        dkr_ref[0] = dk[:, QK_NOPE:]

    tile_col = lambda w, off: pl.BlockSpec((t, w), lambda h, j: (j, 2 * h + off))
    row = pl.BlockSpec((1, 1, S), lambda h, j: (h, 0, 0))
    return pl.pallas_call(
        body, grid=(H, nk),
        in_specs=[tile_col(QK_NOPE, 0), tile_col(V_HEAD, 1), pl.BlockSpec((t, LANES), lambda h, j: (j, 0)),
                  pl.BlockSpec((S, HEAD_PAD), lambda h, j: (0, h)), pl.BlockSpec((S, V_HEAD), lambda h, j: (0, h)), row, row],
        out_specs=[pl.BlockSpec((S, HEAD_PAD), lambda h, j: (0, h)), pl.BlockSpec((t, HEAD_PAD), lambda h, j: (j, h)),
                   pl.BlockSpec((1, t, LANES), lambda h, j: (h, j, 0))],
        out_shape=[jax.ShapeDtypeStruct((S, H * HEAD_PAD), F32), jax.ShapeDtypeStruct((S, H * HEAD_PAD), BF16),
                   jax.ShapeDtypeStruct((H, S, LANES), F32)],
        scratch_shapes=[pltpu.VMEM((t, HEAD_PAD), F32), pltpu.VMEM((t, V_HEAD), F32)],
        compiler_params=_cp(("arbitrary", "arbitrary")), name="attn_bwd",
    )(kv, kv, kr, q, do, lse_row, delta_row)


def _shift_down(x, prev8, s):
    rx = pltpu.roll(x, s, 0)
    rp = pltpu.roll(prev8, s, 0)
    rows = lax.broadcasted_iota(jnp.int32, rp.shape, 0)
    return jnp.concatenate([jnp.where(rows < s, rp, rx[:8]), rx[8:]], axis=0)


def _shift_up(x, next8, s):
    n = x.shape[0]
    rx = pltpu.roll(x, n - s, 0)
    rn = pltpu.roll(next8, 8 - s, 0)
    rows = lax.broadcasted_iota(jnp.int32, rn.shape, 0)
    return jnp.concatenate([rx[:n - 8], jnp.where(rows >= 8 - s, rn, rx[n - 8:])], axis=0)


def _scan_rows(a, b, up):
    n = a.shape[0]
    rows = lax.broadcasted_iota(jnp.int32, a.shape, 0)
    d = 1
    while d < n:
        keep = (rows < n - d) if up else (rows >= d)
        sh = n - d if up else d
        a_s = jnp.where(keep, pltpu.roll(a, sh, 0), 1.0)
        b_s = jnp.where(keep, pltpu.roll(b, sh, 0), 0.0)
        b = a * b_s + b
        a = a * a_s
        d *= 2
    return a, b


def _log1p(e):
    u = 1.0 + e
    return jnp.where(u == 1.0, e, jnp.log(u) * (e / (u - 1.0)))


def _lru_pre(u, prev8, cw_ref, cb_ref, wr_ref, br_ref, wi_ref, bi_ref, lam_ref):
    us = [u, _shift_down(u, prev8, 1), _shift_down(u, prev8, 2), _shift_down(u, prev8, 3)]
    xc = cb_ref[...] + cw_ref[3:4, :] * us[0] + cw_ref[2:3, :] * us[1] + cw_ref[1:2, :] * us[2] + cw_ref[0:1, :] * us[3]
    x16 = xc.astype(BF16)
    nb = xc.shape[1] // LRU_BLOCK_DIM
    blk = lambda k: slice(k * LRU_BLOCK_DIM, (k + 1) * LRU_BLOCK_DIM)
    pr = jnp.concatenate([jnp.dot(x16[:, blk(k)], wr_ref[k].astype(BF16), preferred_element_type=F32) for k in range(nb)], axis=1)
    pi = jnp.concatenate([jnp.dot(x16[:, blk(k)], wi_ref[k].astype(BF16), preferred_element_type=F32) for k in range(nb)], axis=1)
    r = _sig(pr + br_ref[...])
    i = _sig(pi + bi_ref[...])
    nlam = -lam_ref[...]
    sp = jnp.maximum(nlam, 0.0) + _log1p(jnp.exp(-jnp.abs(nlam)))
    log_a = (-LRU_C * r) * sp
    a = jnp.exp(log_a)
    mult = jnp.sqrt(-jnp.tanh(log_a) * (a * a + 1.0))
    return us, xc, x16, r, i, sp, a, mult


def _lru_specs(D, cb, tt, nT, rev):
    nb = cb // LRU_BLOCK_DIM
    tmap = (lambda t: nT - 1 - t) if rev else (lambda t: t)
    ncb = D // cb

    def tile(piece):
        return pl.BlockSpec((tt, cb), lambda c, t: (tmap(t), piece * ncb + c))

    def halo(piece):
        return pl.BlockSpec((8, cb), lambda c, t: (jnp.maximum(tmap(t) * (tt // 8) - 1, 0), piece * ncb + c))

    par = lambda rows: pl.BlockSpec((rows, cb), lambda c, t: (0, c))
    wblk = pl.BlockSpec((nb, LRU_BLOCK_DIM, LRU_BLOCK_DIM), lambda c, t: (c, 0, 0))
    return tile, halo, par, wblk, tmap


def _lru_fwd(zbig, cw, cbias, wr, br, wi, bi, lam, D):
    S = zbig.shape[0]
    tt, cb = _pick(S, LRU_TT, 8), _pick(D, LRU_CB)
    nT = S // tt
    tile, halo, par, wblk, _ = _lru_specs(D, cb, tt, nT, False)

    def body(u_ref, up_ref, g_ref, cw_ref, cb_ref, wr_ref, br_ref, wi_ref, bi_ref, lam_ref, h_ref, al_ref, carry):
        t = pl.program_id(1)
        prev8 = jnp.where(t > 0, up_ref[...], 0.0)
        _, xc, _, _, i, _, a, mult = _lru_pre(u_ref[...], prev8, cw_ref, cb_ref, wr_ref, br_ref, wi_ref, bi_ref, lam_ref)
        pa, hb = _scan_rows(a, mult * (i * xc), False)
        h0 = jnp.where(t > 0, carry[7:8, :], 0.0)
        h = hb + pa * h0
        h_ref[...] = h
        carry[...] = h[tt - 8:, :]
        g = g_ref[...]
        al_ref[...] = (h * g * _sig(g)).astype(BF16)

    return pl.pallas_call(
        body, grid=(D // cb, nT),
        in_specs=[tile(1), halo(1), tile(2), par(CONV_K), par(1), wblk, par(1), wblk, par(1), par(1)],
        out_specs=[tile(0), tile(0)],
        out_shape=[jax.ShapeDtypeStruct((S, D), F32), jax.ShapeDtypeStruct((S, D), BF16)],
        scratch_shapes=[pltpu.VMEM((8, cb), F32)],
        compiler_params=_cp(("parallel", "arbitrary")), name="lru_fwd",
    )(zbig, zbig, zbig, cw, cbias.reshape(1, D), wr, br.reshape(1, D), wi, bi.reshape(1, D), lam.reshape(1, D))


def _lru_bwd(zbig, h, d_al, cw, cbias, wr, br, wi, bi, lam, D):
    S = zbig.shape[0]
    tt, cb = _pick(S, LRU_TT, 8), _pick(D, LRU_CB)
    nT = S // tt
    nb = cb // LRU_BLOCK_DIM
    tile, halo, par, wblk, tmap = _lru_specs(D, cb, tt, nT, True)

    def body(u_ref, up_ref, g_ref, h_ref, hp_ref, dal_ref, cw_ref, cb_ref, wr_ref, br_ref, wi_ref, bi_ref, lam_ref,
             du_ref, dg_ref, dcw_ref, dcb_ref, dwr_ref, dbr_ref, dwi_ref, dbi_ref, dlam_ref, g_car, a_car, x_car):
        step = pl.program_id(1)
        first = step == 0
        t = nT - 1 - step
        prev8 = jnp.where(t > 0, up_ref[...], 0.0)
        us, xc, x16, r, i, sp, a, mult = _lru_pre(u_ref[...], prev8, cw_ref, cb_ref, wr_ref, br_ref, wi_ref, bi_ref, lam_ref)
        hv = h_ref[...]
        h_m1 = _shift_down(hv, jnp.where(t > 0, hp_ref[...], 0.0), 1)
        g, dal = g_ref[...], dal_ref[...]
        sg = _sig(g)
        dg_ref[...] = (dal * hv * sg * (1.0 + g * (1.0 - sg))).astype(BF16)
        dh = dal * g * sg
        coef = _shift_up(a, jnp.where(first, 0.0, a_car[...]), 1)
        pa, gb = _scan_rows(coef, dh, True)
        G = gb + pa * jnp.where(first, 0.0, g_car[0:1, :])
        g_car[...] = G[:8]
        a_car[...] = a[:8]
        da = G * h_m1
        ixc = i * xc
        dixc = G * mult
        dlog = da * a - (G * ixc) * (a * a) / mult
        dpr = dlog * (-LRU_C * sp) * r * (1.0 - r)
        dpi = dixc * xc * i * (1.0 - i)
        dxc = dixc * i
        dsp = jnp.sum(dlog * (-LRU_C) * r, axis=0, keepdims=True)
        dpr16, dpi16 = dpr.astype(BF16), dpi.astype(BF16)
        blk = lambda k: slice(k * LRU_BLOCK_DIM, (k + 1) * LRU_BLOCK_DIM)
        back = []
        for k in range(nb):
            xk = x16[:, blk(k)].T
            dwr_k = jnp.dot(xk, dpr16[:, blk(k)], preferred_element_type=F32)
            dwi_k = jnp.dot(xk, dpi16[:, blk(k)], preferred_element_type=F32)

            @pl.when(first)
            def _():
                dwr_ref[k] = dwr_k
                dwi_ref[k] = dwi_k

            @pl.when(jnp.logical_not(first))
            def _():
                dwr_ref[k] += dwr_k
                dwi_ref[k] += dwi_k

            back.append(lax.dot_general(dpr16[:, blk(k)], wr_ref[k].astype(BF16), NT_DIMS, preferred_element_type=F32)
                        + lax.dot_general(dpi16[:, blk(k)], wi_ref[k].astype(BF16), NT_DIMS, preferred_element_type=F32))
        dxc = dxc + jnp.concatenate(back, axis=1)
        _accum(dbr_ref, jnp.sum(dpr, axis=0, keepdims=True), first)
        _accum(dbi_ref, jnp.sum(dpi, axis=0, keepdims=True), first)
        _accum(dlam_ref, dsp * (-_sig(-lam_ref[...])), first)
        _accum(dcb_ref, jnp.sum(dxc, axis=0, keepdims=True), first)
        _accum(dcw_ref, jnp.concatenate([jnp.sum(dxc * us[3 - k], axis=0, keepdims=True) for k in range(CONV_K)], axis=0), first)
        nxt = jnp.where(first, 0.0, x_car[...])
        du = cw_ref[3:4, :] * dxc
        for s in range(1, CONV_K):
            du = du + cw_ref[3 - s:4 - s, :] * _shift_up(dxc, nxt, s)
        x_car[...] = dxc[:8]
        du_ref[...] = du.astype(BF16)

    act = jax.ShapeDtypeStruct((S, D), BF16)
    vec = jax.ShapeDtypeStruct((1, D), F32)
    wsh = jax.ShapeDtypeStruct(wr.shape, F32)
    rtile = pl.BlockSpec((tt, cb), lambda c, t: (tmap(t), c))
    rhalo = pl.BlockSpec((8, cb), lambda c, t: (jnp.maximum(tmap(t) * (tt // 8) - 1, 0), c))
    return pl.pallas_call(
        body, grid=(D // cb, nT),
        in_specs=[tile(1), halo(1), tile(2), rtile, rhalo, rtile, par(CONV_K), par(1), wblk, par(1), wblk, par(1), par(1)],
        out_specs=[rtile, rtile, par(CONV_K), par(1), wblk, par(1), wblk, par(1), par(1)],
        out_shape=[act, act, jax.ShapeDtypeStruct((CONV_K, D), F32), vec, wsh, vec, wsh, vec, vec],
        scratch_shapes=[pltpu.VMEM((8, cb), F32)] * 3,
        compiler_params=_cp(("parallel", "arbitrary")), name="lru_bwd",
    )(zbig, zbig, zbig, h, h, d_al, cw, cbias.reshape(1, D), wr, br.reshape(1, D), wi, bi.reshape(1, D), lam.reshape(1, D))


def _adamw(w, g, m, v, name):
    R, C = w.shape
    tm = _pick(R, max(8, ((1 << 18) // C) // 8 * 8), 8)
    c1 = 1.0 - ADAM_B1 ** ADAM_STEP
    c2 = 1.0 - ADAM_B2 ** ADAM_STEP

    def body(w_ref, g_ref, m_ref, v_ref, d_ref, mo_ref, vo_ref):
        gv = g_ref[...]
        mn = ADAM_B1 * m_ref[...] + (1.0 - ADAM_B1) * gv
        vn = ADAM_B2 * v_ref[...] + (1.0 - ADAM_B2) * (gv * gv)
        d_ref[...] = -ADAM_LR * ((mn / c1) / (jnp.sqrt(vn / c2) + ADAM_EPS) + ADAM_WD * w_ref[...])
        mo_ref[...] = mn
        vo_ref[...] = vn

    o = jax.ShapeDtypeStruct((R, C), F32)
    return pl.pallas_call(
        body, grid=(R // tm,), in_specs=[_rows(C)(tm)] * 4, out_specs=[_rows(C)(tm)] * 3, out_shape=[o] * 3,
        compiler_params=_cp(("parallel",)), name=name,
    )(w, g, m, v)


def _sum_slabs(x, name, out_dtype=F32):
    n, R, C = x.shape
    tm = _pick(R, max(8, ((1 << 18) // C) // 8 * 8), 16)

    def body(x_ref, o_ref):
        s = x_ref[0].astype(F32)
        for k in range(1, n):
            s = s + x_ref[k].astype(F32)
        o_ref[...] = s.astype(out_dtype)

    return pl.pallas_call(
        body, grid=(R // tm,), in_specs=[pl.BlockSpec((n, tm, C), lambda r: (0, r, 0))], out_specs=_rows(C)(tm),
        out_shape=jax.ShapeDtypeStruct((R, C), out_dtype), compiler_params=_cp(("parallel",)), name=name,
    )(x)


def _sum_core_halves(g, recv, core):
    n, _, Rh, C = g.shape
    tm = _pick(Rh, max(16, ((1 << 18) // C) // 16 * 16), 16)

    def body(c_ref, g_ref, r_ref, o_ref):
        o_ref[0] = (g_ref[0, 0].astype(F32) + r_ref[0, 0].astype(F32)).astype(BF16)

    return pl.pallas_call(
        body,
        grid_spec=pltpu.PrefetchScalarGridSpec(
            num_scalar_prefetch=1, grid=(n, Rh // tm),
            in_specs=[pl.BlockSpec((1, 1, tm, C), lambda k, r, c_ref: (k, c_ref[0], r, 0)),
                      pl.BlockSpec((1, 1, tm, C), lambda k, r, c_ref: (k, 0, r, 0))],
            out_specs=pl.BlockSpec((1, tm, C), lambda k, r, c_ref: (k, r, 0)),
        ),
        out_shape=jax.ShapeDtypeStruct((n, Rh, C), BF16),
        compiler_params=_cp(("parallel", "parallel")), name="grad_sum_cores",
    )(core.reshape(1).astype(jnp.int32), g, recv)


ANY = pl.BlockSpec(memory_space=pl.ANY)


def _place():
    x, y, c = lax.axis_index("x"), lax.axis_index("y"), lax.axis_index("c")
    chips = [(1 - x, y), (x, 1 - y), (1 - x, 1 - y)]
    return x, y, c, chips


def _allgather_chips(shard, name):
    R, C = shard.shape
    Rh = R // 2

    def body(x_ref, out_ref, send_sems, recv_sems):
        x, y, c, chips = _place()
        me = 2 * x + y
        sibling = (x, y, 1 - c)

        def half(k, hc):
            return out_ref.at[k, pl.ds(hc * Rh, Rh), :]

        first = [pltpu.make_async_remote_copy(
            src_ref=x_ref.at[pl.ds(c * Rh, Rh), :], dst_ref=half(me, c), send_sem=send_sems.at[j], recv_sem=recv_sems.at[j],
            device_id=(*chip, c), device_id_type=MESH) for j, chip in enumerate(chips)]
        for cp in first:
            cp.start()

        def landed(j, chip, hc):
            k = 2 * chip[0] + chip[1]
            return pltpu.make_async_remote_copy(
                src_ref=half(k, hc), dst_ref=half(k, hc), send_sem=send_sems.at[j], recv_sem=recv_sems.at[j],
                device_id=sibling, device_id_type=MESH)

        passed = []
        for j, chip in enumerate(chips):
            landed(j, chip, c).wait_recv()
            cp = landed(3 + j, chip, c)
            cp.start()
            passed.append(cp)
        for j, chip in enumerate(chips):
            landed(3 + j, chip, 1 - c).wait_recv()
        for cp in first + passed:
            cp.wait_send()

    out = pl.pallas_call(
        body, in_specs=[ANY], out_specs=ANY, out_shape=jax.ShapeDtypeStruct((N_CHIPS, R, C), shard.dtype),
        scratch_shapes=[pltpu.SemaphoreType.DMA((6,)), pltpu.SemaphoreType.DMA((6,))],
        name=name,
    )(shard)
    return lax.dynamic_update_slice(out, shard[None], (2 * lax.axis_index("x") + lax.axis_index("y"), 0, 0))


def _allgather_all(blockv, name):
    R, C = blockv.shape

    def body(x_ref, out_ref, send_sems, recv_sems):
        x, y, c, chips = _place()
        sibling = (x, y, 1 - c)

        def slab(px, py, pc):
            return out_ref.at[4 * px + 2 * py + pc]

        def copy(k, block, to, src=None):
            return pltpu.make_async_remote_copy(
                src_ref=slab(*block) if src is None else src, dst_ref=slab(*block), send_sem=send_sems.at[k],
                recv_sem=recv_sems.at[k], device_id=to, device_id_type=MESH)

        first = [copy(0, (x, y, c), sibling, src=x_ref)]
        first += [copy(1 + j, (x, y, c), (*chip, c), src=x_ref) for j, chip in enumerate(chips)]
        for cp in first:
            cp.start()
        passed = [copy(4 + j, (*chip, c), sibling) for j, chip in enumerate(chips)]
        for j, chip in enumerate(chips):
            copy(1 + j, (*chip, c), (x, y, c)).wait_recv()
            passed[j].start()
        copy(0, (x, y, 1 - c), (x, y, c)).wait_recv()
        for j, chip in enumerate(chips):
            copy(4 + j, (*chip, 1 - c), (x, y, c)).wait_recv()
        for cp in first + passed:
            cp.wait_send()

    out = pl.pallas_call(
        body, in_specs=[ANY], out_specs=ANY, out_shape=jax.ShapeDtypeStruct((N_DEV, R, C), blockv.dtype),
        scratch_shapes=[pltpu.SemaphoreType.DMA((7,)), pltpu.SemaphoreType.DMA((7,))],
        name=name,
    )(blockv)
    me = 4 * lax.axis_index("x") + 2 * lax.axis_index("y") + lax.axis_index("c")
    return lax.dynamic_update_slice(out, blockv[None], (me, 0, 0))


def _swap_cores_half(g):
    n, _, Rh, C = g.shape

    def body(g_ref, out_ref, send_sem, recv_sem):
        x, y, c, _ = _place()
        cp = pltpu.make_async_remote_copy(
            src_ref=g_ref.at[:, pl.ds(1 - c, 1)], dst_ref=out_ref, send_sem=send_sem, recv_sem=recv_sem,
            device_id=(x, y, 1 - c), device_id_type=MESH)
        cp.start()
        cp.wait()

    return pl.pallas_call(
        body, in_specs=[ANY], out_specs=ANY, out_shape=jax.ShapeDtypeStruct((n, 1, Rh, C), g.dtype),
        scratch_shapes=[pltpu.SemaphoreType.DMA, pltpu.SemaphoreType.DMA], name="grad_swap_cores",
    )(g)


def _alltoall_chips(s):
    n, Rh, C = s.shape

    def body(s_ref, out_ref, send_sems, recv_sems):
        x, y, c, chips = _place()
        me = 2 * x + y
        sent = []
        for j, chip in enumerate(chips):
            k = 2 * chip[0] + chip[1]
            cp = pltpu.make_async_remote_copy(
                src_ref=s_ref.at[k], dst_ref=out_ref.at[me], send_sem=send_sems.at[j], recv_sem=recv_sems.at[j],
                device_id=(*chip, c), device_id_type=MESH)
            cp.start()
            sent.append(cp)
        for j, chip in enumerate(chips):
            k = 2 * chip[0] + chip[1]
            pltpu.make_async_remote_copy(
                src_ref=s_ref.at[k], dst_ref=out_ref.at[k], send_sem=send_sems.at[j], recv_sem=recv_sems.at[j],
                device_id=(*chip, c), device_id_type=MESH).wait_recv()
        for cp in sent:
            cp.wait_send()

    out = pl.pallas_call(
        body, in_specs=[ANY], out_specs=ANY, out_shape=jax.ShapeDtypeStruct((n, Rh, C), s.dtype),
        scratch_shapes=[pltpu.SemaphoreType.DMA((3,)), pltpu.SemaphoreType.DMA((3,))],
        name="grad_alltoall_chips",
    )(s)
    me = 2 * lax.axis_index("x") + lax.axis_index("y")
    return lax.dynamic_update_slice(out, lax.dynamic_slice_in_dim(s, me, 1, axis=0), (me, 0, 0))


def _join_core_halves(half):
    Rh, C = half.shape

    def body(h_ref, out_ref, send_sem, recv_sem):
        x, y, c, _ = _place()
        cp = pltpu.make_async_remote_copy(
            src_ref=h_ref, dst_ref=out_ref.at[c], send_sem=send_sem, recv_sem=recv_sem,
            device_id=(x, y, 1 - c), device_id_type=MESH)
        cp.start()
        pltpu.make_async_remote_copy(
            src_ref=h_ref, dst_ref=out_ref.at[1 - c], send_sem=send_sem, recv_sem=recv_sem,
            device_id=(x, y, 1 - c), device_id_type=MESH).wait_recv()
        cp.wait_send()

    out = pl.pallas_call(
        body, in_specs=[ANY], out_specs=ANY, out_shape=jax.ShapeDtypeStruct((2, Rh, C), half.dtype),
        scratch_shapes=[pltpu.SemaphoreType.DMA, pltpu.SemaphoreType.DMA], name="grad_join_cores",
    )(half)
    return lax.dynamic_update_slice(out, half[None], (lax.axis_index("c"), 0, 0))


def _pack(arrays, cols, row_align):
    flat = jnp.concatenate([a.reshape(-1) for a in arrays])
    unit = cols * row_align
    total = -(-flat.size // unit) * unit
    return jnp.pad(flat, (0, total - flat.size)).reshape(total // cols, cols)


def _unpack(buf, shapes):
    flat = buf.reshape(-1)
    out, off = [], 0
    for shp in shapes:
        n = 1
        for d in shp:
            n *= d
        out.append(flat[off:off + n].reshape(shp))
        off += n
    return out


def _layer_fwd(x, p_l, w, tabs, dm):
    D, H, ql, kvl = dm["D"], dm["H"], dm["ql"], dm["kvl"]
    h = _rms_fwd(x, w["attn_norm"], "attn_norm_fwd")
    zbig = _mm(h, w["w_big"], "nn", "in_proj_big")
    zsm = _mm(h, w["w_sm"], "nn", "in_proj_small")
    qn, kvn, kr = _latent_fwd(zsm, w["q_a_norm"], w["kv_a_norm"], tabs, ql, kvl)
    q = _q_rope(_mm(qn, w["w_q"], "nn", "q_proj"), tabs, False, "q_rope_fwd", gain=QK_SCALE * LOG2E)
    kv = _mm(kvn, w["w_kv"], "nn", "kv_proj", out_dtype=BF16)
    o, a_mla, lse = _attn_fwd(q, kv, kr, zbig, H)
    y_mla = _mm(a_mla, w["w_o_mla"], "nn", "o_mla_proj")
    h_lru, a_lru = _lru_fwd(zbig, w["conv_w"], w["conv_b"], w["w_rg"], w["b_rg"], w["w_ig"], w["b_ig"], w["lru_lambda"], D)
    y_lru = _mm(a_lru, w["w_o_lru"], "nn", "o_lru_proj")
    merged = _merge_fwd(zbig, y_mla, y_lru, D)
    x1 = _mm(merged, w["w_out"], "nn", "out_proj", add=x)
    hp = _rms_fwd(x1, w["ple_norm"], "ple_norm_fwd")
    pg = _mm(hp, w["w_ple_gate"], "nn", "ple_gate_proj")
    pe = _mm(p_l, w["w_ple"], "nn", "ple_proj")
    x2 = _ple_fwd(x1, pe, pg)
    res = dict(x=x, h=h, zbig=zbig, zsm=zsm, qn=qn, kvn=kvn, kr=kr, q=q, kv=kv, o=o, a_mla=a_mla, lse=lse, y_mla=y_mla,
               h_lru=h_lru, a_lru=a_lru, y_lru=y_lru, merged=merged, x1=x1, hp=hp, pg=pg, pe=pe, p=p_l)
    return x2, res


def _layer_bwd(dx2, r, w, tabs, dm):
    D, H, ql, kvl = dm["D"], dm["H"], dm["ql"], dm["kvl"]
    S = dx2.shape[0]
    g = {}
    d_pe, d_pg = _ple_bwd(dx2, r["pe"], r["pg"])
    g["w_ple"] = _mm(r["p"], d_pe, "tn", "ple_proj_dw", out_dtype=BF16)
    g["w_ple_gate"] = _mm(r["hp"], d_pg, "tn", "ple_gate_dw", out_dtype=BF16)
    d_hp = _mm(d_pg, w["w_ple_gate"], "nt", "ple_gate_dx")
    dx1, g["ple_norm"] = _rms_bwd(r["x1"], w["ple_norm"], d_hp, dx2, "ple_norm_bwd")
    g["w_out"] = _mm(r["merged"], dx1, "tn", "out_proj_dw", out_dtype=BF16)
    d_merged = _mm(dx1, w["w_out"], "nt", "out_proj_dx")
    d_ym, d_yl, d_mm, d_ml = _merge_bwd(r["zbig"], r["y_mla"], r["y_lru"], d_merged, D)
    g["w_o_mla"] = _mm(r["a_mla"], d_ym, "tn", "o_mla_dw", out_dtype=BF16)
    d_a_mla = _mm(d_ym, w["w_o_mla"], "nt", "o_mla_dx")
    g["w_o_lru"] = _mm(r["a_lru"], d_yl, "tn", "o_lru_dw", out_dtype=BF16)
    d_a_lru = _mm(d_yl, w["w_o_lru"], "nt", "o_lru_dx")
    d_o, d_gm, delta = _attn_gate_bwd(d_a_mla, r["o"], r["zbig"], H)
    dq, dkv, dkr = _attn_bwd(r["q"], r["kv"], r["kr"], d_o, r["lse"][:, :, 0].reshape(H, 1, S), delta[:, :, 0].reshape(H, 1, S), H)
    dq_pre = _q_rope(dq, tabs, True, "q_rope_bwd", gain=QK_SCALE)
    g["w_q"] = _mm(r["qn"], dq_pre, "tn", "q_proj_dw", out_dtype=BF16)
    d_qn = _mm(dq_pre, w["w_q"], "nt", "q_proj_dx")
    g["w_kv"] = _mm(r["kvn"], dkv, "tn", "kv_proj_dw", out_dtype=BF16)
    d_kvn = _mm(dkv, w["w_kv"], "nt", "kv_proj_dx")
    dzsm, g["q_a_norm"], g["kv_a_norm"] = _latent_bwd(r["zsm"], w["q_a_norm"], w["kv_a_norm"], tabs, d_qn, d_kvn, dkr, ql, kvl)
    (d_u, d_gl, g["conv_w"], g["conv_b"], g["w_rg"], g["b_rg"], g["w_ig"], g["b_ig"], g["lru_lambda"]) = _lru_bwd(
        r["zbig"], r["h_lru"], d_a_lru, w["conv_w"], w["conv_b"], w["w_rg"], w["b_rg"], w["w_ig"], w["b_ig"], w["lru_lambda"], D)
    dzbig = jnp.concatenate([d_gm, d_u, d_gl, d_mm, d_ml], axis=1)
    g["w_big"] = _mm(r["h"], dzbig, "tn", "in_proj_big_dw", out_dtype=BF16)
    g["w_sm"] = _mm(r["h"], dzsm, "tn", "in_proj_small_dw", out_dtype=BF16)
    dh = _mm(dzbig, w["w_big"], "nt", "in_proj_big_dx")
    dh = _mm(dzsm, w["w_sm"], "nt", "in_proj_small_dx", add=dh)
    dx, g["attn_norm"] = _rms_bwd(r["x"], w["attn_norm"], dh, dx1, "attn_norm_bwd")
    return dx, g


SHARDED = ("w_in", "w_q_b", "w_kv_b", "w_o_mla", "w_o_lru", "w_out", "w_ple_gate", "w_ple")
COL_SHARDED = ("w_in", "w_q_b", "w_kv_b", "w_ple")
REPLICATED = ("attn_norm", "q_a_norm", "kv_a_norm", "conv_b", "w_rg", "b_rg", "w_ig", "b_ig", "lru_lambda", "ple_norm", "final_norm")
WEIGHTS = ("attn_norm", "w_in", "q_a_norm", "w_q_b", "kv_a_norm", "w_kv_b", "conv_w", "conv_b", "w_rg", "b_rg", "w_ig", "b_ig",
           "lru_lambda", "w_o_mla", "w_o_lru", "w_out", "ple_norm", "w_ple_gate", "w_ple", "final_norm")


def _chip_concat(slabs, name):
    return jnp.concatenate(slabs, axis=-1 if name in COL_SHARDED else 1)


def _chip_slice(full, name, k):
    n = full.shape[-1 if name in COL_SHARDED else 1] // N_CHIPS
    return full[..., k * n:(k + 1) * n] if name in COL_SHARDED else full[:, k * n:(k + 1) * n, :]


def kernel(x, p, positions, attn_norm, w_in, q_a_norm, w_q_b, kv_a_norm, w_kv_b, conv_w, conv_b, w_rg, b_rg, w_ig, b_ig, lru_lambda, w_o_mla, w_o_lru, w_out, ple_norm, w_ple_gate, w_ple, final_norm, loss_target, m_attn_norm, m_w_in, m_q_a_norm, m_w_q_b, m_kv_a_norm, m_w_kv_b, m_conv_w, m_conv_b, m_w_rg, m_b_rg, m_w_ig, m_b_ig, m_lru_lambda, m_w_o_mla, m_w_o_lru, m_w_out, m_ple_norm, m_w_ple_gate, m_w_ple, m_final_norm, v_attn_norm, v_w_in, v_q_a_norm, v_w_q_b, v_kv_a_norm, v_w_kv_b, v_conv_w, v_conv_b, v_w_rg, v_b_rg, v_w_ig, v_b_ig, v_lru_lambda, v_w_o_mla, v_w_o_lru, v_w_out, v_ple_norm, v_w_ple_gate, v_w_ple, v_final_norm):
    W = dict(attn_norm=attn_norm, w_in=w_in, q_a_norm=q_a_norm, w_q_b=w_q_b, kv_a_norm=kv_a_norm, w_kv_b=w_kv_b, conv_w=conv_w,
             conv_b=conv_b, w_rg=w_rg, b_rg=b_rg, w_ig=w_ig, b_ig=b_ig, lru_lambda=lru_lambda, w_o_mla=w_o_mla, w_o_lru=w_o_lru,
             w_out=w_out, ple_norm=ple_norm, w_ple_gate=w_ple_gate, w_ple=w_ple, final_norm=final_norm)
    M = dict(attn_norm=m_attn_norm, w_in=m_w_in, q_a_norm=m_q_a_norm, w_q_b=m_w_q_b, kv_a_norm=m_kv_a_norm, w_kv_b=m_w_kv_b,
             conv_w=m_conv_w, conv_b=m_conv_b, w_rg=m_w_rg, b_rg=m_b_rg, w_ig=m_w_ig, b_ig=m_b_ig, lru_lambda=m_lru_lambda,
             w_o_mla=m_w_o_mla, w_o_lru=m_w_o_lru, w_out=m_w_out, ple_norm=m_ple_norm, w_ple_gate=m_w_ple_gate, w_ple=m_w_ple,
             final_norm=m_final_norm)
    V = dict(attn_norm=v_attn_norm, w_in=v_w_in, q_a_norm=v_q_a_norm, w_q_b=v_w_q_b, kv_a_norm=v_kv_a_norm, w_kv_b=v_w_kv_b,
             conv_w=v_conv_w, conv_b=v_conv_b, w_rg=v_w_rg, b_rg=v_b_rg, w_ig=v_w_ig, b_ig=v_b_ig, lru_lambda=v_lru_lambda,
             w_o_mla=v_w_o_mla, w_o_lru=v_w_o_lru, w_out=v_w_out, ple_norm=v_ple_norm, w_ple_gate=v_w_ple_gate, w_ple=v_w_ple,
             final_norm=v_final_norm)
    depth = attn_norm.shape[0]
    S, D = x.shape[1], x.shape[2]
    ql, kvl = q_a_norm.shape[1], kv_a_norm.shape[1]
    H = w_q_b.shape[2] * N_CHIPS // (QK_NOPE + QK_ROPE)
    dm = dict(D=D, H=H, ql=ql, kvl=kvl)
    chip = 2 * lax.axis_index("x") + lax.axis_index("y")
    core = lax.axis_index("c")

    shard_shapes = [W[n].shape for n in SHARDED]
    gathered = _allgather_chips(_pack([W[n].astype(BF16) for n in SHARDED], PACK_C, PACK_ROWS), "weights_allgather")
    slabs = [_unpack(gathered[k], shard_shapes) for k in range(N_CHIPS)]
    full = {n: _chip_concat([slabs[k][i] for k in range(N_CHIPS)], n) for i, n in enumerate(SHARDED)}
    cw_all = _allgather_chips(_pack([conv_w], LANES, 16), "conv_w_allgather")
    conv_w_full = jnp.concatenate([_unpack(cw_all[k], [conv_w.shape])[0] for k in range(N_CHIPS)], axis=-1)

    n_small = ql + kvl + QK_ROPE
    layers = []
    for l in range(depth):
        wq = full["w_q_b"][l].reshape(ql, H, QK_NOPE + QK_ROPE)
        wq = jnp.pad(wq, ((0, 0), (0, 0), (0, HEAD_PAD - QK_NOPE - QK_ROPE))).reshape(ql, H * HEAD_PAD)
        layers.append(dict(
            w_big=full["w_in"][l][:, n_small:], w_sm=jnp.pad(full["w_in"][l][:, :n_small], ((0, 0), (0, LANES - QK_ROPE))),
            w_q=wq, w_kv=full["w_kv_b"][l], w_o_mla=full["w_o_mla"][l], w_o_lru=full["w_o_lru"][l], w_out=full["w_out"][l],
            w_ple_gate=full["w_ple_gate"][l], w_ple=full["w_ple"][l], conv_w=conv_w_full[l],
            **{n: W[n][l] for n in REPLICATED if n != "final_norm"}))

    inv_freq = ROPE_THETA ** (-jnp.arange(0, QK_ROPE, 2, dtype=F32) / QK_ROPE)
    tabs = _rope_tables(positions[0], inv_freq)

    xs = x[0]
    saved = []
    for l in range(depth):
        xs, res = _layer_fwd(xs, p[l, 0], layers[l], tabs, dm)
        saved.append(res)
    dx, g_final_norm, loss_part = _loss_head(xs, final_norm, loss_target[0])
    grads = [None] * depth
    for l in reversed(range(depth)):
        dx, grads[l] = _layer_bwd(dx, saved[l], layers[l], tabs, dm)

    def stack(name):
        return jnp.stack([grads[l][name] for l in range(depth)])

    gfull = dict(
        w_in=jnp.concatenate([stack("w_sm")[:, :, :n_small], stack("w_big")], axis=-1),
        w_q_b=stack("w_q").reshape(depth, ql, H, HEAD_PAD)[..., :QK_NOPE + QK_ROPE].reshape(depth, ql, H * (QK_NOPE + QK_ROPE)),
        w_kv_b=stack("w_kv"), w_o_mla=stack("w_o_mla"), w_o_lru=stack("w_o_lru"), w_out=stack("w_out"),
        w_ple_gate=stack("w_ple_gate"), w_ple=stack("w_ple"))
    gpack = jnp.stack([_pack([_chip_slice(gfull[n], n, k) for n in SHARDED], PACK_C, PACK_ROWS) for k in range(N_CHIPS)])
    R = gpack.shape[1]
    gpack = gpack.reshape(N_CHIPS, 2, R // 2, PACK_C)
    core_sum = _sum_core_halves(gpack, _swap_cores_half(gpack), core)
    chip_sum = _sum_slabs(_alltoall_chips(core_sum), "grad_sum_chips")
    gshard = _unpack(_join_core_halves(chip_sum).reshape(R, PACK_C), shard_shapes)
    G = dict(zip(SHARDED, gshard))

    rep_shapes = [W[n].shape for n in REPLICATED] + [(depth, CONV_K, D), (LANES,)]
    rep = [stack(n).reshape(W[n].shape) for n in REPLICATED if n != "final_norm"]
    rep += [g_final_norm.reshape(D), stack("conv_w"), loss_part.reshape(LANES)]
    rep_sum = _unpack(_sum_slabs(_allgather_all(_pack(rep, LANES, SMALL_ROWS), "small_grads_allgather"), "small_grads_sum"), rep_shapes)
    for n, gv in zip(REPLICATED, rep_sum):
        G[n] = gv
    cshard = D // N_CHIPS
    G["conv_w"] = lax.dynamic_slice_in_dim(rep_sum[-2], chip * cshard, cshard, axis=2)
    loss = rep_sum[-1][0]

    small = REPLICATED + ("conv_w",)
    small_shapes = [W[n].shape for n in small]
    pk = lambda src: _pack([src[n] for n in small], LANES, SMALL_ROWS)
    upd = _adamw(pk(W), pk(G), pk(M), pk(V), "adamw_small")
    delta, new_m, new_v = ({n: a for n, a in zip(small, _unpack(u, small_shapes))} for u in upd)
    for n in SHARDED:
        shp = W[n].shape
        two_d = lambda a: a.reshape(-1, shp[-1])
        d_, m_, v_ = _adamw(two_d(W[n]), two_d(G[n]), two_d(M[n]), two_d(V[n]), "adamw_" + n)
        delta[n], new_m[n], new_v[n] = d_.reshape(shp), m_.reshape(shp), v_.reshape(shp)

    return (loss, dx.reshape(x.shape), *[G[n] for n in WEIGHTS], *[delta[n] for n in WEIGHTS],
            *[new_m[n] for n in WEIGHTS], *[new_v[n] for n in WEIGHTS])
```

```python
import jax
import jax.numpy as jnp
from jax import lax
from jax.experimental import pallas as pl
from jax.experimental.pallas import tpu as pltpu

F32 = jnp.float32
BF16 = jnp.bfloat16
MESH = pl.DeviceIdType.MESH

CHUNK = 64
QK_NOPE = 128
QK_ROPE = 64
V_HEAD = 128
ROPE_THETA = 10000.0
CONV_K = 4
LRU_C = 8.0
LRU_BLOCK_DIM = 128
EPS = 1e-6
ADAM_LR = 0.001
ADAM_B1 = 0.9
ADAM_B2 = 0.999
ADAM_EPS = 1e-08
ADAM_WD = 0.01
ADAM_STEP = 10

N_CHIPS = 4
N_DEV = 8
LANES = 128
HEAD_PAD = 256
VMEM_LIMIT = 48 * 1024 * 1024

MM_TILE_BYTES = 8 * 1024 * 1024
ATT_T = 512
ATT_SUB = 512
ROW_T = 256
LRU_TT = 512
LRU_CB = 512
PACK_C = 512
PACK_ROWS = 1024
SMALL_ROWS = 512

NT_DIMS = (((1,), (1,)), ((), ()))


def _cp(sem):
    return pltpu.CompilerParams(dimension_semantics=sem, vmem_limit_bytes=VMEM_LIMIT)


def _pick(n, pref, align=LANES):
    if n <= pref:
        return n
    t = pref - pref % align
    while t >= align:
        if n % t == 0:
            return t
        t -= align
    return n


def _sig(x):
    return 1.0 / (1.0 + jnp.exp(-x))


def _mm(a, b, mode, name, out_dtype=F32, add=None, tm=512, tn=1024, tk=2048):
    if mode == "nn":
        (M, K), (_, N) = a.shape, b.shape
    elif mode == "nt":
        (M, K), (N, _) = a.shape, b.shape
    else:
        (K, M), (_, N) = a.shape, b.shape
    tm, tn = _pick(M, tm), _pick(N, tn)
    while tk > 512 and tk * (tm * a.dtype.itemsize + tn * b.dtype.itemsize) > MM_TILE_BYTES:
        tk //= 2
    tk = _pick(K, tk)
    nm, nn, nk = M // tm, N // tn, K // tk
    i_outer = nm * b.size * b.dtype.itemsize <= nn * a.size * a.dtype.itemsize

    def ij(g0, g1):
        return (g0, g1) if i_outer else (g1, g0)

    def amap(g0, g1, k):
        i, _ = ij(g0, g1)
        return (k, i) if mode == "tn" else (i, k)

    def bmap(g0, g1, k):
        _, j = ij(g0, g1)
        return (j, k) if mode == "nt" else (k, j)

    def omap(g0, g1, k):
        return ij(g0, g1)

    ablk = (tk, tm) if mode == "tn" else (tm, tk)
    bblk = (tn, tk) if mode == "nt" else (tk, tn)

    def body(*refs):
        if add is None:
            a_ref, b_ref, o_ref = refs[:3]
            add_ref = None
        else:
            a_ref, b_ref, add_ref, o_ref = refs[:4]
        x = a_ref[...].astype(BF16)
        y = b_ref[...].astype(BF16)
        if mode == "nn":
            p = jnp.dot(x, y, preferred_element_type=F32)
        elif mode == "nt":
            p = lax.dot_general(x, y, NT_DIMS, preferred_element_type=F32)
        else:
            p = jnp.dot(x.T, y, preferred_element_type=F32)
        if nk == 1:
            if add_ref is not None:
                p = p + add_ref[...]
            o_ref[...] = p.astype(out_dtype)
        else:
            acc = refs[-1]
            k = pl.program_id(2)

            @pl.when(k == 0)
            def _():
                acc[...] = p if add_ref is None else p + add_ref[...]

            @pl.when(k > 0)
            def _():
                acc[...] += p

            @pl.when(k == nk - 1)
            def _():
                o_ref[...] = acc[...].astype(out_dtype)

    in_specs = [pl.BlockSpec(ablk, amap), pl.BlockSpec(bblk, bmap)]
    args = [a, b]
    if add is not None:
        in_specs.append(pl.BlockSpec((tm, tn), omap))
        args.append(add)
    grid = (nm, nn, nk) if i_outer else (nn, nm, nk)
    return pl.pallas_call(
        body, grid=grid, in_specs=in_specs, out_specs=pl.BlockSpec((tm, tn), omap),
        out_shape=jax.ShapeDtypeStruct((M, N), out_dtype),
        scratch_shapes=[pltpu.VMEM((tm, tn), F32)] if nk > 1 else [],
        compiler_params=_cp(("parallel", "parallel", "arbitrary")), name=name,
    )(*args)


def _rows(cols, i=0):
    def make(tm):
        return pl.BlockSpec((tm, cols), lambda r: (r, i))
    return make


def _par(cols):
    return pl.BlockSpec((1, cols), lambda r: (0, 0))


def _rms_fwd(x, g, name):
    S, D = x.shape
    tm = _pick(S, ROW_T, 8)

    def body(x_ref, g_ref, o_ref):
        xv = x_ref[...]
        rs = lax.rsqrt(jnp.mean(xv * xv, axis=-1, keepdims=True) + EPS)
        o_ref[...] = (xv * rs * g_ref[...]).astype(BF16)

    return pl.pallas_call(
        body, grid=(S // tm,), in_specs=[_rows(D)(tm), _par(D)], out_specs=_rows(D)(tm),
        out_shape=jax.ShapeDtypeStruct((S, D), BF16), compiler_params=_cp(("parallel",)), name=name,
    )(x, g.reshape(1, D))


def _rms_bwd_math(xv, g, dy):
    rs = lax.rsqrt(jnp.mean(xv * xv, axis=-1, keepdims=True) + EPS)
    xh = xv * rs
    dg = jnp.sum(dy * xh, axis=0, keepdims=True)
    dyg = dy * g
    dx = rs * (dyg - xh * jnp.mean(dyg * xh, axis=-1, keepdims=True))
    return dx, dg


def _accum(ref, val, first):
    @pl.when(first)
    def _():
        ref[...] = val

    @pl.when(jnp.logical_not(first))
    def _():
        ref[...] += val


def _rms_bwd(x, g, dy, dres, name):
    S, D = x.shape
    tm = _pick(S, ROW_T, 8)

    def body(x_ref, g_ref, dy_ref, dres_ref, dx_ref, dg_ref):
        dx, dg = _rms_bwd_math(x_ref[...], g_ref[...], dy_ref[...])
        dx_ref[...] = dres_ref[...] + dx
        _accum(dg_ref, dg, pl.program_id(0) == 0)

    return pl.pallas_call(
        body, grid=(S // tm,), in_specs=[_rows(D)(tm), _par(D), _rows(D)(tm), _rows(D)(tm)],
        out_specs=[_rows(D)(tm), _par(D)],
        out_shape=[jax.ShapeDtypeStruct((S, D), F32), jax.ShapeDtypeStruct((1, D), F32)],
        compiler_params=_cp(("arbitrary",)), name=name,
    )(x, g.reshape(1, D), dy, dres)


def _rope_tables(pos, inv_freq):
    S = pos.shape[0]
    tm = _pick(S, 512, 8)
    half = QK_ROPE // 2
    invf = jnp.concatenate([inv_freq, inv_freq, jnp.zeros((LANES - QK_ROPE,), F32)]).reshape(1, LANES)

    def body(pos_ref, f_ref, c_ref, sa_ref, sb_ref):
        ang = pos_ref[...].astype(F32) * f_ref[...]
        lane = lax.broadcasted_iota(jnp.int32, ang.shape, 1)
        c, s = jnp.cos(ang), jnp.sin(ang)
        c_ref[...] = jnp.where(lane < QK_ROPE, c, 0.0)
        sa_ref[...] = jnp.where(lane < half, -s, 0.0)
        sb_ref[...] = jnp.where((lane >= half) & (lane < QK_ROPE), s, 0.0)

    tab = jax.ShapeDtypeStruct((S, LANES), F32)
    return pl.pallas_call(
        body, grid=(S // tm,), in_specs=[pl.BlockSpec((tm, 1), lambda r: (r, 0)), _par(LANES)],
        out_specs=[_rows(LANES)(tm)] * 3, out_shape=[tab] * 3, compiler_params=_cp(("parallel",)), name="rope_tables",
    )(pos.reshape(S, 1), invf)


def _rope(x, c, sa, sb):
    return x * c + pltpu.roll(x, LANES - QK_ROPE // 2, 1) * sa + pltpu.roll(x, QK_ROPE // 2, 1) * sb


def _rope_t(d, c, sa, sb):
    return d * c + pltpu.roll(d * sa, QK_ROPE // 2, 1) + pltpu.roll(d * sb, LANES - QK_ROPE // 2, 1)


def _latent_fwd(zsm, gq, gkv, tabs, ql, kvl):
    S = zsm.shape[0]
    tm = _pick(S, ROW_T, 8)
    kr_blk = (ql + kvl) // LANES

    def body(q_ref, kv_ref, kr_ref, gq_ref, gkv_ref, c_ref, sa_ref, sb_ref, qn_ref, kvn_ref, kro_ref):
        for src, g_ref, dst in ((q_ref, gq_ref, qn_ref), (kv_ref, gkv_ref, kvn_ref)):
            v = src[...]
            rs = lax.rsqrt(jnp.mean(v * v, axis=-1, keepdims=True) + EPS)
            dst[...] = (v * rs * g_ref[...]).astype(BF16)
        kro_ref[...] = _rope(kr_ref[...], c_ref[...], sa_ref[...], sb_ref[...]).astype(BF16)

    return pl.pallas_call(
        body, grid=(S // tm,),
        in_specs=[_rows(ql, 0)(tm), _rows(kvl, 1)(tm), _rows(LANES, kr_blk)(tm), _par(ql), _par(kvl)] + [_rows(LANES)(tm)] * 3,
        out_specs=[_rows(ql)(tm), _rows(kvl)(tm), _rows(LANES)(tm)],
        out_shape=[jax.ShapeDtypeStruct((S, ql), BF16), jax.ShapeDtypeStruct((S, kvl), BF16), jax.ShapeDtypeStruct((S, LANES), BF16)],
        compiler_params=_cp(("parallel",)), name="latent_fwd",
    )(zsm, zsm, zsm, gq.reshape(1, ql), gkv.reshape(1, kvl), *tabs)


def _latent_bwd(zsm, gq, gkv, tabs, d_qn, d_kvn, dkr, ql, kvl):
    S, W = zsm.shape
    H = dkr.shape[0]
    tm = _pick(S, ROW_T, 8)
    kr_blk = (ql + kvl) // LANES

    def body(q_ref, kv_ref, gq_ref, gkv_ref, c_ref, sa_ref, sb_ref, dqn_ref, dkvn_ref, dkr_ref, dz_ref, dgq_ref, dgkv_ref):
        first = pl.program_id(0) == 0
        dq, dgq = _rms_bwd_math(q_ref[...], gq_ref[...], dqn_ref[...])
        dkv, dgkv = _rms_bwd_math(kv_ref[...], gkv_ref[...], dkvn_ref[...])
        dk = dkr_ref[0]
        for h in range(1, H):
            dk = dk + dkr_ref[h]
        dz_ref[:, 0:ql] = dq.astype(BF16)
        dz_ref[:, ql:ql + kvl] = dkv.astype(BF16)
        dz_ref[:, ql + kvl:] = _rope_t(dk, c_ref[...], sa_ref[...], sb_ref[...]).astype(BF16)
        _accum(dgq_ref, dgq, first)
        _accum(dgkv_ref, dgkv, first)

    return pl.pallas_call(
        body, grid=(S // tm,),
        in_specs=[_rows(ql, 0)(tm), _rows(kvl, 1)(tm), _par(ql), _par(kvl)] + [_rows(LANES)(tm)] * 3
        + [_rows(ql)(tm), _rows(kvl)(tm), pl.BlockSpec((H, tm, LANES), lambda r: (0, r, 0))],
        out_specs=[_rows(W)(tm), _par(ql), _par(kvl)],
        out_shape=[jax.ShapeDtypeStruct((S, W), BF16), jax.ShapeDtypeStruct((1, ql), F32), jax.ShapeDtypeStruct((1, kvl), F32)],
        compiler_params=_cp(("arbitrary",)), name="latent_bwd",
    )(zsm, zsm, gq.reshape(1, ql), gkv.reshape(1, kvl), *tabs, d_qn, d_kvn, dkr)


def _q_rope(q, tabs, transpose, name, gain=1.0):
    S, W = q.shape
    H = W // HEAD_PAD
    tm = _pick(S, ROW_T, 8)
    fn = _rope_t if transpose else _rope

    def body(q_ref, c_ref, sa_ref, sb_ref, o_ref):
        c, sa, sb = c_ref[...], sa_ref[...], sb_ref[...]
        if gain != 1.0:
            c, sa, sb = c * gain, sa * gain, sb * gain
        for h in range(H):
            lo = h * HEAD_PAD
            nope = q_ref[:, lo:lo + QK_NOPE]
            o_ref[:, lo:lo + QK_NOPE] = (nope if gain == 1.0 else nope * gain).astype(BF16)
            o_ref[:, lo + QK_NOPE:lo + HEAD_PAD] = fn(q_ref[:, lo + QK_NOPE:lo + HEAD_PAD], c, sa, sb).astype(BF16)

    return pl.pallas_call(
        body, grid=(S // tm,), in_specs=[_rows(W)(tm)] + [_rows(LANES)(tm)] * 3, out_specs=_rows(W)(tm),
        out_shape=jax.ShapeDtypeStruct((S, W), BF16), compiler_params=_cp(("parallel",)), name=name,
    )(q, *tabs)


def _merge_fwd(zbig, y_mla, y_lru, D):
    S = y_mla.shape[0]
    tm = _pick(S, ROW_T, 8)

    def body(mm_ref, ml_ref, ym_ref, yl_ref, o_ref):
        o_ref[...] = (_sig(mm_ref[...]) * ym_ref[...] + _sig(ml_ref[...]) * yl_ref[...]).astype(BF16)

    return pl.pallas_call(
        body, grid=(S // tm,), in_specs=[_rows(D, 3)(tm), _rows(D, 4)(tm), _rows(D)(tm), _rows(D)(tm)], out_specs=_rows(D)(tm),
        out_shape=jax.ShapeDtypeStruct((S, D), BF16), compiler_params=_cp(("parallel",)), name="merge_fwd",
    )(zbig, zbig, y_mla, y_lru)


def _merge_bwd(zbig, y_mla, y_lru, d_merged, D):
    S = y_mla.shape[0]
    tm = _pick(S, ROW_T, 8)

    def body(mm_ref, ml_ref, ym_ref, yl_ref, d_ref, dym_ref, dyl_ref, dmm_ref, dml_ref):
        d = d_ref[...]
        sm, sl = _sig(mm_ref[...]), _sig(ml_ref[...])
        dym_ref[...] = (d * sm).astype(BF16)
        dyl_ref[...] = (d * sl).astype(BF16)
        dmm_ref[...] = (d * ym_ref[...] * sm * (1.0 - sm)).astype(BF16)
        dml_ref[...] = (d * yl_ref[...] * sl * (1.0 - sl)).astype(BF16)

    o = jax.ShapeDtypeStruct((S, D), BF16)
    return pl.pallas_call(
        body, grid=(S // tm,), in_specs=[_rows(D, 3)(tm), _rows(D, 4)(tm)] + [_rows(D)(tm)] * 3, out_specs=[_rows(D)(tm)] * 4,
        out_shape=[o] * 4, compiler_params=_cp(("parallel",)), name="merge_bwd",
    )(zbig, zbig, y_mla, y_lru, d_merged)


def _ple_fwd(x1, pe, pg):
    S, D = x1.shape
    tm = _pick(S, ROW_T, 8)

    def body(x_ref, pe_ref, pg_ref, o_ref):
        o_ref[...] = x_ref[...] + pe_ref[...] * _sig(pg_ref[...])

    return pl.pallas_call(
        body, grid=(S // tm,), in_specs=[_rows(D)(tm)] * 3, out_specs=_rows(D)(tm),
        out_shape=jax.ShapeDtypeStruct((S, D), F32), compiler_params=_cp(("parallel",)), name="ple_fwd",
    )(x1, pe, pg)


def _ple_bwd(dx2, pe, pg):
    S, D = dx2.shape
    tm = _pick(S, ROW_T, 8)

    def body(d_ref, pe_ref, pg_ref, dpe_ref, dpg_ref):
        d = d_ref[...]
        s = _sig(pg_ref[...])
        dpe_ref[...] = (d * s).astype(BF16)
        dpg_ref[...] = (d * pe_ref[...] * s * (1.0 - s)).astype(BF16)

    o = jax.ShapeDtypeStruct((S, D), BF16)
    return pl.pallas_call(
        body, grid=(S // tm,), in_specs=[_rows(D)(tm)] * 3, out_specs=[_rows(D)(tm)] * 2, out_shape=[o] * 2,
        compiler_params=_cp(("parallel",)), name="ple_bwd",
    )(dx2, pe, pg)


def _loss_head(x, g, target):
    S, D = x.shape
    tm = _pick(S, ROW_T, 8)

    def body(x_ref, g_ref, t_ref, dx_ref, dg_ref, loss_ref):
        first = pl.program_id(0) == 0
        xv, gv = x_ref[...], g_ref[...]
        rs = lax.rsqrt(jnp.mean(xv * xv, axis=-1, keepdims=True) + EPS)
        e = xv * rs * gv - t_ref[...]
        part = 0.5 * jnp.sum(jnp.mean(e * e, axis=-1, keepdims=True), axis=0, keepdims=True)
        dx, dg = _rms_bwd_math(xv, gv, e * (1.0 / D))
        dx_ref[...] = dx
        _accum(dg_ref, dg, first)
        _accum(loss_ref, jnp.broadcast_to(part, (1, LANES)), first)

    return pl.pallas_call(
        body, grid=(S // tm,), in_specs=[_rows(D)(tm), _par(D), _rows(D)(tm)], out_specs=[_rows(D)(tm), _par(D), _par(LANES)],
        out_shape=[jax.ShapeDtypeStruct((S, D), F32), jax.ShapeDtypeStruct((1, D), F32), jax.ShapeDtypeStruct((1, LANES), F32)],
        compiler_params=_cp(("arbitrary",)), name="loss_head",
    )(x, g.reshape(1, D), target)


def _attn_gate_bwd(d_a, o, zbig, H):
    S, W = o.shape
    tm = _pick(S, ROW_T, 8)

    def body(da_ref, o_ref, g_ref, do_ref, dg_ref, dl_ref):
        da, ov, g = da_ref[...], o_ref[...], g_ref[...]
        s = _sig(g)
        do = da * g * s
        do_ref[...] = do.astype(BF16)
        dg_ref[...] = (da * ov * s * (1.0 + g * (1.0 - s))).astype(BF16)
        prod = do * ov
        for h in range(H):
            r = jnp.sum(prod[:, h * V_HEAD:(h + 1) * V_HEAD], axis=-1, keepdims=True)
            dl_ref[h] = jnp.broadcast_to(r, (tm, LANES))

    return pl.pallas_call(
        body, grid=(S // tm,), in_specs=[_rows(W)(tm), _rows(W)(tm), _rows(W, 0)(tm)],
        out_specs=[_rows(W)(tm), _rows(W)(tm), pl.BlockSpec((H, tm, LANES), lambda r: (0, r, 0))],
        out_shape=[jax.ShapeDtypeStruct((S, W), BF16), jax.ShapeDtypeStruct((S, W), BF16), jax.ShapeDtypeStruct((H, S, LANES), F32)],
        compiler_params=_cp(("parallel",)), name="attn_gate_bwd",
    )(d_a, o, zbig)


def _chunk_mask(ts, t, lo, q_rows):
    r = (lax.broadcasted_iota(jnp.int32, (ts, t), 0) + lo) // CHUNK
    c = lax.broadcasted_iota(jnp.int32, (ts, t), 1) // CHUNK
    return (c <= r) if q_rows else (r <= c)


def _attn_tiles(S):
    t = _pick(S, ATT_T)
    ts = _pick(t, ATT_SUB, 8)
    return t, ts, [r * ts for r in range(t // ts)]


QK_SCALE = 1.0 / (QK_NOPE + QK_ROPE) ** 0.5
LOG2E = 1.4426950408889634


def _attn_fwd(q, kv, kr, zbig, H):
    S = q.shape[0]
    t, ts, subs = _attn_tiles(S)
    nq = S // t

    def body(q_ref, kn_ref, v_ref, kr_ref, g_ref, o_ref, a_ref, lse_ref, m_scr, l_scr, acc_scr, s0_scr, s1_scr, p0_scr, p1_scr):
        i = pl.program_id(1)
        m_scr[...] = jnp.full((t, LANES), -1e30, F32)
        l_scr[...] = jnp.zeros((t, LANES), F32)
        acc_scr[...] = jnp.zeros((t, V_HEAD), F32)

        s_buf, p_buf = (s0_scr, s1_scr), (p0_scr, p1_scr)

        def logits(j, slot):
            ks = pl.multiple_of(j * t, t)
            k = jnp.concatenate([kn_ref[pl.ds(ks, t), :], kr_ref[pl.ds(ks, t), :]], axis=1)
            s_buf[slot][...] = lax.dot_general(q_ref[...], k, NT_DIMS, preferred_element_type=F32)

        def values(j, slot):
            ks = pl.multiple_of(jnp.maximum(j, 0) * t, t)
            acc_scr[...] += jnp.dot(p_buf[slot][...], v_ref[pl.ds(ks, t), :], preferred_element_type=F32)

        def softmax(slot, masked):
            s = s_buf[slot][...]
            if masked:
                s = jnp.where(_chunk_mask(t, t, 0, True), s, -1e30)
            m_prev = m_scr[...]
            m_next = jnp.maximum(m_prev, jnp.max(s, axis=1, keepdims=True))
            p = jnp.exp2(s - jnp.tile(m_next, (1, t // LANES)))
            alpha = jnp.exp2(m_prev - m_next)
            l_scr[...] = alpha * l_scr[...] + jnp.sum(p, axis=1, keepdims=True)
            m_scr[...] = m_next
            acc_scr[...] = acc_scr[...] * alpha
            p_buf[slot][...] = p.astype(BF16)

        def step(j, slot, masked=False, more=True):
            if more:
                logits(j + 1, 1 - slot)
            values(j - 1, 1 - slot)
            softmax(slot, masked)

        logits(0, 0)
        p1_scr[...] = jnp.zeros((t, t), BF16)

        def loop(a, carry):
            step(2 * a, 0)
            step(2 * a + 1, 1)
            return carry

        lax.fori_loop(0, i // 2, loop, 0)

        @pl.when(i % 2 == 0)
        def _():
            step(i, 0, masked=True, more=False)
            values(i, 0)

        @pl.when(i % 2 == 1)
        def _():
            step(i - 1, 0)
            step(i, 1, masked=True, more=False)
            values(i, 1)

        l = l_scr[...]
        ov = acc_scr[...] / l
        g = g_ref[...]
        o_ref[...] = ov
        a_ref[...] = (ov * g * _sig(g)).astype(BF16)
        lse_ref[0] = m_scr[...] + jnp.log(l) * LOG2E

    head_col = lambda w, off: pl.BlockSpec((S, w), lambda h, i: (0, 2 * h + off))
    return pl.pallas_call(
        body, grid=(H, nq),
        in_specs=[pl.BlockSpec((t, HEAD_PAD), lambda h, i: (i, h)), head_col(QK_NOPE, 0), head_col(V_HEAD, 1),
                  pl.BlockSpec((S, LANES), lambda h, i: (0, 0)), pl.BlockSpec((t, V_HEAD), lambda h, i: (i, h))],
        out_specs=[pl.BlockSpec((t, V_HEAD), lambda h, i: (i, h)), pl.BlockSpec((t, V_HEAD), lambda h, i: (i, h)),
                   pl.BlockSpec((1, t, LANES), lambda h, i: (h, i, 0))],
        out_shape=[jax.ShapeDtypeStruct((S, H * V_HEAD), F32), jax.ShapeDtypeStruct((S, H * V_HEAD), BF16),
                   jax.ShapeDtypeStruct((H, S, LANES), F32)],
        scratch_shapes=[pltpu.VMEM((t, LANES), F32), pltpu.VMEM((t, LANES), F32), pltpu.VMEM((t, V_HEAD), F32),
                        pltpu.VMEM((t, t), F32), pltpu.VMEM((t, t), F32), pltpu.VMEM((t, t), BF16), pltpu.VMEM((t, t), BF16)],
        compiler_params=_cp(("parallel", "arbitrary")), name="attn_fwd",
    )(q, kv, kv, kr, zbig)


TN_DIMS = (((0,), (0,)), ((), ()))


def _attn_bwd(q, kv, kr, do, lse_row, delta_row, H):
    S = q.shape[0]
    t, _, _ = _attn_tiles(S)
    nk = S // t

    def body(kn_ref, v_ref, kr_ref, q_ref, do_ref, lse_ref, dl_ref, dq_ref, dkv_ref, dkr_ref, dk_scr, dv_scr,
             st0_scr, st1_scr, dp0_scr, dp1_scr):
        j = pl.program_id(1)
        dk_scr[...] = jnp.zeros((t, HEAD_PAD), F32)
        dv_scr[...] = jnp.zeros((t, V_HEAD), F32)

        @pl.when(j == 0)
        def _():
            dq_ref[...] = jnp.zeros((S, HEAD_PAD), F32)

        st_buf, dp_buf = (st0_scr, st1_scr), (dp0_scr, dp1_scr)

        def keys():
            return jnp.concatenate([kn_ref[...], kr_ref[...]], axis=1)

        def rows(i):
            return pl.ds(pl.multiple_of(jnp.minimum(i, nk - 1) * t, t), t)

        def scores(i, slot):
            st_buf[slot][...] = lax.dot_general(keys(), q_ref[rows(i), :], NT_DIMS, preferred_element_type=F32)
            dp_buf[slot][...] = lax.dot_general(v_ref[...], do_ref[rows(i), :], NT_DIMS, preferred_element_type=F32)

        def step(s, slot, masked=False):
            i = j + s
            scores(i + 1, 1 - slot)
            pt = jnp.exp2(st_buf[slot][...] - lse_ref[0, :, rows(i)])
            if masked:
                pt = jnp.where(_chunk_mask(t, t, 0, False), pt, 0.0)
            dst = (pt * (dp_buf[slot][...] - dl_ref[0, :, rows(i)])).astype(BF16)
            dv_scr[...] += jnp.dot(pt.astype(BF16), do_ref[rows(i), :], preferred_element_type=F32)
            dk_scr[...] += jnp.dot(dst, q_ref[rows(i), :], preferred_element_type=F32)
            dq_ref[rows(i), :] += lax.dot_general(dst, keys(), TN_DIMS, preferred_element_type=F32)

        n = nk - j
        scores(j, 0)
        step(0, 0, masked=True)

        def loop(a, carry):
            step(2 * a + 1, 1)
            step(2 * a + 2, 0)
            return carry

        lax.fori_loop(0, (n - 1) // 2, loop, 0)

        @pl.when((n - 1) % 2 == 1)
        def _():
            step(n - 1, 1)

        dk = dk_scr[...] * (1.0 / LOG2E)
        dkv_ref[:, 0:QK_NOPE] = dk[:, 0:QK_NOPE].astype(BF16)
        dkv_ref[:, QK_NOPE:] = dv_scr[...].astype(BF16)
        dkr_ref[0] = dk[:, QK_NOPE:]

    tile_col = lambda w, off: pl.BlockSpec((t, w), lambda h, j: (j, 2 * h + off))
    row = pl.BlockSpec((1, 1, S), lambda h, j: (h, 0, 0))
    return pl.pallas_call(
        body, grid=(H, nk),
        in_specs=[tile_col(QK_NOPE, 0), tile_col(V_HEAD, 1), pl.BlockSpec((t, LANES), lambda h, j: (j, 0)),
                  pl.BlockSpec((S, HEAD_PAD), lambda h, j: (0, h)), pl.BlockSpec((S, V_HEAD), lambda h, j: (0, h)), row, row],
        out_specs=[pl.BlockSpec((S, HEAD_PAD), lambda h, j: (0, h)), pl.BlockSpec((t, HEAD_PAD), lambda h, j: (j, h)),
                   pl.BlockSpec((1, t, LANES), lambda h, j: (h, j, 0))],
        out_shape=[jax.ShapeDtypeStruct((S, H * HEAD_PAD), F32), jax.ShapeDtypeStruct((S, H * HEAD_PAD), BF16),
                   jax.ShapeDtypeStruct((H, S, LANES), F32)],
        scratch_shapes=[pltpu.VMEM((t, HEAD_PAD), F32), pltpu.VMEM((t, V_HEAD), F32)] + [pltpu.VMEM((t, t), F32)] * 4,
        compiler_params=_cp(("arbitrary", "arbitrary")), name="attn_bwd",
    )(kv, kv, kr, q, do, lse_row, delta_row)


def _shift_down(x, prev8, s):
    rx = pltpu.roll(x, s, 0)
    rp = pltpu.roll(prev8, s, 0)
    rows = lax.broadcasted_iota(jnp.int32, rp.shape, 0)
    return jnp.concatenate([jnp.where(rows < s, rp, rx[:8]), rx[8:]], axis=0)


def _shift_up(x, next8, s):
    n = x.shape[0]
    rx = pltpu.roll(x, n - s, 0)
    rn = pltpu.roll(next8, 8 - s, 0)
    rows = lax.broadcasted_iota(jnp.int32, rn.shape, 0)
    return jnp.concatenate([rx[:n - 8], jnp.where(rows >= 8 - s, rn, rx[n - 8:])], axis=0)


def _scan_rows(a, b, up):
    n = a.shape[0]
    rows = lax.broadcasted_iota(jnp.int32, a.shape, 0)
    d = 1
    while d < n:
        keep = (rows < n - d) if up else (rows >= d)
        sh = n - d if up else d
        a_s = jnp.where(keep, pltpu.roll(a, sh, 0), 1.0)
        b_s = jnp.where(keep, pltpu.roll(b, sh, 0), 0.0)
        b = a * b_s + b
        a = a * a_s
        d *= 2
    return a, b


def _log1p(e):
    u = 1.0 + e
    return jnp.where(u == 1.0, e, jnp.log(u) * (e / (u - 1.0)))


def _lru_pre(u, prev8, cw_ref, cb_ref, wr_ref, br_ref, wi_ref, bi_ref, lam_ref):
    us = [u, _shift_down(u, prev8, 1), _shift_down(u, prev8, 2), _shift_down(u, prev8, 3)]
    xc = cb_ref[...] + cw_ref[3:4, :] * us[0] + cw_ref[2:3, :] * us[1] + cw_ref[1:2, :] * us[2] + cw_ref[0:1, :] * us[3]
    x16 = xc.astype(BF16)
    nb = xc.shape[1] // LRU_BLOCK_DIM
    blk = lambda k: slice(k * LRU_BLOCK_DIM, (k + 1) * LRU_BLOCK_DIM)
    pr = jnp.concatenate([jnp.dot(x16[:, blk(k)], wr_ref[k].astype(BF16), preferred_element_type=F32) for k in range(nb)], axis=1)
    pi = jnp.concatenate([jnp.dot(x16[:, blk(k)], wi_ref[k].astype(BF16), preferred_element_type=F32) for k in range(nb)], axis=1)
    r = _sig(pr + br_ref[...])
    i = _sig(pi + bi_ref[...])
    nlam = -lam_ref[...]
    sp = jnp.maximum(nlam, 0.0) + _log1p(jnp.exp(-jnp.abs(nlam)))
    log_a = (-LRU_C * r) * sp
    a = jnp.exp(log_a)
    mult = jnp.sqrt(-jnp.tanh(log_a) * (a * a + 1.0))
    return us, xc, x16, r, i, sp, a, mult


def _lru_specs(D, cb, tt, nT, rev):
    nb = cb // LRU_BLOCK_DIM
    tmap = (lambda t: nT - 1 - t) if rev else (lambda t: t)
    ncb = D // cb

    def tile(piece):
        return pl.BlockSpec((tt, cb), lambda c, t: (tmap(t), piece * ncb + c))

    def halo(piece):
        return pl.BlockSpec((8, cb), lambda c, t: (jnp.maximum(tmap(t) * (tt // 8) - 1, 0), piece * ncb + c))

    par = lambda rows: pl.BlockSpec((rows, cb), lambda c, t: (0, c))
    wblk = pl.BlockSpec((nb, LRU_BLOCK_DIM, LRU_BLOCK_DIM), lambda c, t: (c, 0, 0))
    return tile, halo, par, wblk, tmap


def _lru_fwd(zbig, cw, cbias, wr, br, wi, bi, lam, D):
    S = zbig.shape[0]
    tt, cb = _pick(S, LRU_TT, 8), _pick(D, LRU_CB)
    nT = S // tt
    tile, halo, par, wblk, _ = _lru_specs(D, cb, tt, nT, False)

    def body(u_ref, up_ref, g_ref, cw_ref, cb_ref, wr_ref, br_ref, wi_ref, bi_ref, lam_ref, h_ref, al_ref, carry):
        t = pl.program_id(1)
        prev8 = jnp.where(t > 0, up_ref[...], 0.0)
        _, xc, _, _, i, _, a, mult = _lru_pre(u_ref[...], prev8, cw_ref, cb_ref, wr_ref, br_ref, wi_ref, bi_ref, lam_ref)
        pa, hb = _scan_rows(a, mult * (i * xc), False)
        h0 = jnp.where(t > 0, carry[7:8, :], 0.0)
        h = hb + pa * h0
        h_ref[...] = h
        carry[...] = h[tt - 8:, :]
        g = g_ref[...]
        al_ref[...] = (h * g * _sig(g)).astype(BF16)

    return pl.pallas_call(
        body, grid=(D // cb, nT),
        in_specs=[tile(1), halo(1), tile(2), par(CONV_K), par(1), wblk, par(1), wblk, par(1), par(1)],
        out_specs=[tile(0), tile(0)],
        out_shape=[jax.ShapeDtypeStruct((S, D), F32), jax.ShapeDtypeStruct((S, D), BF16)],
        scratch_shapes=[pltpu.VMEM((8, cb), F32)],
        compiler_params=_cp(("parallel", "arbitrary")), name="lru_fwd",
    )(zbig, zbig, zbig, cw, cbias.reshape(1, D), wr, br.reshape(1, D), wi, bi.reshape(1, D), lam.reshape(1, D))


def _lru_bwd(zbig, h, d_al, cw, cbias, wr, br, wi, bi, lam, D):
    S = zbig.shape[0]
    tt, cb = _pick(S, LRU_TT, 8), _pick(D, LRU_CB)
    nT = S // tt
    nb = cb // LRU_BLOCK_DIM
    tile, halo, par, wblk, tmap = _lru_specs(D, cb, tt, nT, True)

    def body(u_ref, up_ref, g_ref, h_ref, hp_ref, dal_ref, cw_ref, cb_ref, wr_ref, br_ref, wi_ref, bi_ref, lam_ref,
             du_ref, dg_ref, dcw_ref, dcb_ref, dwr_ref, dbr_ref, dwi_ref, dbi_ref, dlam_ref, g_car, a_car, x_car):
        step = pl.program_id(1)
        first = step == 0
        t = nT - 1 - step
        prev8 = jnp.where(t > 0, up_ref[...], 0.0)
        us, xc, x16, r, i, sp, a, mult = _lru_pre(u_ref[...], prev8, cw_ref, cb_ref, wr_ref, br_ref, wi_ref, bi_ref, lam_ref)
        hv = h_ref[...]
        h_m1 = _shift_down(hv, jnp.where(t > 0, hp_ref[...], 0.0), 1)
        g, dal = g_ref[...], dal_ref[...]
        sg = _sig(g)
        dg_ref[...] = (dal * hv * sg * (1.0 + g * (1.0 - sg))).astype(BF16)
        dh = dal * g * sg
        coef = _shift_up(a, jnp.where(first, 0.0, a_car[...]), 1)
        pa, gb = _scan_rows(coef, dh, True)
        G = gb + pa * jnp.where(first, 0.0, g_car[0:1, :])
        g_car[...] = G[:8]
        a_car[...] = a[:8]
        da = G * h_m1
        ixc = i * xc
        dixc = G * mult
        dlog = da * a - (G * ixc) * (a * a) / mult
        dpr = dlog * (-LRU_C * sp) * r * (1.0 - r)
        dpi = dixc * xc * i * (1.0 - i)
        dxc = dixc * i
        dsp = jnp.sum(dlog * (-LRU_C) * r, axis=0, keepdims=True)
        dpr16, dpi16 = dpr.astype(BF16), dpi.astype(BF16)
        blk = lambda k: slice(k * LRU_BLOCK_DIM, (k + 1) * LRU_BLOCK_DIM)
        back = []
        for k in range(nb):
            xk = x16[:, blk(k)].T
            dwr_k = jnp.dot(xk, dpr16[:, blk(k)], preferred_element_type=F32)
            dwi_k = jnp.dot(xk, dpi16[:, blk(k)], preferred_element_type=F32)

            @pl.when(first)
            def _():
                dwr_ref[k] = dwr_k
                dwi_ref[k] = dwi_k

            @pl.when(jnp.logical_not(first))
            def _():
                dwr_ref[k] += dwr_k
                dwi_ref[k] += dwi_k

            back.append(lax.dot_general(dpr16[:, blk(k)], wr_ref[k].astype(BF16), NT_DIMS, preferred_element_type=F32)
                        + lax.dot_general(dpi16[:, blk(k)], wi_ref[k].astype(BF16), NT_DIMS, preferred_element_type=F32))
        dxc = dxc + jnp.concatenate(back, axis=1)
        _accum(dbr_ref, jnp.sum(dpr, axis=0, keepdims=True), first)
        _accum(dbi_ref, jnp.sum(dpi, axis=0, keepdims=True), first)
        _accum(dlam_ref, dsp * (-_sig(-lam_ref[...])), first)
        _accum(dcb_ref, jnp.sum(dxc, axis=0, keepdims=True), first)
        _accum(dcw_ref, jnp.concatenate([jnp.sum(dxc * us[3 - k], axis=0, keepdims=True) for k in range(CONV_K)], axis=0), first)
        nxt = jnp.where(first, 0.0, x_car[...])
        du = cw_ref[3:4, :] * dxc
        for s in range(1, CONV_K):
            du = du + cw_ref[3 - s:4 - s, :] * _shift_up(dxc, nxt, s)
        x_car[...] = dxc[:8]
        du_ref[...] = du.astype(BF16)

    act = jax.ShapeDtypeStruct((S, D), BF16)
    vec = jax.ShapeDtypeStruct((1, D), F32)
    wsh = jax.ShapeDtypeStruct(wr.shape, F32)
    rtile = pl.BlockSpec((tt, cb), lambda c, t: (tmap(t), c))
    rhalo = pl.BlockSpec((8, cb), lambda c, t: (jnp.maximum(tmap(t) * (tt // 8) - 1, 0), c))
    return pl.pallas_call(
        body, grid=(D // cb, nT),
        in_specs=[tile(1), halo(1), tile(2), rtile, rhalo, rtile, par(CONV_K), par(1), wblk, par(1), wblk, par(1), par(1)],
        out_specs=[rtile, rtile, par(CONV_K), par(1), wblk, par(1), wblk, par(1), par(1)],
        out_shape=[act, act, jax.ShapeDtypeStruct((CONV_K, D), F32), vec, wsh, vec, wsh, vec, vec],
        scratch_shapes=[pltpu.VMEM((8, cb), F32)] * 3,
        compiler_params=_cp(("parallel", "arbitrary")), name="lru_bwd",
    )(zbig, zbig, zbig, h, h, d_al, cw, cbias.reshape(1, D), wr, br.reshape(1, D), wi, bi.reshape(1, D), lam.reshape(1, D))


def _adamw(w, g, m, v, name):
    R, C = w.shape
    tm = _pick(R, max(8, ((1 << 18) // C) // 8 * 8), 8)
    c1 = 1.0 - ADAM_B1 ** ADAM_STEP
    c2 = 1.0 - ADAM_B2 ** ADAM_STEP

    def body(w_ref, g_ref, m_ref, v_ref, d_ref, mo_ref, vo_ref):
        gv = g_ref[...]
        mn = ADAM_B1 * m_ref[...] + (1.0 - ADAM_B1) * gv
        vn = ADAM_B2 * v_ref[...] + (1.0 - ADAM_B2) * (gv * gv)
        d_ref[...] = -ADAM_LR * ((mn / c1) / (jnp.sqrt(vn / c2) + ADAM_EPS) + ADAM_WD * w_ref[...])
        mo_ref[...] = mn
        vo_ref[...] = vn

    o = jax.ShapeDtypeStruct((R, C), F32)
    return pl.pallas_call(
        body, grid=(R // tm,), in_specs=[_rows(C)(tm)] * 4, out_specs=[_rows(C)(tm)] * 3, out_shape=[o] * 3,
        compiler_params=_cp(("parallel",)), name=name,
    )(w, g, m, v)


def _sum_slabs(x, name, out_dtype=F32):
    n, R, C = x.shape
    tm = _pick(R, max(8, ((1 << 18) // C) // 8 * 8), 16)

    def body(x_ref, o_ref):
        s = x_ref[0].astype(F32)
        for k in range(1, n):
            s = s + x_ref[k].astype(F32)
        o_ref[...] = s.astype(out_dtype)

    return pl.pallas_call(
        body, grid=(R // tm,), in_specs=[pl.BlockSpec((n, tm, C), lambda r: (0, r, 0))], out_specs=_rows(C)(tm),
        out_shape=jax.ShapeDtypeStruct((R, C), out_dtype), compiler_params=_cp(("parallel",)), name=name,
    )(x)


def _sum_core_halves(g, recv, core):
    n, _, Rh, C = g.shape
    tm = _pick(Rh, max(16, ((1 << 18) // C) // 16 * 16), 16)

    def body(c_ref, g_ref, r_ref, o_ref):
        o_ref[0] = (g_ref[0, 0].astype(F32) + r_ref[0, 0].astype(F32)).astype(BF16)

    return pl.pallas_call(
        body,
        grid_spec=pltpu.PrefetchScalarGridSpec(
            num_scalar_prefetch=1, grid=(n, Rh // tm),
            in_specs=[pl.BlockSpec((1, 1, tm, C), lambda k, r, c_ref: (k, c_ref[0], r, 0)),
                      pl.BlockSpec((1, 1, tm, C), lambda k, r, c_ref: (k, 0, r, 0))],
            out_specs=pl.BlockSpec((1, tm, C), lambda k, r, c_ref: (k, r, 0)),
        ),
        out_shape=jax.ShapeDtypeStruct((n, Rh, C), BF16),
        compiler_params=_cp(("parallel", "parallel")), name="grad_sum_cores",
    )(core.reshape(1).astype(jnp.int32), g, recv)


ANY = pl.BlockSpec(memory_space=pl.ANY)


def _place():
    x, y, c = lax.axis_index("x"), lax.axis_index("y"), lax.axis_index("c")
    chips = [(1 - x, y), (x, 1 - y), (1 - x, 1 - y)]
    return x, y, c, chips


def _allgather_chips(shard, name):
    R, C = shard.shape
    Rh = R // 2

    def body(x_ref, out_ref, send_sems, recv_sems):
        x, y, c, chips = _place()
        me = 2 * x + y
        sibling = (x, y, 1 - c)

        def half(k, hc):
            return out_ref.at[k, pl.ds(hc * Rh, Rh), :]

        first = [pltpu.make_async_remote_copy(
            src_ref=x_ref.at[pl.ds(c * Rh, Rh), :], dst_ref=half(me, c), send_sem=send_sems.at[j], recv_sem=recv_sems.at[j],
            device_id=(*chip, c), device_id_type=MESH) for j, chip in enumerate(chips)]
        for cp in first:
            cp.start()

        def landed(j, chip, hc):
            k = 2 * chip[0] + chip[1]
            return pltpu.make_async_remote_copy(
                src_ref=half(k, hc), dst_ref=half(k, hc), send_sem=send_sems.at[j], recv_sem=recv_sems.at[j],
                device_id=sibling, device_id_type=MESH)

        passed = []
        for j, chip in enumerate(chips):
            landed(j, chip, c).wait_recv()
            cp = landed(3 + j, chip, c)
            cp.start()
            passed.append(cp)
        for j, chip in enumerate(chips):
            landed(3 + j, chip, 1 - c).wait_recv()
        for cp in first + passed:
            cp.wait_send()

    out = pl.pallas_call(
        body, in_specs=[ANY], out_specs=ANY, out_shape=jax.ShapeDtypeStruct((N_CHIPS, R, C), shard.dtype),
        scratch_shapes=[pltpu.SemaphoreType.DMA((6,)), pltpu.SemaphoreType.DMA((6,))],
        name=name,
    )(shard)
    return lax.dynamic_update_slice(out, shard[None], (2 * lax.axis_index("x") + lax.axis_index("y"), 0, 0))


def _allgather_all(blockv, name):
    R, C = blockv.shape

    def body(x_ref, out_ref, send_sems, recv_sems):
        x, y, c, chips = _place()
        sibling = (x, y, 1 - c)

        def slab(px, py, pc):
            return out_ref.at[4 * px + 2 * py + pc]

        def copy(k, block, to, src=None):
            return pltpu.make_async_remote_copy(
                src_ref=slab(*block) if src is None else src, dst_ref=slab(*block), send_sem=send_sems.at[k],
                recv_sem=recv_sems.at[k], device_id=to, device_id_type=MESH)

        first = [copy(0, (x, y, c), sibling, src=x_ref)]
        first += [copy(1 + j, (x, y, c), (*chip, c), src=x_ref) for j, chip in enumerate(chips)]
        for cp in first:
            cp.start()
        passed = [copy(4 + j, (*chip, c), sibling) for j, chip in enumerate(chips)]
        for j, chip in enumerate(chips):
            copy(1 + j, (*chip, c), (x, y, c)).wait_recv()
            passed[j].start()
        copy(0, (x, y, 1 - c), (x, y, c)).wait_recv()
        for j, chip in enumerate(chips):
            copy(4 + j, (*chip, 1 - c), (x, y, c)).wait_recv()
        for cp in first + passed:
            cp.wait_send()

    out = pl.pallas_call(
        body, in_specs=[ANY], out_specs=ANY, out_shape=jax.ShapeDtypeStruct((N_DEV, R, C), blockv.dtype),
        scratch_shapes=[pltpu.SemaphoreType.DMA((7,)), pltpu.SemaphoreType.DMA((7,))],
        name=name,
    )(blockv)
    me = 4 * lax.axis_index("x") + 2 * lax.axis_index("y") + lax.axis_index("c")
    return lax.dynamic_update_slice(out, blockv[None], (me, 0, 0))


def _swap_cores_half(g):
    n, _, Rh, C = g.shape

    def body(g_ref, out_ref, send_sem, recv_sem):
        x, y, c, _ = _place()
        cp = pltpu.make_async_remote_copy(
            src_ref=g_ref.at[:, pl.ds(1 - c, 1)], dst_ref=out_ref, send_sem=send_sem, recv_sem=recv_sem,
            device_id=(x, y, 1 - c), device_id_type=MESH)
        cp.start()
        cp.wait()

    return pl.pallas_call(
        body, in_specs=[ANY], out_specs=ANY, out_shape=jax.ShapeDtypeStruct((n, 1, Rh, C), g.dtype),
        scratch_shapes=[pltpu.SemaphoreType.DMA, pltpu.SemaphoreType.DMA], name="grad_swap_cores",
    )(g)


def _alltoall_chips(s):
    n, Rh, C = s.shape

    def body(s_ref, out_ref, send_sems, recv_sems):
        x, y, c, chips = _place()
        me = 2 * x + y
        sent = []
        for j, chip in enumerate(chips):
            k = 2 * chip[0] + chip[1]
            cp = pltpu.make_async_remote_copy(
                src_ref=s_ref.at[k], dst_ref=out_ref.at[me], send_sem=send_sems.at[j], recv_sem=recv_sems.at[j],
                device_id=(*chip, c), device_id_type=MESH)
            cp.start()
            sent.append(cp)
        for j, chip in enumerate(chips):
            k = 2 * chip[0] + chip[1]
            pltpu.make_async_remote_copy(
                src_ref=s_ref.at[k], dst_ref=out_ref.at[k], send_sem=send_sems.at[j], recv_sem=recv_sems.at[j],
                device_id=(*chip, c), device_id_type=MESH).wait_recv()
        for cp in sent:
            cp.wait_send()

    out = pl.pallas_call(
        body, in_specs=[ANY], out_specs=ANY, out_shape=jax.ShapeDtypeStruct((n, Rh, C), s.dtype),
        scratch_shapes=[pltpu.SemaphoreType.DMA((3,)), pltpu.SemaphoreType.DMA((3,))],
        name="grad_alltoall_chips",
    )(s)
    me = 2 * lax.axis_index("x") + lax.axis_index("y")
    return lax.dynamic_update_slice(out, lax.dynamic_slice_in_dim(s, me, 1, axis=0), (me, 0, 0))


def _join_core_halves(half):
    Rh, C = half.shape

    def body(h_ref, out_ref, send_sem, recv_sem):
        x, y, c, _ = _place()
        cp = pltpu.make_async_remote_copy(
            src_ref=h_ref, dst_ref=out_ref.at[c], send_sem=send_sem, recv_sem=recv_sem,
            device_id=(x, y, 1 - c), device_id_type=MESH)
        cp.start()
        pltpu.make_async_remote_copy(
            src_ref=h_ref, dst_ref=out_ref.at[1 - c], send_sem=send_sem, recv_sem=recv_sem,
            device_id=(x, y, 1 - c), device_id_type=MESH).wait_recv()
        cp.wait_send()

    out = pl.pallas_call(
        body, in_specs=[ANY], out_specs=ANY, out_shape=jax.ShapeDtypeStruct((2, Rh, C), half.dtype),
        scratch_shapes=[pltpu.SemaphoreType.DMA, pltpu.SemaphoreType.DMA], name="grad_join_cores",
    )(half)
    return lax.dynamic_update_slice(out, half[None], (lax.axis_index("c"), 0, 0))


def _pack(arrays, cols, row_align):
    flat = jnp.concatenate([a.reshape(-1) for a in arrays])
    unit = cols * row_align
    total = -(-flat.size // unit) * unit
    return jnp.pad(flat, (0, total - flat.size)).reshape(total // cols, cols)


def _unpack(buf, shapes):
    flat = buf.reshape(-1)
    out, off = [], 0
    for shp in shapes:
        n = 1
        for d in shp:
            n *= d
        out.append(flat[off:off + n].reshape(shp))
        off += n
    return out


def _layer_fwd(x, p_l, w, tabs, dm):
    D, H, ql, kvl = dm["D"], dm["H"], dm["ql"], dm["kvl"]
    h = _rms_fwd(x, w["attn_norm"], "attn_norm_fwd")
    zbig = _mm(h, w["w_big"], "nn", "in_proj_big")
    zsm = _mm(h, w["w_sm"], "nn", "in_proj_small")
    qn, kvn, kr = _latent_fwd(zsm, w["q_a_norm"], w["kv_a_norm"], tabs, ql, kvl)
    q = _q_rope(_mm(qn, w["w_q"], "nn", "q_proj"), tabs, False, "q_rope_fwd", gain=QK_SCALE * LOG2E)
    kv = _mm(kvn, w["w_kv"], "nn", "kv_proj", out_dtype=BF16)
    o, a_mla, lse = _attn_fwd(q, kv, kr, zbig, H)
    y_mla = _mm(a_mla, w["w_o_mla"], "nn", "o_mla_proj")
    h_lru, a_lru = _lru_fwd(zbig, w["conv_w"], w["conv_b"], w["w_rg"], w["b_rg"], w["w_ig"], w["b_ig"], w["lru_lambda"], D)
    y_lru = _mm(a_lru, w["w_o_lru"], "nn", "o_lru_proj")
    merged = _merge_fwd(zbig, y_mla, y_lru, D)
    x1 = _mm(merged, w["w_out"], "nn", "out_proj", add=x)
    hp = _rms_fwd(x1, w["ple_norm"], "ple_norm_fwd")
    pg = _mm(hp, w["w_ple_gate"], "nn", "ple_gate_proj")
    pe = _mm(p_l, w["w_ple"], "nn", "ple_proj")
    x2 = _ple_fwd(x1, pe, pg)
    res = dict(x=x, h=h, zbig=zbig, zsm=zsm, qn=qn, kvn=kvn, kr=kr, q=q, kv=kv, o=o, a_mla=a_mla, lse=lse, y_mla=y_mla,
               h_lru=h_lru, a_lru=a_lru, y_lru=y_lru, merged=merged, x1=x1, hp=hp, pg=pg, pe=pe, p=p_l)
    return x2, res


def _layer_bwd(dx2, r, w, tabs, dm):
    D, H, ql, kvl = dm["D"], dm["H"], dm["ql"], dm["kvl"]
    S = dx2.shape[0]
    g = {}
    d_pe, d_pg = _ple_bwd(dx2, r["pe"], r["pg"])
    g["w_ple"] = _mm(r["p"], d_pe, "tn", "ple_proj_dw", out_dtype=BF16)
    g["w_ple_gate"] = _mm(r["hp"], d_pg, "tn", "ple_gate_dw", out_dtype=BF16)
    d_hp = _mm(d_pg, w["w_ple_gate"], "nt", "ple_gate_dx")
    dx1, g["ple_norm"] = _rms_bwd(r["x1"], w["ple_norm"], d_hp, dx2, "ple_norm_bwd")
    g["w_out"] = _mm(r["merged"], dx1, "tn", "out_proj_dw", out_dtype=BF16)
    d_merged = _mm(dx1, w["w_out"], "nt", "out_proj_dx")
    d_ym, d_yl, d_mm, d_ml = _merge_bwd(r["zbig"], r["y_mla"], r["y_lru"], d_merged, D)
    g["w_o_mla"] = _mm(r["a_mla"], d_ym, "tn", "o_mla_dw", out_dtype=BF16)
    d_a_mla = _mm(d_ym, w["w_o_mla"], "nt", "o_mla_dx")
    g["w_o_lru"] = _mm(r["a_lru"], d_yl, "tn", "o_lru_dw", out_dtype=BF16)
    d_a_lru = _mm(d_yl, w["w_o_lru"], "nt", "o_lru_dx")
    d_o, d_gm, delta = _attn_gate_bwd(d_a_mla, r["o"], r["zbig"], H)
    dq, dkv, dkr = _attn_bwd(r["q"], r["kv"], r["kr"], d_o, r["lse"][:, :, 0].reshape(H, 1, S), delta[:, :, 0].reshape(H, 1, S), H)
    dq_pre = _q_rope(dq, tabs, True, "q_rope_bwd", gain=QK_SCALE)
    g["w_q"] = _mm(r["qn"], dq_pre, "tn", "q_proj_dw", out_dtype=BF16)
    d_qn = _mm(dq_pre, w["w_q"], "nt", "q_proj_dx")
    g["w_kv"] = _mm(r["kvn"], dkv, "tn", "kv_proj_dw", out_dtype=BF16)
    d_kvn = _mm(dkv, w["w_kv"], "nt", "kv_proj_dx")
    dzsm, g["q_a_norm"], g["kv_a_norm"] = _latent_bwd(r["zsm"], w["q_a_norm"], w["kv_a_norm"], tabs, d_qn, d_kvn, dkr, ql, kvl)
    (d_u, d_gl, g["conv_w"], g["conv_b"], g["w_rg"], g["b_rg"], g["w_ig"], g["b_ig"], g["lru_lambda"]) = _lru_bwd(
        r["zbig"], r["h_lru"], d_a_lru, w["conv_w"], w["conv_b"], w["w_rg"], w["b_rg"], w["w_ig"], w["b_ig"], w["lru_lambda"], D)
    dzbig = jnp.concatenate([d_gm, d_u, d_gl, d_mm, d_ml], axis=1)
    g["w_big"] = _mm(r["h"], dzbig, "tn", "in_proj_big_dw", out_dtype=BF16)
    g["w_sm"] = _mm(r["h"], dzsm, "tn", "in_proj_small_dw", out_dtype=BF16)
    dh = _mm(dzbig, w["w_big"], "nt", "in_proj_big_dx")
    dh = _mm(dzsm, w["w_sm"], "nt", "in_proj_small_dx", add=dh)
    dx, g["attn_norm"] = _rms_bwd(r["x"], w["attn_norm"], dh, dx1, "attn_norm_bwd")
    return dx, g


SHARDED = ("w_in", "w_q_b", "w_kv_b", "w_o_mla", "w_o_lru", "w_out", "w_ple_gate", "w_ple")
COL_SHARDED = ("w_in", "w_q_b", "w_kv_b", "w_ple")
REPLICATED = ("attn_norm", "q_a_norm", "kv_a_norm", "conv_b", "w_rg", "b_rg", "w_ig", "b_ig", "lru_lambda", "ple_norm", "final_norm")
WEIGHTS = ("attn_norm", "w_in", "q_a_norm", "w_q_b", "kv_a_norm", "w_kv_b", "conv_w", "conv_b", "w_rg", "b_rg", "w_ig", "b_ig",
           "lru_lambda", "w_o_mla", "w_o_lru", "w_out", "ple_norm", "w_ple_gate", "w_ple", "final_norm")


def kernel(x, p, positions, attn_norm, w_in, q_a_norm, w_q_b, kv_a_norm, w_kv_b, conv_w, conv_b, w_rg, b_rg, w_ig, b_ig, lru_lambda, w_o_mla, w_o_lru, w_out, ple_norm, w_ple_gate, w_ple, final_norm, loss_target, m_attn_norm, m_w_in, m_q_a_norm, m_w_q_b, m_kv_a_norm, m_w_kv_b, m_conv_w, m_conv_b, m_w_rg, m_b_rg, m_w_ig, m_b_ig, m_lru_lambda, m_w_o_mla, m_w_o_lru, m_w_out, m_ple_norm, m_w_ple_gate, m_w_ple, m_final_norm, v_attn_norm, v_w_in, v_q_a_norm, v_w_q_b, v_kv_a_norm, v_w_kv_b, v_conv_w, v_conv_b, v_w_rg, v_b_rg, v_w_ig, v_b_ig, v_lru_lambda, v_w_o_mla, v_w_o_lru, v_w_out, v_ple_norm, v_w_ple_gate, v_w_ple, v_final_norm):
    W = dict(attn_norm=attn_norm, w_in=w_in, q_a_norm=q_a_norm, w_q_b=w_q_b, kv_a_norm=kv_a_norm, w_kv_b=w_kv_b, conv_w=conv_w,
             conv_b=conv_b, w_rg=w_rg, b_rg=b_rg, w_ig=w_ig, b_ig=b_ig, lru_lambda=lru_lambda, w_o_mla=w_o_mla, w_o_lru=w_o_lru,
             w_out=w_out, ple_norm=ple_norm, w_ple_gate=w_ple_gate, w_ple=w_ple, final_norm=final_norm)
    M = dict(attn_norm=m_attn_norm, w_in=m_w_in, q_a_norm=m_q_a_norm, w_q_b=m_w_q_b, kv_a_norm=m_kv_a_norm, w_kv_b=m_w_kv_b,
             conv_w=m_conv_w, conv_b=m_conv_b, w_rg=m_w_rg, b_rg=m_b_rg, w_ig=m_w_ig, b_ig=m_b_ig, lru_lambda=m_lru_lambda,
             w_o_mla=m_w_o_mla, w_o_lru=m_w_o_lru, w_out=m_w_out, ple_norm=m_ple_norm, w_ple_gate=m_w_ple_gate, w_ple=m_w_ple,
             final_norm=m_final_norm)
    V = dict(attn_norm=v_attn_norm, w_in=v_w_in, q_a_norm=v_q_a_norm, w_q_b=v_w_q_b, kv_a_norm=v_kv_a_norm, w_kv_b=v_w_kv_b,
             conv_w=v_conv_w, conv_b=v_conv_b, w_rg=v_w_rg, b_rg=v_b_rg, w_ig=v_w_ig, b_ig=v_b_ig, lru_lambda=v_lru_lambda,
             w_o_mla=v_w_o_mla, w_o_lru=v_w_o_lru, w_out=v_w_out, ple_norm=v_ple_norm, w_ple_gate=v_w_ple_gate, w_ple=v_w_ple,
             final_norm=v_final_norm)
    depth = attn_norm.shape[0]
    S, D = x.shape[1], x.shape[2]
    ql, kvl = q_a_norm.shape[1], kv_a_norm.shape[1]
    H = w_q_b.shape[2] * N_CHIPS // (QK_NOPE + QK_ROPE)
    dm = dict(D=D, H=H, ql=ql, kvl=kvl)
    chip = 2 * lax.axis_index("x") + lax.axis_index("y")
    core = lax.axis_index("c")

    shard_shapes = [W[n].shape for n in SHARDED]
    gathered = _allgather_chips(_pack([W[n].astype(BF16) for n in SHARDED], PACK_C, PACK_ROWS), "weights_allgather")
    slabs = [dict(zip(SHARDED, _unpack(gathered[k], shard_shapes))) for k in range(N_CHIPS)]
    cw_all = _allgather_chips(_pack([conv_w], LANES, 16), "conv_w_allgather")
    conv_w_full = jnp.concatenate([_unpack(cw_all[k], [conv_w.shape])[0] for k in range(N_CHIPS)], axis=-1)

    n_small = ql + kvl + QK_ROPE
    hpc = H // N_CHIPS
    head_pad = ((0, 0), (0, 0), (0, HEAD_PAD - QK_NOPE - QK_ROPE))
    layers = []
    for l in range(depth):
        cat = lambda n, axis: jnp.concatenate([s[n][l] for s in slabs], axis=axis)
        layers.append(dict(
            w_big=jnp.concatenate([slabs[0]["w_in"][l][:, n_small:]] + [s["w_in"][l] for s in slabs[1:]], axis=1),
            w_sm=jnp.pad(slabs[0]["w_in"][l][:, :n_small], ((0, 0), (0, LANES - QK_ROPE))),
            w_q=jnp.concatenate([jnp.pad(s["w_q_b"][l].reshape(ql, hpc, QK_NOPE + QK_ROPE), head_pad).reshape(ql, hpc * HEAD_PAD)
                                 for s in slabs], axis=1),
            w_kv=cat("w_kv_b", 1), w_o_mla=cat("w_o_mla", 0), w_o_lru=cat("w_o_lru", 0), w_out=cat("w_out", 0),
            w_ple_gate=cat("w_ple_gate", 0), w_ple=cat("w_ple", 1), conv_w=conv_w_full[l],
            **{n: W[n][l] for n in REPLICATED if n != "final_norm"}))

    inv_freq = ROPE_THETA ** (-jnp.arange(0, QK_ROPE, 2, dtype=F32) / QK_ROPE)
    tabs = _rope_tables(positions[0], inv_freq)

    xs = x[0]
    saved = []
    for l in range(depth):
        xs, res = _layer_fwd(xs, p[l, 0], layers[l], tabs, dm)
        saved.append(res)
    dx, g_final_norm, loss_part = _loss_head(xs, final_norm, loss_target[0])
    grads = [None] * depth
    for l in reversed(range(depth)):
        dx, grads[l] = _layer_bwd(dx, saved[l], layers[l], tabs, dm)

    def stack(name):
        return jnp.stack([grads[l][name] for l in range(depth)])

    def shard_of(l, n, k):
        g = grads[l]
        if n == "w_in":
            lo, hi = k * W[n].shape[2], (k + 1) * W[n].shape[2]
            parts = ([g["w_sm"][:, lo:min(hi, n_small)]] if lo < n_small else []) + (
                [g["w_big"][:, max(lo, n_small) - n_small:hi - n_small]] if hi > n_small else [])
            return jnp.concatenate(parts, axis=1)
        if n == "w_q_b":
            return g["w_q"].reshape(ql, H, HEAD_PAD)[:, k * hpc:(k + 1) * hpc, :QK_NOPE + QK_ROPE].reshape(ql, -1)
        mine = {"w_kv_b": "w_kv"}.get(n, n)
        if n in COL_SHARDED:
            return g[mine][:, k * W[n].shape[2]:(k + 1) * W[n].shape[2]]
        return g[mine][k * W[n].shape[1]:(k + 1) * W[n].shape[1], :]

    R = gathered.shape[1]
    pieces = []
    for k in range(N_CHIPS):
        slab = [shard_of(l, n, k).reshape(-1) for n in SHARDED for l in range(depth)]
        fill = R * PACK_C - sum(a.size for a in slab)
        pieces += slab + [jnp.zeros((fill,), BF16)]
    gpack = jnp.concatenate(pieces).reshape(N_CHIPS, 2, R // 2, PACK_C)
    core_sum = _sum_core_halves(gpack, _swap_cores_half(gpack), core)
    chip_sum = _sum_slabs(_alltoall_chips(core_sum), "grad_sum_chips")
    gshard = _unpack(_join_core_halves(chip_sum).reshape(R, PACK_C), shard_shapes)
    G = dict(zip(SHARDED, gshard))

    rep_shapes = [W[n].shape for n in REPLICATED] + [(depth, CONV_K, D), (LANES,)]
    rep = [stack(n).reshape(W[n].shape) for n in REPLICATED if n != "final_norm"]
    rep += [g_final_norm.reshape(D), stack("conv_w"), loss_part.reshape(LANES)]
    rep_sum = _unpack(_sum_slabs(_allgather_all(_pack(rep, LANES, SMALL_ROWS), "small_grads_allgather"), "small_grads_sum"), rep_shapes)
    for n, gv in zip(REPLICATED, rep_sum):
        G[n] = gv
    cshard = D // N_CHIPS
    G["conv_w"] = lax.dynamic_slice_in_dim(rep_sum[-2], chip * cshard, cshard, axis=2)
    loss = rep_sum[-1][0]

    small = REPLICATED + ("conv_w",)
    small_shapes = [W[n].shape for n in small]
    pk = lambda src: _pack([src[n] for n in small], LANES, SMALL_ROWS)
    upd = _adamw(pk(W), pk(G), pk(M), pk(V), "adamw_small")
    delta, new_m, new_v = ({n: a for n, a in zip(small, _unpack(u, small_shapes))} for u in upd)
    for n in SHARDED:
        shp = W[n].shape
        two_d = lambda a: a.reshape(-1, shp[-1])
        d_, m_, v_ = _adamw(two_d(W[n]), two_d(G[n]), two_d(M[n]), two_d(V[n]), "adamw_" + n)
        delta[n], new_m[n], new_v[n] = d_.reshape(shp), m_.reshape(shp), v_.reshape(shp)

    return (loss, dx.reshape(x.shape), *[G[n] for n in WEIGHTS], *[delta[n] for n in WEIGHTS],
            *[new_m[n] for n in WEIGHTS], *[new_v[n] for n in WEIGHTS])
```

```python
import jax
import jax.numpy as jnp
from jax import lax
from jax.experimental import pallas as pl
from jax.experimental.pallas import tpu as pltpu

F32 = jnp.float32
BF16 = jnp.bfloat16
MESH = pl.DeviceIdType.MESH

CHUNK = 64
QK_NOPE = 128
QK_ROPE = 64
V_HEAD = 128
ROPE_THETA = 10000.0
CONV_K = 4
LRU_C = 8.0
LRU_BLOCK_DIM = 128
EPS = 1e-6
ADAM_LR = 0.001
ADAM_B1 = 0.9
ADAM_B2 = 0.999
ADAM_EPS = 1e-08
ADAM_WD = 0.01
ADAM_STEP = 10

N_CHIPS = 4
N_DEV = 8
LANES = 128
HEAD_PAD = 256
VMEM_LIMIT = 48 * 1024 * 1024

MM_TILE_BYTES = 8 * 1024 * 1024
ATT_T = 512
ATT_SUB = 512
ROW_T = 256
LRU_TT = 512
LRU_CB = 512
PACK_ROWS = 1024
SMALL_ROWS = 512

NT_DIMS = (((1,), (1,)), ((), ()))


def _cp(sem):
    return pltpu.CompilerParams(dimension_semantics=sem, vmem_limit_bytes=VMEM_LIMIT)


def _pick(n, pref, align=LANES):
    if n <= pref:
        return n
    t = pref - pref % align
    while t >= align:
        if n % t == 0:
            return t
        t -= align
    return n


def _sig(x):
    return 1.0 / (1.0 + jnp.exp(-x))


def _mm(a, b, mode, name, out_dtype=F32, add=None, tm=512, tn=1024, tk=2048):
    if mode == "nn":
        (M, K), (_, N) = a.shape, b.shape
    elif mode == "nt":
        (M, K), (N, _) = a.shape, b.shape
    else:
        (K, M), (_, N) = a.shape, b.shape
    tm, tn = _pick(M, tm), _pick(N, tn)
    while tk > 512 and tk * (tm * a.dtype.itemsize + tn * b.dtype.itemsize) > MM_TILE_BYTES:
        tk //= 2
    tk = _pick(K, tk)
    nm, nn, nk = M // tm, N // tn, K // tk
    i_outer = nm * b.size * b.dtype.itemsize <= nn * a.size * a.dtype.itemsize

    def ij(g0, g1):
        return (g0, g1) if i_outer else (g1, g0)

    def amap(g0, g1, k):
        i, _ = ij(g0, g1)
        return (k, i) if mode == "tn" else (i, k)

    def bmap(g0, g1, k):
        _, j = ij(g0, g1)
        return (j, k) if mode == "nt" else (k, j)

    def omap(g0, g1, k):
        return ij(g0, g1)

    ablk = (tk, tm) if mode == "tn" else (tm, tk)
    bblk = (tn, tk) if mode == "nt" else (tk, tn)

    def body(*refs):
        if add is None:
            a_ref, b_ref, o_ref = refs[:3]
            add_ref = None
        else:
            a_ref, b_ref, add_ref, o_ref = refs[:4]
        x = a_ref[...].astype(BF16)
        y = b_ref[...].astype(BF16)
        if mode == "nn":
            p = jnp.dot(x, y, preferred_element_type=F32)
        elif mode == "nt":
            p = lax.dot_general(x, y, NT_DIMS, preferred_element_type=F32)
        else:
            p = jnp.dot(x.T, y, preferred_element_type=F32)
        if nk == 1:
            if add_ref is not None:
                p = p + add_ref[...]
            o_ref[...] = p.astype(out_dtype)
        else:
            acc = refs[-1]
            k = pl.program_id(2)

            @pl.when(k == 0)
            def _():
                acc[...] = p if add_ref is None else p + add_ref[...]

            @pl.when(k > 0)
            def _():
                acc[...] += p

            @pl.when(k == nk - 1)
            def _():
                o_ref[...] = acc[...].astype(out_dtype)

    in_specs = [pl.BlockSpec(ablk, amap), pl.BlockSpec(bblk, bmap)]
    args = [a, b]
    if add is not None:
        in_specs.append(pl.BlockSpec((tm, tn), omap))
        args.append(add)
    grid = (nm, nn, nk) if i_outer else (nn, nm, nk)
    return pl.pallas_call(
        body, grid=grid, in_specs=in_specs, out_specs=pl.BlockSpec((tm, tn), omap),
        out_shape=jax.ShapeDtypeStruct((M, N), out_dtype),
        scratch_shapes=[pltpu.VMEM((tm, tn), F32)] if nk > 1 else [],
        compiler_params=_cp(("parallel", "parallel", "arbitrary")), name=name,
    )(*args)


def _rows(cols, i=0):
    def make(tm):
        return pl.BlockSpec((tm, cols), lambda r: (r, i))
    return make


def _par(cols):
    return pl.BlockSpec((1, cols), lambda r: (0, 0))


def _rms_fwd(x, g, name):
    S, D = x.shape
    tm = _pick(S, ROW_T, 8)

    def body(x_ref, g_ref, o_ref):
        xv = x_ref[...]
        rs = lax.rsqrt(jnp.mean(xv * xv, axis=-1, keepdims=True) + EPS)
        o_ref[...] = (xv * rs * g_ref[...]).astype(BF16)

    return pl.pallas_call(
        body, grid=(S // tm,), in_specs=[_rows(D)(tm), _par(D)], out_specs=_rows(D)(tm),
        out_shape=jax.ShapeDtypeStruct((S, D), BF16), compiler_params=_cp(("parallel",)), name=name,
    )(x, g.reshape(1, D))


def _rms_bwd_math(xv, g, dy):
    rs = lax.rsqrt(jnp.mean(xv * xv, axis=-1, keepdims=True) + EPS)
    xh = xv * rs
    dg = jnp.sum(dy * xh, axis=0, keepdims=True)
    dyg = dy * g
    dx = rs * (dyg - xh * jnp.mean(dyg * xh, axis=-1, keepdims=True))
    return dx, dg


def _accum(ref, val, first):
    @pl.when(first)
    def _():
        ref[...] = val

    @pl.when(jnp.logical_not(first))
    def _():
        ref[...] += val


def _rms_bwd(x, g, dy, dres, name):
    S, D = x.shape
    tm = _pick(S, ROW_T, 8)

    def body(x_ref, g_ref, dy_ref, dres_ref, dx_ref, dg_ref):
        dx, dg = _rms_bwd_math(x_ref[...], g_ref[...], dy_ref[...])
        dx_ref[...] = dres_ref[...] + dx
        _accum(dg_ref, dg, pl.program_id(0) == 0)

    return pl.pallas_call(
        body, grid=(S // tm,), in_specs=[_rows(D)(tm), _par(D), _rows(D)(tm), _rows(D)(tm)],
        out_specs=[_rows(D)(tm), _par(D)],
        out_shape=[jax.ShapeDtypeStruct((S, D), F32), jax.ShapeDtypeStruct((1, D), F32)],
        compiler_params=_cp(("arbitrary",)), name=name,
    )(x, g.reshape(1, D), dy, dres)


def _rope_tables(pos, inv_freq):
    S = pos.shape[0]
    tm = _pick(S, 512, 8)
    half = QK_ROPE // 2
    invf = jnp.concatenate([inv_freq, inv_freq, jnp.zeros((LANES - QK_ROPE,), F32)]).reshape(1, LANES)

    def body(pos_ref, f_ref, c_ref, sa_ref, sb_ref):
        ang = pos_ref[...].astype(F32) * f_ref[...]
        lane = lax.broadcasted_iota(jnp.int32, ang.shape, 1)
        c, s = jnp.cos(ang), jnp.sin(ang)
        c_ref[...] = jnp.where(lane < QK_ROPE, c, 0.0)
        sa_ref[...] = jnp.where(lane < half, -s, 0.0)
        sb_ref[...] = jnp.where((lane >= half) & (lane < QK_ROPE), s, 0.0)

    tab = jax.ShapeDtypeStruct((S, LANES), F32)
    return pl.pallas_call(
        body, grid=(S // tm,), in_specs=[pl.BlockSpec((tm, 1), lambda r: (r, 0)), _par(LANES)],
        out_specs=[_rows(LANES)(tm)] * 3, out_shape=[tab] * 3, compiler_params=_cp(("parallel",)), name="rope_tables",
    )(pos.reshape(S, 1), invf)


def _rope(x, c, sa, sb):
    return x * c + pltpu.roll(x, LANES - QK_ROPE // 2, 1) * sa + pltpu.roll(x, QK_ROPE // 2, 1) * sb


def _rope_t(d, c, sa, sb):
    return d * c + pltpu.roll(d * sa, QK_ROPE // 2, 1) + pltpu.roll(d * sb, LANES - QK_ROPE // 2, 1)


def _latent_fwd(zsm, gq, gkv, tabs, ql, kvl):
    S = zsm.shape[0]
    tm = _pick(S, ROW_T, 8)
    kr_blk = (ql + kvl) // LANES

    def body(q_ref, kv_ref, kr_ref, gq_ref, gkv_ref, c_ref, sa_ref, sb_ref, qn_ref, kvn_ref, kro_ref):
        for src, g_ref, dst in ((q_ref, gq_ref, qn_ref), (kv_ref, gkv_ref, kvn_ref)):
            v = src[...]
            rs = lax.rsqrt(jnp.mean(v * v, axis=-1, keepdims=True) + EPS)
            dst[...] = (v * rs * g_ref[...]).astype(BF16)
        kro_ref[...] = _rope(kr_ref[...], c_ref[...], sa_ref[...], sb_ref[...]).astype(BF16)

    return pl.pallas_call(
        body, grid=(S // tm,),
        in_specs=[_rows(ql, 0)(tm), _rows(kvl, 1)(tm), _rows(LANES, kr_blk)(tm), _par(ql), _par(kvl)] + [_rows(LANES)(tm)] * 3,
        out_specs=[_rows(ql)(tm), _rows(kvl)(tm), _rows(LANES)(tm)],
        out_shape=[jax.ShapeDtypeStruct((S, ql), BF16), jax.ShapeDtypeStruct((S, kvl), BF16), jax.ShapeDtypeStruct((S, LANES), BF16)],
        compiler_params=_cp(("parallel",)), name="latent_fwd",
    )(zsm, zsm, zsm, gq.reshape(1, ql), gkv.reshape(1, kvl), *tabs)


def _latent_bwd(zsm, gq, gkv, tabs, d_qn, d_kvn, dkr, ql, kvl):
    S, W = zsm.shape
    H = dkr.shape[0]
    tm = _pick(S, ROW_T, 8)
    kr_blk = (ql + kvl) // LANES

    def body(q_ref, kv_ref, gq_ref, gkv_ref, c_ref, sa_ref, sb_ref, dqn_ref, dkvn_ref, dkr_ref, dz_ref, dgq_ref, dgkv_ref):
        first = pl.program_id(0) == 0
        dq, dgq = _rms_bwd_math(q_ref[...], gq_ref[...], dqn_ref[...])
        dkv, dgkv = _rms_bwd_math(kv_ref[...], gkv_ref[...], dkvn_ref[...])
        dk = dkr_ref[0]
        for h in range(1, H):
            dk = dk + dkr_ref[h]
        dz_ref[:, 0:ql] = dq.astype(BF16)
        dz_ref[:, ql:ql + kvl] = dkv.astype(BF16)
        dz_ref[:, ql + kvl:] = _rope_t(dk, c_ref[...], sa_ref[...], sb_ref[...]).astype(BF16)
        _accum(dgq_ref, dgq, first)
        _accum(dgkv_ref, dgkv, first)

    return pl.pallas_call(
        body, grid=(S // tm,),
        in_specs=[_rows(ql, 0)(tm), _rows(kvl, 1)(tm), _par(ql), _par(kvl)] + [_rows(LANES)(tm)] * 3
        + [_rows(ql)(tm), _rows(kvl)(tm), pl.BlockSpec((H, tm, LANES), lambda r: (0, r, 0))],
        out_specs=[_rows(W)(tm), _par(ql), _par(kvl)],
        out_shape=[jax.ShapeDtypeStruct((S, W), BF16), jax.ShapeDtypeStruct((1, ql), F32), jax.ShapeDtypeStruct((1, kvl), F32)],
        compiler_params=_cp(("arbitrary",)), name="latent_bwd",
    )(zsm, zsm, gq.reshape(1, ql), gkv.reshape(1, kvl), *tabs, d_qn, d_kvn, dkr)


def _q_rope(q, tabs, transpose, name, gain=1.0):
    S, W = q.shape
    H = W // HEAD_PAD
    tm = _pick(S, ROW_T, 8)
    fn = _rope_t if transpose else _rope

    def body(q_ref, c_ref, sa_ref, sb_ref, o_ref):
        c, sa, sb = c_ref[...], sa_ref[...], sb_ref[...]
        if gain != 1.0:
            c, sa, sb = c * gain, sa * gain, sb * gain
        for h in range(H):
            lo = h * HEAD_PAD
            nope = q_ref[:, lo:lo + QK_NOPE]
            o_ref[:, lo:lo + QK_NOPE] = (nope if gain == 1.0 else nope * gain).astype(BF16)
            o_ref[:, lo + QK_NOPE:lo + HEAD_PAD] = fn(q_ref[:, lo + QK_NOPE:lo + HEAD_PAD], c, sa, sb).astype(BF16)

    return pl.pallas_call(
        body, grid=(S // tm,), in_specs=[_rows(W)(tm)] + [_rows(LANES)(tm)] * 3, out_specs=_rows(W)(tm),
        out_shape=jax.ShapeDtypeStruct((S, W), BF16), compiler_params=_cp(("parallel",)), name=name,
    )(q, *tabs)


def _merge_fwd(zbig, y_mla, y_lru, D):
    S = y_mla.shape[0]
    tm = _pick(S, ROW_T, 8)

    def body(mm_ref, ml_ref, ym_ref, yl_ref, o_ref):
        o_ref[...] = (_sig(mm_ref[...]) * ym_ref[...] + _sig(ml_ref[...]) * yl_ref[...]).astype(BF16)

    return pl.pallas_call(
        body, grid=(S // tm,), in_specs=[_rows(D, 3)(tm), _rows(D, 4)(tm), _rows(D)(tm), _rows(D)(tm)], out_specs=_rows(D)(tm),
        out_shape=jax.ShapeDtypeStruct((S, D), BF16), compiler_params=_cp(("parallel",)), name="merge_fwd",
    )(zbig, zbig, y_mla, y_lru)


def _merge_bwd(zbig, y_mla, y_lru, d_merged, D):
    S = y_mla.shape[0]
    tm = _pick(S, ROW_T, 8)

    def body(mm_ref, ml_ref, ym_ref, yl_ref, d_ref, dym_ref, dyl_ref, dmm_ref, dml_ref):
        d = d_ref[...]
        sm, sl = _sig(mm_ref[...]), _sig(ml_ref[...])
        dym_ref[...] = (d * sm).astype(BF16)
        dyl_ref[...] = (d * sl).astype(BF16)
        dmm_ref[...] = (d * ym_ref[...] * sm * (1.0 - sm)).astype(BF16)
        dml_ref[...] = (d * yl_ref[...] * sl * (1.0 - sl)).astype(BF16)

    o = jax.ShapeDtypeStruct((S, D), BF16)
    return pl.pallas_call(
        body, grid=(S // tm,), in_specs=[_rows(D, 3)(tm), _rows(D, 4)(tm)] + [_rows(D)(tm)] * 3, out_specs=[_rows(D)(tm)] * 4,
        out_shape=[o] * 4, compiler_params=_cp(("parallel",)), name="merge_bwd",
    )(zbig, zbig, y_mla, y_lru, d_merged)


def _ple_fwd(x1, pe, pg):
    S, D = x1.shape
    tm = _pick(S, ROW_T, 8)

    def body(x_ref, pe_ref, pg_ref, o_ref):
        o_ref[...] = x_ref[...] + pe_ref[...] * _sig(pg_ref[...])

    return pl.pallas_call(
        body, grid=(S // tm,), in_specs=[_rows(D)(tm)] * 3, out_specs=_rows(D)(tm),
        out_shape=jax.ShapeDtypeStruct((S, D), F32), compiler_params=_cp(("parallel",)), name="ple_fwd",
    )(x1, pe, pg)


def _ple_bwd(dx2, pe, pg):
    S, D = dx2.shape
    tm = _pick(S, ROW_T, 8)

    def body(d_ref, pe_ref, pg_ref, dpe_ref, dpg_ref):
        d = d_ref[...]
        s = _sig(pg_ref[...])
        dpe_ref[...] = (d * s).astype(BF16)
        dpg_ref[...] = (d * pe_ref[...] * s * (1.0 - s)).astype(BF16)

    o = jax.ShapeDtypeStruct((S, D), BF16)
    return pl.pallas_call(
        body, grid=(S // tm,), in_specs=[_rows(D)(tm)] * 3, out_specs=[_rows(D)(tm)] * 2, out_shape=[o] * 2,
        compiler_params=_cp(("parallel",)), name="ple_bwd",
    )(dx2, pe, pg)


def _loss_head(x, g, target):
    S, D = x.shape
    tm = _pick(S, ROW_T, 8)

    def body(x_ref, g_ref, t_ref, dx_ref, dg_ref, loss_ref):
        first = pl.program_id(0) == 0
        xv, gv = x_ref[...], g_ref[...]
        rs = lax.rsqrt(jnp.mean(xv * xv, axis=-1, keepdims=True) + EPS)
        e = xv * rs * gv - t_ref[...]
        part = 0.5 * jnp.sum(jnp.mean(e * e, axis=-1, keepdims=True), axis=0, keepdims=True)
        dx, dg = _rms_bwd_math(xv, gv, e * (1.0 / D))
        dx_ref[...] = dx
        _accum(dg_ref, dg, first)
        _accum(loss_ref, jnp.broadcast_to(part, (1, LANES)), first)

    return pl.pallas_call(
        body, grid=(S // tm,), in_specs=[_rows(D)(tm), _par(D), _rows(D)(tm)], out_specs=[_rows(D)(tm), _par(D), _par(LANES)],
        out_shape=[jax.ShapeDtypeStruct((S, D), F32), jax.ShapeDtypeStruct((1, D), F32), jax.ShapeDtypeStruct((1, LANES), F32)],
        compiler_params=_cp(("arbitrary",)), name="loss_head",
    )(x, g.reshape(1, D), target)


def _attn_gate_bwd(d_a, o, zbig, H):
    S, W = o.shape
    tm = _pick(S, ROW_T, 8)

    def body(da_ref, o_ref, g_ref, do_ref, dg_ref, dl_ref):
        da, ov, g = da_ref[...], o_ref[...], g_ref[...]
        s = _sig(g)
        do = da * g * s
        do_ref[...] = do.astype(BF16)
        dg_ref[...] = (da * ov * s * (1.0 + g * (1.0 - s))).astype(BF16)
        prod = do * ov
        for h in range(H):
            r = jnp.sum(prod[:, h * V_HEAD:(h + 1) * V_HEAD], axis=-1, keepdims=True)
            dl_ref[h] = jnp.broadcast_to(r, (tm, LANES))

    return pl.pallas_call(
        body, grid=(S // tm,), in_specs=[_rows(W)(tm), _rows(W)(tm), _rows(W, 0)(tm)],
        out_specs=[_rows(W)(tm), _rows(W)(tm), pl.BlockSpec((H, tm, LANES), lambda r: (0, r, 0))],
        out_shape=[jax.ShapeDtypeStruct((S, W), BF16), jax.ShapeDtypeStruct((S, W), BF16), jax.ShapeDtypeStruct((H, S, LANES), F32)],
        compiler_params=_cp(("parallel",)), name="attn_gate_bwd",
    )(d_a, o, zbig)


def _chunk_mask(ts, t, lo, q_rows):
    r = (lax.broadcasted_iota(jnp.int32, (ts, t), 0) + lo) // CHUNK
    c = lax.broadcasted_iota(jnp.int32, (ts, t), 1) // CHUNK
    return (c <= r) if q_rows else (r <= c)


def _attn_tiles(S):
    t = _pick(S, ATT_T)
    ts = _pick(t, ATT_SUB, 8)
    return t, ts, [r * ts for r in range(t // ts)]


QK_SCALE = 1.0 / (QK_NOPE + QK_ROPE) ** 0.5
LOG2E = 1.4426950408889634


def _attn_fwd(q, kv, kr, zbig, H):
    S = q.shape[0]
    t, ts, subs = _attn_tiles(S)
    nq = S // t

    def body(q_ref, kn_ref, v_ref, kr_ref, g_ref, o_ref, a_ref, lse_ref, m_scr, l_scr, acc_scr, s0_scr, s1_scr, p0_scr, p1_scr):
        i = pl.program_id(1)
        m_scr[...] = jnp.full((t, LANES), -1e30, F32)
        l_scr[...] = jnp.zeros((t, LANES), F32)
        acc_scr[...] = jnp.zeros((t, V_HEAD), F32)

        s_buf, p_buf = (s0_scr, s1_scr), (p0_scr, p1_scr)

        def logits(j, slot):
            ks = pl.multiple_of(j * t, t)
            k = jnp.concatenate([kn_ref[pl.ds(ks, t), :], kr_ref[pl.ds(ks, t), :]], axis=1)
            s_buf[slot][...] = lax.dot_general(q_ref[...], k, NT_DIMS, preferred_element_type=F32)

        def values(j, slot):
            ks = pl.multiple_of(jnp.maximum(j, 0) * t, t)
            acc_scr[...] += jnp.dot(p_buf[slot][...], v_ref[pl.ds(ks, t), :], preferred_element_type=F32)

        def softmax(slot, masked):
            s = s_buf[slot][...]
            if masked:
                s = jnp.where(_chunk_mask(t, t, 0, True), s, -1e30)
            m_prev = m_scr[...]
            m_next = jnp.maximum(m_prev, jnp.max(s, axis=1, keepdims=True))
            p = jnp.exp2(s - jnp.tile(m_next, (1, t // LANES)))
            alpha = jnp.exp2(m_prev - m_next)
            l_scr[...] = alpha * l_scr[...] + jnp.sum(p, axis=1, keepdims=True)
            m_scr[...] = m_next
            acc_scr[...] = acc_scr[...] * alpha
            p_buf[slot][...] = p.astype(BF16)

        def step(j, slot, masked=False, more=True):
            if more:
                logits(j + 1, 1 - slot)
            values(j - 1, 1 - slot)
            softmax(slot, masked)

        logits(0, 0)
        p1_scr[...] = jnp.zeros((t, t), BF16)

        def loop(a, carry):
            step(2 * a, 0)
            step(2 * a + 1, 1)
            return carry

        lax.fori_loop(0, i // 2, loop, 0)

        @pl.when(i % 2 == 0)
        def _():
            step(i, 0, masked=True, more=False)
            values(i, 0)

        @pl.when(i % 2 == 1)
        def _():
            step(i - 1, 0)
            step(i, 1, masked=True, more=False)
            values(i, 1)

        l = l_scr[...]
        ov = acc_scr[...] / l
        g = g_ref[...]
        o_ref[...] = ov
        a_ref[...] = (ov * g * _sig(g)).astype(BF16)
        lse_ref[0] = m_scr[...] + jnp.log(l) * LOG2E

    head_col = lambda w, off: pl.BlockSpec((S, w), lambda h, i: (0, 2 * h + off))
    return pl.pallas_call(
        body, grid=(H, nq),
        in_specs=[pl.BlockSpec((t, HEAD_PAD), lambda h, i: (i, h)), head_col(QK_NOPE, 0), head_col(V_HEAD, 1),
                  pl.BlockSpec((S, LANES), lambda h, i: (0, 0)), pl.BlockSpec((t, V_HEAD), lambda h, i: (i, h))],
        out_specs=[pl.BlockSpec((t, V_HEAD), lambda h, i: (i, h)), pl.BlockSpec((t, V_HEAD), lambda h, i: (i, h)),
                   pl.BlockSpec((1, t, LANES), lambda h, i: (h, i, 0))],
        out_shape=[jax.ShapeDtypeStruct((S, H * V_HEAD), F32), jax.ShapeDtypeStruct((S, H * V_HEAD), BF16),
                   jax.ShapeDtypeStruct((H, S, LANES), F32)],
        scratch_shapes=[pltpu.VMEM((t, LANES), F32), pltpu.VMEM((t, LANES), F32), pltpu.VMEM((t, V_HEAD), F32),
                        pltpu.VMEM((t, t), F32), pltpu.VMEM((t, t), F32), pltpu.VMEM((t, t), BF16), pltpu.VMEM((t, t), BF16)],
        compiler_params=_cp(("parallel", "arbitrary")), name="attn_fwd",
    )(q, kv, kv, kr, zbig)


TN_DIMS = (((0,), (0,)), ((), ()))


def _attn_bwd(q, kv, kr, do, lse_row, delta_row, H):
    S = q.shape[0]
    t, _, _ = _attn_tiles(S)
    nk = S // t

    def body(kn_ref, v_ref, kr_ref, q_ref, do_ref, lse_ref, dl_ref, dq_ref, dkv_ref, dkr_ref, dk_scr, dv_scr,
             st0_scr, st1_scr, dp0_scr, dp1_scr):
        j = pl.program_id(1)
        dk_scr[...] = jnp.zeros((t, HEAD_PAD), F32)
        dv_scr[...] = jnp.zeros((t, V_HEAD), F32)

        @pl.when(j == 0)
        def _():
            dq_ref[...] = jnp.zeros((S, HEAD_PAD), F32)

        st_buf, dp_buf = (st0_scr, st1_scr), (dp0_scr, dp1_scr)

        def keys():
            return jnp.concatenate([kn_ref[...], kr_ref[...]], axis=1)

        def rows(i):
            return pl.ds(pl.multiple_of(jnp.minimum(i, nk - 1) * t, t), t)

        def scores(i, slot):
            st_buf[slot][...] = lax.dot_general(keys(), q_ref[rows(i), :], NT_DIMS, preferred_element_type=F32)
            dp_buf[slot][...] = lax.dot_general(v_ref[...], do_ref[rows(i), :], NT_DIMS, preferred_element_type=F32)

        def step(s, slot, masked=False):
            i = j + s
            scores(i + 1, 1 - slot)
            pt = jnp.exp2(st_buf[slot][...] - lse_ref[0, :, rows(i)])
            if masked:
                pt = jnp.where(_chunk_mask(t, t, 0, False), pt, 0.0)
            dst = (pt * (dp_buf[slot][...] - dl_ref[0, :, rows(i)])).astype(BF16)
            dv_scr[...] += jnp.dot(pt.astype(BF16), do_ref[rows(i), :], preferred_element_type=F32)
            dk_scr[...] += jnp.dot(dst, q_ref[rows(i), :], preferred_element_type=F32)
            dq_ref[rows(i), :] += lax.dot_general(dst, keys(), TN_DIMS, preferred_element_type=F32)

        n = nk - j
        scores(j, 0)
        step(0, 0, masked=True)

        def loop(a, carry):
            step(2 * a + 1, 1)
            step(2 * a + 2, 0)
            return carry

        lax.fori_loop(0, (n - 1) // 2, loop, 0)

        @pl.when((n - 1) % 2 == 1)
        def _():
            step(n - 1, 1)

        dk = dk_scr[...] * (1.0 / LOG2E)
        dkv_ref[:, 0:QK_NOPE] = dk[:, 0:QK_NOPE].astype(BF16)
        dkv_ref[:, QK_NOPE:] = dv_scr[...].astype(BF16)
        dkr_ref[0] = dk[:, QK_NOPE:]

    tile_col = lambda w, off: pl.BlockSpec((t, w), lambda h, j: (j, 2 * h + off))
    row = pl.BlockSpec((1, 1, S), lambda h, j: (h, 0, 0))
    return pl.pallas_call(
        body, grid=(H, nk),
        in_specs=[tile_col(QK_NOPE, 0), tile_col(V_HEAD, 1), pl.BlockSpec((t, LANES), lambda h, j: (j, 0)),
                  pl.BlockSpec((S, HEAD_PAD), lambda h, j: (0, h)), pl.BlockSpec((S, V_HEAD), lambda h, j: (0, h)), row, row],
        out_specs=[pl.BlockSpec((S, HEAD_PAD), lambda h, j: (0, h)), pl.BlockSpec((t, HEAD_PAD), lambda h, j: (j, h)),
                   pl.BlockSpec((1, t, LANES), lambda h, j: (h, j, 0))],
        out_shape=[jax.ShapeDtypeStruct((S, H * HEAD_PAD), F32), jax.ShapeDtypeStruct((S, H * HEAD_PAD), BF16),
                   jax.ShapeDtypeStruct((H, S, LANES), F32)],
        scratch_shapes=[pltpu.VMEM((t, HEAD_PAD), F32), pltpu.VMEM((t, V_HEAD), F32)] + [pltpu.VMEM((t, t), F32)] * 4,
        compiler_params=_cp(("arbitrary", "arbitrary")), name="attn_bwd",
    )(kv, kv, kr, q, do, lse_row, delta_row)


def _shift_down(x, prev8, s):
    rx = pltpu.roll(x, s, 0)
    rp = pltpu.roll(prev8, s, 0)
    rows = lax.broadcasted_iota(jnp.int32, rp.shape, 0)
    return jnp.concatenate([jnp.where(rows < s, rp, rx[:8]), rx[8:]], axis=0)


def _shift_up(x, next8, s):
    n = x.shape[0]
    rx = pltpu.roll(x, n - s, 0)
    rn = pltpu.roll(next8, 8 - s, 0)
    rows = lax.broadcasted_iota(jnp.int32, rn.shape, 0)
    return jnp.concatenate([rx[:n - 8], jnp.where(rows >= 8 - s, rn, rx[n - 8:])], axis=0)


def _scan_rows(a, b, up):
    n = a.shape[0]
    rows = lax.broadcasted_iota(jnp.int32, a.shape, 0)
    d = 1
    while d < n:
        keep = (rows < n - d) if up else (rows >= d)
        sh = n - d if up else d
        a_s = jnp.where(keep, pltpu.roll(a, sh, 0), 1.0)
        b_s = jnp.where(keep, pltpu.roll(b, sh, 0), 0.0)
        b = a * b_s + b
        a = a * a_s
        d *= 2
    return a, b


def _log1p(e):
    u = 1.0 + e
    return jnp.where(u == 1.0, e, jnp.log(u) * (e / (u - 1.0)))


def _lru_pre(u, prev8, cw_ref, cb_ref, wr_ref, br_ref, wi_ref, bi_ref, lam_ref):
    us = [u, _shift_down(u, prev8, 1), _shift_down(u, prev8, 2), _shift_down(u, prev8, 3)]
    xc = cb_ref[...] + cw_ref[3:4, :] * us[0] + cw_ref[2:3, :] * us[1] + cw_ref[1:2, :] * us[2] + cw_ref[0:1, :] * us[3]
    x16 = xc.astype(BF16)
    nb = xc.shape[1] // LRU_BLOCK_DIM
    blk = lambda k: slice(k * LRU_BLOCK_DIM, (k + 1) * LRU_BLOCK_DIM)
    pr = jnp.concatenate([jnp.dot(x16[:, blk(k)], wr_ref[k].astype(BF16), preferred_element_type=F32) for k in range(nb)], axis=1)
    pi = jnp.concatenate([jnp.dot(x16[:, blk(k)], wi_ref[k].astype(BF16), preferred_element_type=F32) for k in range(nb)], axis=1)
    r = _sig(pr + br_ref[...])
    i = _sig(pi + bi_ref[...])
    nlam = -lam_ref[...]
    sp = jnp.maximum(nlam, 0.0) + _log1p(jnp.exp(-jnp.abs(nlam)))
    log_a = (-LRU_C * r) * sp
    a = jnp.exp(log_a)
    mult = jnp.sqrt(-jnp.tanh(log_a) * (a * a + 1.0))
    return us, xc, x16, r, i, sp, a, mult


def _lru_specs(D, cb, tt, nT, rev):
    nb = cb // LRU_BLOCK_DIM
    tmap = (lambda t: nT - 1 - t) if rev else (lambda t: t)
    ncb = D // cb

    def tile(piece):
        return pl.BlockSpec((tt, cb), lambda c, t: (tmap(t), piece * ncb + c))

    def halo(piece):
        return pl.BlockSpec((8, cb), lambda c, t: (jnp.maximum(tmap(t) * (tt // 8) - 1, 0), piece * ncb + c))

    par = lambda rows: pl.BlockSpec((rows, cb), lambda c, t: (0, c))
    wblk = pl.BlockSpec((nb, LRU_BLOCK_DIM, LRU_BLOCK_DIM), lambda c, t: (c, 0, 0))
    return tile, halo, par, wblk, tmap


def _lru_fwd(zbig, cw, cbias, wr, br, wi, bi, lam, D):
    S = zbig.shape[0]
    tt, cb = _pick(S, LRU_TT, 8), _pick(D, LRU_CB)
    nT = S // tt
    tile, halo, par, wblk, _ = _lru_specs(D, cb, tt, nT, False)

    def body(u_ref, up_ref, g_ref, cw_ref, cb_ref, wr_ref, br_ref, wi_ref, bi_ref, lam_ref, h_ref, al_ref, carry):
        t = pl.program_id(1)
        prev8 = jnp.where(t > 0, up_ref[...], 0.0)
        _, xc, _, _, i, _, a, mult = _lru_pre(u_ref[...], prev8, cw_ref, cb_ref, wr_ref, br_ref, wi_ref, bi_ref, lam_ref)
        pa, hb = _scan_rows(a, mult * (i * xc), False)
        h0 = jnp.where(t > 0, carry[7:8, :], 0.0)
        h = hb + pa * h0
        h_ref[...] = h
        carry[...] = h[tt - 8:, :]
        g = g_ref[...]
        al_ref[...] = (h * g * _sig(g)).astype(BF16)

    return pl.pallas_call(
        body, grid=(D // cb, nT),
        in_specs=[tile(1), halo(1), tile(2), par(CONV_K), par(1), wblk, par(1), wblk, par(1), par(1)],
        out_specs=[tile(0), tile(0)],
        out_shape=[jax.ShapeDtypeStruct((S, D), F32), jax.ShapeDtypeStruct((S, D), BF16)],
        scratch_shapes=[pltpu.VMEM((8, cb), F32)],
        compiler_params=_cp(("parallel", "arbitrary")), name="lru_fwd",
    )(zbig, zbig, zbig, cw, cbias.reshape(1, D), wr, br.reshape(1, D), wi, bi.reshape(1, D), lam.reshape(1, D))


def _lru_bwd(zbig, h, d_al, cw, cbias, wr, br, wi, bi, lam, D):
    S = zbig.shape[0]
    tt, cb = _pick(S, LRU_TT, 8), _pick(D, LRU_CB)
    nT = S // tt
    nb = cb // LRU_BLOCK_DIM
    tile, halo, par, wblk, tmap = _lru_specs(D, cb, tt, nT, True)

    def body(u_ref, up_ref, g_ref, h_ref, hp_ref, dal_ref, cw_ref, cb_ref, wr_ref, br_ref, wi_ref, bi_ref, lam_ref,
             du_ref, dg_ref, dcw_ref, dcb_ref, dwr_ref, dbr_ref, dwi_ref, dbi_ref, dlam_ref, g_car, a_car, x_car):
        step = pl.program_id(1)
        first = step == 0
        t = nT - 1 - step
        prev8 = jnp.where(t > 0, up_ref[...], 0.0)
        us, xc, x16, r, i, sp, a, mult = _lru_pre(u_ref[...], prev8, cw_ref, cb_ref, wr_ref, br_ref, wi_ref, bi_ref, lam_ref)
        hv = h_ref[...]
        h_m1 = _shift_down(hv, jnp.where(t > 0, hp_ref[...], 0.0), 1)
        g, dal = g_ref[...], dal_ref[...]
        sg = _sig(g)
        dg_ref[...] = (dal * hv * sg * (1.0 + g * (1.0 - sg))).astype(BF16)
        dh = dal * g * sg
        coef = _shift_up(a, jnp.where(first, 0.0, a_car[...]), 1)
        pa, gb = _scan_rows(coef, dh, True)
        G = gb + pa * jnp.where(first, 0.0, g_car[0:1, :])
        g_car[...] = G[:8]
        a_car[...] = a[:8]
        da = G * h_m1
        ixc = i * xc
        dixc = G * mult
        dlog = da * a - (G * ixc) * (a * a) / mult
        dpr = dlog * (-LRU_C * sp) * r * (1.0 - r)
        dpi = dixc * xc * i * (1.0 - i)
        dxc = dixc * i
        dsp = jnp.sum(dlog * (-LRU_C) * r, axis=0, keepdims=True)
        dpr16, dpi16 = dpr.astype(BF16), dpi.astype(BF16)
        blk = lambda k: slice(k * LRU_BLOCK_DIM, (k + 1) * LRU_BLOCK_DIM)
        back = []
        for k in range(nb):
            xk = x16[:, blk(k)].T
            dwr_k = jnp.dot(xk, dpr16[:, blk(k)], preferred_element_type=F32)
            dwi_k = jnp.dot(xk, dpi16[:, blk(k)], preferred_element_type=F32)

            @pl.when(first)
            def _():
                dwr_ref[k] = dwr_k
                dwi_ref[k] = dwi_k

            @pl.when(jnp.logical_not(first))
            def _():
                dwr_ref[k] += dwr_k
                dwi_ref[k] += dwi_k

            back.append(lax.dot_general(dpr16[:, blk(k)], wr_ref[k].astype(BF16), NT_DIMS, preferred_element_type=F32)
                        + lax.dot_general(dpi16[:, blk(k)], wi_ref[k].astype(BF16), NT_DIMS, preferred_element_type=F32))
        dxc = dxc + jnp.concatenate(back, axis=1)
        _accum(dbr_ref, jnp.sum(dpr, axis=0, keepdims=True), first)
        _accum(dbi_ref, jnp.sum(dpi, axis=0, keepdims=True), first)
        _accum(dlam_ref, dsp * (-_sig(-lam_ref[...])), first)
        _accum(dcb_ref, jnp.sum(dxc, axis=0, keepdims=True), first)
        _accum(dcw_ref, jnp.concatenate([jnp.sum(dxc * us[3 - k], axis=0, keepdims=True) for k in range(CONV_K)], axis=0), first)
        nxt = jnp.where(first, 0.0, x_car[...])
        du = cw_ref[3:4, :] * dxc
        for s in range(1, CONV_K):
            du = du + cw_ref[3 - s:4 - s, :] * _shift_up(dxc, nxt, s)
        x_car[...] = dxc[:8]
        du_ref[...] = du.astype(BF16)

    act = jax.ShapeDtypeStruct((S, D), BF16)
    vec = jax.ShapeDtypeStruct((1, D), F32)
    wsh = jax.ShapeDtypeStruct(wr.shape, F32)
    rtile = pl.BlockSpec((tt, cb), lambda c, t: (tmap(t), c))
    rhalo = pl.BlockSpec((8, cb), lambda c, t: (jnp.maximum(tmap(t) * (tt // 8) - 1, 0), c))
    return pl.pallas_call(
        body, grid=(D // cb, nT),
        in_specs=[tile(1), halo(1), tile(2), rtile, rhalo, rtile, par(CONV_K), par(1), wblk, par(1), wblk, par(1), par(1)],
        out_specs=[rtile, rtile, par(CONV_K), par(1), wblk, par(1), wblk, par(1), par(1)],
        out_shape=[act, act, jax.ShapeDtypeStruct((CONV_K, D), F32), vec, wsh, vec, wsh, vec, vec],
        scratch_shapes=[pltpu.VMEM((8, cb), F32)] * 3,
        compiler_params=_cp(("parallel", "arbitrary")), name="lru_bwd",
    )(zbig, zbig, zbig, h, h, d_al, cw, cbias.reshape(1, D), wr, br.reshape(1, D), wi, bi.reshape(1, D), lam.reshape(1, D))


def _adamw(w, g, m, v, name):
    R, C = w.shape
    tm = _pick(R, max(8, ((1 << 18) // C) // 8 * 8), 8)
    c1 = 1.0 - ADAM_B1 ** ADAM_STEP
    c2 = 1.0 - ADAM_B2 ** ADAM_STEP

    def body(w_ref, g_ref, m_ref, v_ref, d_ref, mo_ref, vo_ref):
        gv = g_ref[...]
        mn = ADAM_B1 * m_ref[...] + (1.0 - ADAM_B1) * gv
        vn = ADAM_B2 * v_ref[...] + (1.0 - ADAM_B2) * (gv * gv)
        d_ref[...] = -ADAM_LR * ((mn / c1) / (jnp.sqrt(vn / c2) + ADAM_EPS) + ADAM_WD * w_ref[...])
        mo_ref[...] = mn
        vo_ref[...] = vn

    o = jax.ShapeDtypeStruct((R, C), F32)
    return pl.pallas_call(
        body, grid=(R // tm,), in_specs=[_rows(C)(tm)] * 4, out_specs=[_rows(C)(tm)] * 3, out_shape=[o] * 3,
        compiler_params=_cp(("parallel",)), name=name,
    )(w, g, m, v)


def _sum_slabs(x, name, out_dtype=F32):
    n, R, C = x.shape
    tm = _pick(R, max(8, ((1 << 18) // C) // 8 * 8), 16)

    def body(x_ref, o_ref):
        s = x_ref[0].astype(F32)
        for k in range(1, n):
            s = s + x_ref[k].astype(F32)
        o_ref[...] = s.astype(out_dtype)

    return pl.pallas_call(
        body, grid=(R // tm,), in_specs=[pl.BlockSpec((n, tm, C), lambda r: (0, r, 0))], out_specs=_rows(C)(tm),
        out_shape=jax.ShapeDtypeStruct((R, C), out_dtype), compiler_params=_cp(("parallel",)), name=name,
    )(x)


def _sum_core_halves(g, recv, core, tag):
    n, _, Rh, C = g.shape
    tm = _pick(Rh, max(16, ((1 << 18) // C) // 16 * 16), 16)

    def body(c_ref, g_ref, r_ref, o_ref):
        o_ref[0] = (g_ref[0, 0].astype(F32) + r_ref[0, 0].astype(F32)).astype(BF16)

    return pl.pallas_call(
        body,
        grid_spec=pltpu.PrefetchScalarGridSpec(
            num_scalar_prefetch=1, grid=(n, Rh // tm),
            in_specs=[pl.BlockSpec((1, 1, tm, C), lambda k, r, c_ref: (k, c_ref[0], r, 0)),
                      pl.BlockSpec((1, 1, tm, C), lambda k, r, c_ref: (k, 0, r, 0))],
            out_specs=pl.BlockSpec((1, tm, C), lambda k, r, c_ref: (k, r, 0)),
        ),
        out_shape=jax.ShapeDtypeStruct((n, Rh, C), BF16),
        compiler_params=_cp(("parallel", "parallel")), name="grad_sum_cores" + tag,
    )(core.reshape(1).astype(jnp.int32), g, recv)


ANY = pl.BlockSpec(memory_space=pl.ANY)


def _place():
    x, y, c = lax.axis_index("x"), lax.axis_index("y"), lax.axis_index("c")
    chips = [(1 - x, y), (x, 1 - y), (1 - x, 1 - y)]
    return x, y, c, chips


def _allgather_chips(shard, name):
    R, C = shard.shape
    Rh = R // 2

    def body(x_ref, out_ref, send_sems, recv_sems):
        x, y, c, chips = _place()
        me = 2 * x + y
        sibling = (x, y, 1 - c)

        def half(k, hc):
            return out_ref.at[k, pl.ds(hc * Rh, Rh), :]

        first = [pltpu.make_async_remote_copy(
            src_ref=x_ref.at[pl.ds(c * Rh, Rh), :], dst_ref=half(me, c), send_sem=send_sems.at[j], recv_sem=recv_sems.at[j],
            device_id=(*chip, c), device_id_type=MESH) for j, chip in enumerate(chips)]
        for cp in first:
            cp.start()

        def landed(j, chip, hc):
            k = 2 * chip[0] + chip[1]
            return pltpu.make_async_remote_copy(
                src_ref=half(k, hc), dst_ref=half(k, hc), send_sem=send_sems.at[j], recv_sem=recv_sems.at[j],
                device_id=sibling, device_id_type=MESH)

        passed = []
        for j, chip in enumerate(chips):
            landed(j, chip, c).wait_recv()
            cp = landed(3 + j, chip, c)
            cp.start()
            passed.append(cp)
        for j, chip in enumerate(chips):
            landed(3 + j, chip, 1 - c).wait_recv()
        for cp in first + passed:
            cp.wait_send()

    out = pl.pallas_call(
        body, in_specs=[ANY], out_specs=ANY, out_shape=jax.ShapeDtypeStruct((N_CHIPS, R, C), shard.dtype),
        scratch_shapes=[pltpu.SemaphoreType.DMA((6,)), pltpu.SemaphoreType.DMA((6,))],
        name=name,
    )(shard)
    return lax.dynamic_update_slice(out, shard[None], (2 * lax.axis_index("x") + lax.axis_index("y"), 0, 0))


def _allgather_all(blockv, name):
    R, C = blockv.shape

    def body(x_ref, out_ref, send_sems, recv_sems):
        x, y, c, chips = _place()
        sibling = (x, y, 1 - c)

        def slab(px, py, pc):
            return out_ref.at[4 * px + 2 * py + pc]

        def copy(k, block, to, src=None):
            return pltpu.make_async_remote_copy(
                src_ref=slab(*block) if src is None else src, dst_ref=slab(*block), send_sem=send_sems.at[k],
                recv_sem=recv_sems.at[k], device_id=to, device_id_type=MESH)

        first = [copy(0, (x, y, c), sibling, src=x_ref)]
        first += [copy(1 + j, (x, y, c), (*chip, c), src=x_ref) for j, chip in enumerate(chips)]
        for cp in first:
            cp.start()
        passed = [copy(4 + j, (*chip, c), sibling) for j, chip in enumerate(chips)]
        for j, chip in enumerate(chips):
            copy(1 + j, (*chip, c), (x, y, c)).wait_recv()
            passed[j].start()
        copy(0, (x, y, 1 - c), (x, y, c)).wait_recv()
        for j, chip in enumerate(chips):
            copy(4 + j, (*chip, 1 - c), (x, y, c)).wait_recv()
        for cp in first + passed:
            cp.wait_send()

    out = pl.pallas_call(
        body, in_specs=[ANY], out_specs=ANY, out_shape=jax.ShapeDtypeStruct((N_DEV, R, C), blockv.dtype),
        scratch_shapes=[pltpu.SemaphoreType.DMA((7,)), pltpu.SemaphoreType.DMA((7,))],
        name=name,
    )(blockv)
    me = 4 * lax.axis_index("x") + 2 * lax.axis_index("y") + lax.axis_index("c")
    return lax.dynamic_update_slice(out, blockv[None], (me, 0, 0))


def _swap_cores_half(g, tag):
    n, _, Rh, C = g.shape

    def body(g_ref, out_ref, send_sem, recv_sem):
        x, y, c, _ = _place()
        cp = pltpu.make_async_remote_copy(
            src_ref=g_ref.at[:, pl.ds(1 - c, 1)], dst_ref=out_ref, send_sem=send_sem, recv_sem=recv_sem,
            device_id=(x, y, 1 - c), device_id_type=MESH)
        cp.start()
        cp.wait()

    return pl.pallas_call(
        body, in_specs=[ANY], out_specs=ANY, out_shape=jax.ShapeDtypeStruct((n, 1, Rh, C), g.dtype),
        scratch_shapes=[pltpu.SemaphoreType.DMA, pltpu.SemaphoreType.DMA], name="grad_swap_cores" + tag,
    )(g)


def _alltoall_chips(s, tag):
    n, Rh, C = s.shape

    def body(s_ref, out_ref, send_sems, recv_sems):
        x, y, c, chips = _place()
        me = 2 * x + y
        sent = []
        for j, chip in enumerate(chips):
            k = 2 * chip[0] + chip[1]
            cp = pltpu.make_async_remote_copy(
                src_ref=s_ref.at[k], dst_ref=out_ref.at[me], send_sem=send_sems.at[j], recv_sem=recv_sems.at[j],
                device_id=(*chip, c), device_id_type=MESH)
            cp.start()
            sent.append(cp)
        for j, chip in enumerate(chips):
            k = 2 * chip[0] + chip[1]
            pltpu.make_async_remote_copy(
                src_ref=s_ref.at[k], dst_ref=out_ref.at[k], send_sem=send_sems.at[j], recv_sem=recv_sems.at[j],
                device_id=(*chip, c), device_id_type=MESH).wait_recv()
        for cp in sent:
            cp.wait_send()

    out = pl.pallas_call(
        body, in_specs=[ANY], out_specs=ANY, out_shape=jax.ShapeDtypeStruct((n, Rh, C), s.dtype),
        scratch_shapes=[pltpu.SemaphoreType.DMA((3,)), pltpu.SemaphoreType.DMA((3,))],
        name="grad_alltoall_chips" + tag,
    )(s)
    me = 2 * lax.axis_index("x") + lax.axis_index("y")
    return lax.dynamic_update_slice(out, lax.dynamic_slice_in_dim(s, me, 1, axis=0), (me, 0, 0))


def _join_core_halves(half, tag):
    Rh, C = half.shape

    def body(h_ref, out_ref, send_sem, recv_sem):
        x, y, c, _ = _place()
        cp = pltpu.make_async_remote_copy(
            src_ref=h_ref, dst_ref=out_ref.at[c], send_sem=send_sem, recv_sem=recv_sem,
            device_id=(x, y, 1 - c), device_id_type=MESH)
        cp.start()
        pltpu.make_async_remote_copy(
            src_ref=h_ref, dst_ref=out_ref.at[1 - c], send_sem=send_sem, recv_sem=recv_sem,
            device_id=(x, y, 1 - c), device_id_type=MESH).wait_recv()
        cp.wait_send()

    out = pl.pallas_call(
        body, in_specs=[ANY], out_specs=ANY, out_shape=jax.ShapeDtypeStruct((2, Rh, C), half.dtype),
        scratch_shapes=[pltpu.SemaphoreType.DMA, pltpu.SemaphoreType.DMA], name="grad_join_cores" + tag,
    )(half)
    return lax.dynamic_update_slice(out, half[None], (lax.axis_index("c"), 0, 0))


def _pack(arrays, cols, row_align):
    flat = jnp.concatenate([a.reshape(-1) for a in arrays])
    unit = cols * row_align
    total = -(-flat.size // unit) * unit
    return jnp.pad(flat, (0, total - flat.size)).reshape(total // cols, cols)


def _unpack(buf, shapes):
    flat = buf.reshape(-1)
    out, off = [], 0
    for shp in shapes:
        n = 1
        for d in shp:
            n *= d
        out.append(flat[off:off + n].reshape(shp))
        off += n
    return out


def _layer_fwd(x, p_l, w, tabs, dm):
    D, H, ql, kvl = dm["D"], dm["H"], dm["ql"], dm["kvl"]
    h = _rms_fwd(x, w["attn_norm"], "attn_norm_fwd")
    zbig = _mm(h, w["w_big"], "nn", "in_proj_big")
    zsm = _mm(h, w["w_sm"], "nn", "in_proj_small")
    qn, kvn, kr = _latent_fwd(zsm, w["q_a_norm"], w["kv_a_norm"], tabs, ql, kvl)
    q = _q_rope(_mm(qn, w["w_q"], "nn", "q_proj"), tabs, False, "q_rope_fwd", gain=QK_SCALE * LOG2E)
    kv = _mm(kvn, w["w_kv"], "nn", "kv_proj", out_dtype=BF16)
    o, a_mla, lse = _attn_fwd(q, kv, kr, zbig, H)
    y_mla = _mm(a_mla, w["w_o_mla"], "nn", "o_mla_proj")
    h_lru, a_lru = _lru_fwd(zbig, w["conv_w"], w["conv_b"], w["w_rg"], w["b_rg"], w["w_ig"], w["b_ig"], w["lru_lambda"], D)
    y_lru = _mm(a_lru, w["w_o_lru"], "nn", "o_lru_proj")
    merged = _merge_fwd(zbig, y_mla, y_lru, D)
    x1 = _mm(merged, w["w_out"], "nn", "out_proj", add=x)
    hp = _rms_fwd(x1, w["ple_norm"], "ple_norm_fwd")
    pg = _mm(hp, w["w_ple_gate"], "nn", "ple_gate_proj")
    pe = _mm(p_l, w["w_ple"], "nn", "ple_proj")
    x2 = _ple_fwd(x1, pe, pg)
    res = dict(x=x, h=h, zbig=zbig, zsm=zsm, qn=qn, kvn=kvn, kr=kr, q=q, kv=kv, o=o, a_mla=a_mla, lse=lse, y_mla=y_mla,
               h_lru=h_lru, a_lru=a_lru, y_lru=y_lru, merged=merged, x1=x1, hp=hp, pg=pg, pe=pe, p=p_l)
    return x2, res


def _layer_bwd(dx2, r, w, tabs, dm):
    D, H, ql, kvl = dm["D"], dm["H"], dm["ql"], dm["kvl"]
    S = dx2.shape[0]
    g = {}
    d_pe, d_pg = _ple_bwd(dx2, r["pe"], r["pg"])
    g["w_ple"] = _mm(r["p"], d_pe, "tn", "ple_proj_dw", out_dtype=BF16)
    g["w_ple_gate"] = _mm(r["hp"], d_pg, "tn", "ple_gate_dw", out_dtype=BF16)
    d_hp = _mm(d_pg, w["w_ple_gate"], "nt", "ple_gate_dx")
    dx1, g["ple_norm"] = _rms_bwd(r["x1"], w["ple_norm"], d_hp, dx2, "ple_norm_bwd")
    g["w_out"] = _mm(r["merged"], dx1, "tn", "out_proj_dw", out_dtype=BF16)
    d_merged = _mm(dx1, w["w_out"], "nt", "out_proj_dx")
    d_ym, d_yl, d_mm, d_ml = _merge_bwd(r["zbig"], r["y_mla"], r["y_lru"], d_merged, D)
    g["w_o_mla"] = _mm(r["a_mla"], d_ym, "tn", "o_mla_dw", out_dtype=BF16)
    d_a_mla = _mm(d_ym, w["w_o_mla"], "nt", "o_mla_dx")
    g["w_o_lru"] = _mm(r["a_lru"], d_yl, "tn", "o_lru_dw", out_dtype=BF16)
    d_a_lru = _mm(d_yl, w["w_o_lru"], "nt", "o_lru_dx")
    d_o, d_gm, delta = _attn_gate_bwd(d_a_mla, r["o"], r["zbig"], H)
    dq, dkv, dkr = _attn_bwd(r["q"], r["kv"], r["kr"], d_o, r["lse"][:, :, 0].reshape(H, 1, S), delta[:, :, 0].reshape(H, 1, S), H)
    dq_pre = _q_rope(dq, tabs, True, "q_rope_bwd", gain=QK_SCALE)
    g["w_q"] = _mm(r["qn"], dq_pre, "tn", "q_proj_dw", out_dtype=BF16)
    d_qn = _mm(dq_pre, w["w_q"], "nt", "q_proj_dx")
    g["w_kv"] = _mm(r["kvn"], dkv, "tn", "kv_proj_dw", out_dtype=BF16)
    d_kvn = _mm(dkv, w["w_kv"], "nt", "kv_proj_dx")
    dzsm, g["q_a_norm"], g["kv_a_norm"] = _latent_bwd(r["zsm"], w["q_a_norm"], w["kv_a_norm"], tabs, d_qn, d_kvn, dkr, ql, kvl)
    (d_u, d_gl, g["conv_w"], g["conv_b"], g["w_rg"], g["b_rg"], g["w_ig"], g["b_ig"], g["lru_lambda"]) = _lru_bwd(
        r["zbig"], r["h_lru"], d_a_lru, w["conv_w"], w["conv_b"], w["w_rg"], w["b_rg"], w["w_ig"], w["b_ig"], w["lru_lambda"], D)
    dzbig = jnp.concatenate([d_gm, d_u, d_gl, d_mm, d_ml], axis=1)
    g["w_big"] = _mm(r["h"], dzbig, "tn", "in_proj_big_dw", out_dtype=BF16)
    g["w_sm"] = _mm(r["h"], dzsm, "tn", "in_proj_small_dw", out_dtype=BF16)
    dh = _mm(dzbig, w["w_big"], "nt", "in_proj_big_dx")
    dh = _mm(dzsm, w["w_sm"], "nt", "in_proj_small_dx", add=dh)
    dx, g["attn_norm"] = _rms_bwd(r["x"], w["attn_norm"], dh, dx1, "attn_norm_bwd")
    return dx, g


SHARDED = ("w_in", "w_q_b", "w_kv_b", "w_o_mla", "w_o_lru", "w_out", "w_ple_gate", "w_ple")
COL_SHARDED = ("w_in", "w_q_b", "w_kv_b", "w_ple")
ROWED = ("w_o_mla", "w_o_lru", "w_out", "w_ple_gate")
FLAT = ("w_q_b", "w_kv_b", "w_ple")
REPLICATED = ("attn_norm", "q_a_norm", "kv_a_norm", "conv_b", "w_rg", "b_rg", "w_ig", "b_ig", "lru_lambda", "ple_norm", "final_norm")
WEIGHTS = ("attn_norm", "w_in", "q_a_norm", "w_q_b", "kv_a_norm", "w_kv_b", "conv_w", "conv_b", "w_rg", "b_rg", "w_ig", "b_ig",
           "lru_lambda", "w_o_mla", "w_o_lru", "w_out", "ple_norm", "w_ple_gate", "w_ple", "final_norm")


def kernel(x, p, positions, attn_norm, w_in, q_a_norm, w_q_b, kv_a_norm, w_kv_b, conv_w, conv_b, w_rg, b_rg, w_ig, b_ig, lru_lambda, w_o_mla, w_o_lru, w_out, ple_norm, w_ple_gate, w_ple, final_norm, loss_target, m_attn_norm, m_w_in, m_q_a_norm, m_w_q_b, m_kv_a_norm, m_w_kv_b, m_conv_w, m_conv_b, m_w_rg, m_b_rg, m_w_ig, m_b_ig, m_lru_lambda, m_w_o_mla, m_w_o_lru, m_w_out, m_ple_norm, m_w_ple_gate, m_w_ple, m_final_norm, v_attn_norm, v_w_in, v_q_a_norm, v_w_q_b, v_kv_a_norm, v_w_kv_b, v_conv_w, v_conv_b, v_w_rg, v_b_rg, v_w_ig, v_b_ig, v_lru_lambda, v_w_o_mla, v_w_o_lru, v_w_out, v_ple_norm, v_w_ple_gate, v_w_ple, v_final_norm):
    W = dict(attn_norm=attn_norm, w_in=w_in, q_a_norm=q_a_norm, w_q_b=w_q_b, kv_a_norm=kv_a_norm, w_kv_b=w_kv_b, conv_w=conv_w,
             conv_b=conv_b, w_rg=w_rg, b_rg=b_rg, w_ig=w_ig, b_ig=b_ig, lru_lambda=lru_lambda, w_o_mla=w_o_mla, w_o_lru=w_o_lru,
             w_out=w_out, ple_norm=ple_norm, w_ple_gate=w_ple_gate, w_ple=w_ple, final_norm=final_norm)
    M = dict(attn_norm=m_attn_norm, w_in=m_w_in, q_a_norm=m_q_a_norm, w_q_b=m_w_q_b, kv_a_norm=m_kv_a_norm, w_kv_b=m_w_kv_b,
             conv_w=m_conv_w, conv_b=m_conv_b, w_rg=m_w_rg, b_rg=m_b_rg, w_ig=m_w_ig, b_ig=m_b_ig, lru_lambda=m_lru_lambda,
             w_o_mla=m_w_o_mla, w_o_lru=m_w_o_lru, w_out=m_w_out, ple_norm=m_ple_norm, w_ple_gate=m_w_ple_gate, w_ple=m_w_ple,
             final_norm=m_final_norm)
    V = dict(attn_norm=v_attn_norm, w_in=v_w_in, q_a_norm=v_q_a_norm, w_q_b=v_w_q_b, kv_a_norm=v_kv_a_norm, w_kv_b=v_w_kv_b,
             conv_w=v_conv_w, conv_b=v_conv_b, w_rg=v_w_rg, b_rg=v_b_rg, w_ig=v_w_ig, b_ig=v_b_ig, lru_lambda=v_lru_lambda,
             w_o_mla=v_w_o_mla, w_o_lru=v_w_o_lru, w_out=v_w_out, ple_norm=v_ple_norm, w_ple_gate=v_w_ple_gate, w_ple=v_w_ple,
             final_norm=v_final_norm)
    depth = attn_norm.shape[0]
    S, D = x.shape[1], x.shape[2]
    ql, kvl = q_a_norm.shape[1], kv_a_norm.shape[1]
    H = w_q_b.shape[2] * N_CHIPS // (QK_NOPE + QK_ROPE)
    dm = dict(D=D, H=H, ql=ql, kvl=kvl)
    chip = 2 * lax.axis_index("x") + lax.axis_index("y")
    core = lax.axis_index("c")

    def rest_rows(get):
        rows = [a.reshape(-1, D) for n in ROWED for a in get(n)] + [_pack(get(n), D, 16) for n in FLAT]
        fill = -sum(r.shape[0] for r in rows) % PACK_ROWS
        return rows + ([jnp.zeros((fill, D), rows[0].dtype)] if fill else [])

    def unpack_rest(buf):
        out, off = {}, 0
        for n in ROWED + FLAT:
            rows = -(-W[n].size // (D * 16)) * 16
            part = buf[off:off + rows]
            out[n] = part.reshape(W[n].shape) if n in ROWED else _unpack(part, [W[n].shape])[0]
            off += rows
        return out

    cin = w_in.shape[2]
    got_in = _allgather_chips(w_in.astype(BF16).reshape(depth * D, cin), "w_in_allgather")
    got_rest = _allgather_chips(jnp.concatenate(rest_rows(lambda n: [W[n].astype(BF16)]), axis=0), "weights_allgather")
    slabs = [dict(unpack_rest(got_rest[k]), w_in=got_in[k].reshape(depth, D, cin)) for k in range(N_CHIPS)]
    cw_all = _allgather_chips(_pack([conv_w], LANES, 16), "conv_w_allgather")
    conv_w_full = jnp.concatenate([_unpack(cw_all[k], [conv_w.shape])[0] for k in range(N_CHIPS)], axis=-1)

    n_small = ql + kvl + QK_ROPE
    hpc = H // N_CHIPS
    head_pad = ((0, 0), (0, 0), (0, HEAD_PAD - QK_NOPE - QK_ROPE))
    layers = []
    for l in range(depth):
        cat = lambda n, axis: jnp.concatenate([s[n][l] for s in slabs], axis=axis)
        layers.append(dict(
            w_big=jnp.concatenate([slabs[0]["w_in"][l][:, n_small:]] + [s["w_in"][l] for s in slabs[1:]], axis=1),
            w_sm=jnp.pad(slabs[0]["w_in"][l][:, :n_small], ((0, 0), (0, LANES - QK_ROPE))),
            w_q=jnp.concatenate([jnp.pad(s["w_q_b"][l].reshape(ql, hpc, QK_NOPE + QK_ROPE), head_pad).reshape(ql, hpc * HEAD_PAD)
                                 for s in slabs], axis=1),
            w_kv=cat("w_kv_b", 1), w_o_mla=cat("w_o_mla", 0), w_o_lru=cat("w_o_lru", 0), w_out=cat("w_out", 0),
            w_ple_gate=cat("w_ple_gate", 0), w_ple=cat("w_ple", 1), conv_w=conv_w_full[l],
            **{n: W[n][l] for n in REPLICATED if n != "final_norm"}))

    inv_freq = ROPE_THETA ** (-jnp.arange(0, QK_ROPE, 2, dtype=F32) / QK_ROPE)
    tabs = _rope_tables(positions[0], inv_freq)

    xs = x[0]
    saved = []
    for l in range(depth):
        xs, res = _layer_fwd(xs, p[l, 0], layers[l], tabs, dm)
        saved.append(res)
    dx, g_final_norm, loss_part = _loss_head(xs, final_norm, loss_target[0])
    grads = [None] * depth
    for l in reversed(range(depth)):
        dx, grads[l] = _layer_bwd(dx, saved[l], layers[l], tabs, dm)

    def stack(name):
        return jnp.stack([grads[l][name] for l in range(depth)])

    def shard_of(l, n, k):
        g = grads[l]
        if n == "w_in":
            lo, hi = k * W[n].shape[2], (k + 1) * W[n].shape[2]
            parts = ([g["w_sm"][:, lo:min(hi, n_small)]] if lo < n_small else []) + (
                [g["w_big"][:, max(lo, n_small) - n_small:hi - n_small]] if hi > n_small else [])
            return jnp.concatenate(parts, axis=1)
        if n == "w_q_b":
            return g["w_q"].reshape(ql, H, HEAD_PAD)[:, k * hpc:(k + 1) * hpc, :QK_NOPE + QK_ROPE].reshape(ql, -1)
        mine = {"w_kv_b": "w_kv"}.get(n, n)
        if n in COL_SHARDED:
            return g[mine][:, k * W[n].shape[2]:(k + 1) * W[n].shape[2]]
        return g[mine][k * W[n].shape[1]:(k + 1) * W[n].shape[1], :]

    def reduce_scatter(slab_rows, tag):
        _, R, C = slab_rows.shape
        gp = slab_rows.reshape(N_CHIPS, 2, R // 2, C)
        core_sum = _sum_core_halves(gp, _swap_cores_half(gp, tag), core, tag)
        chip_sum = _sum_slabs(_alltoall_chips(core_sum, tag), "grad_sum_chips" + tag)
        return _join_core_halves(chip_sum, tag).reshape(R, C)

    in_rows = [shard_of(l, "w_in", k) for k in range(N_CHIPS) for l in range(depth)]
    g_in = reduce_scatter(jnp.concatenate(in_rows, axis=0).reshape(N_CHIPS, depth * D, cin), "_w_in")
    rest = []
    for k in range(N_CHIPS):
        rest += rest_rows(lambda n: [shard_of(l, n, k) for l in range(depth)])
    g_rest = reduce_scatter(jnp.concatenate(rest, axis=0).reshape(N_CHIPS, -1, D), "_rest")
    G = dict(unpack_rest(g_rest), w_in=g_in.reshape(w_in.shape))

    rep_shapes = [W[n].shape for n in REPLICATED] + [(depth, CONV_K, D), (LANES,)]
    rep = [stack(n).reshape(W[n].shape) for n in REPLICATED if n != "final_norm"]
    rep += [g_final_norm.reshape(D), stack("conv_w"), loss_part.reshape(LANES)]
    rep_sum = _unpack(_sum_slabs(_allgather_all(_pack(rep, LANES, SMALL_ROWS), "small_grads_allgather"), "small_grads_sum"), rep_shapes)
    for n, gv in zip(REPLICATED, rep_sum):
        G[n] = gv
    cshard = D // N_CHIPS
    G["conv_w"] = lax.dynamic_slice_in_dim(rep_sum[-2], chip * cshard, cshard, axis=2)
    loss = rep_sum[-1][0]

    small = REPLICATED + ("conv_w",)
    small_shapes = [W[n].shape for n in small]
    pk = lambda src: _pack([src[n] for n in small], LANES, SMALL_ROWS)
    upd = _adamw(pk(W), pk(G), pk(M), pk(V), "adamw_small")
    delta, new_m, new_v = ({n: a for n, a in zip(small, _unpack(u, small_shapes))} for u in upd)
    for n in SHARDED:
        shp = W[n].shape
        two_d = lambda a: a.reshape(-1, shp[-1])
        d_, m_, v_ = _adamw(two_d(W[n]), two_d(G[n]), two_d(M[n]), two_d(V[n]), "adamw_" + n)
        delta[n], new_m[n], new_v[n] = d_.reshape(shp), m_.reshape(shp), v_.reshape(shp)

    return (loss, dx.reshape(x.shape), *[G[n] for n in WEIGHTS], *[delta[n] for n in WEIGHTS],
            *[new_m[n] for n in WEIGHTS], *[new_v[n] for n in WEIGHTS])
```

```python
import jax
import jax.numpy as jnp
from jax import lax
from jax.experimental import pallas as pl
from jax.experimental.pallas import tpu as pltpu

F32 = jnp.float32
BF16 = jnp.bfloat16
MESH = pl.DeviceIdType.MESH

CHUNK = 64
QK_NOPE = 128
QK_ROPE = 64
V_HEAD = 128
ROPE_THETA = 10000.0
CONV_K = 4
LRU_C = 8.0
LRU_BLOCK_DIM = 128
EPS = 1e-6
ADAM_LR = 0.001
ADAM_B1 = 0.9
ADAM_B2 = 0.999
ADAM_EPS = 1e-08
ADAM_WD = 0.01
ADAM_STEP = 10

N_CHIPS = 4
N_DEV = 8
LANES = 128
HEAD_PAD = 256
VMEM_LIMIT = 48 * 1024 * 1024

MM_TILE_BYTES = 8 * 1024 * 1024
ATT_T = 512
ATT_SUB = 512
ROW_T = 256
LRU_TT = 512
LRU_CB = 512
PACK_ROWS = 1024
SMALL_ROWS = 512

NT_DIMS = (((1,), (1,)), ((), ()))


def _cp(sem):
    return pltpu.CompilerParams(dimension_semantics=sem, vmem_limit_bytes=VMEM_LIMIT)


def _pick(n, pref, align=LANES):
    if n <= pref:
        return n
    t = pref - pref % align
    while t >= align:
        if n % t == 0:
            return t
        t -= align
    return n


def _sig(x):
    return 1.0 / (1.0 + jnp.exp(-x))


def _mm(a, b, mode, name, out_dtype=F32, add=None, tm=None, tn=1024, tk=2048):
    if mode == "nn":
        (M, K), (_, N) = a.shape, b.shape
    elif mode == "nt":
        (M, K), (N, _) = a.shape, b.shape
    else:
        (K, M), (_, N) = a.shape, b.shape
    if tm is None:
        tm = 1024 if a.dtype.itemsize == 2 and b.dtype.itemsize == 2 else 512
    tm, tn = _pick(M, tm), _pick(N, tn)
    while tk > 512 and tk * (tm * a.dtype.itemsize + tn * b.dtype.itemsize) > MM_TILE_BYTES:
        tk //= 2
    tk = _pick(K, tk)
    nm, nn, nk = M // tm, N // tn, K // tk
    i_outer = nm * b.size * b.dtype.itemsize <= nn * a.size * a.dtype.itemsize

    def ij(g0, g1):
        return (g0, g1) if i_outer else (g1, g0)

    def amap(g0, g1, k):
        i, _ = ij(g0, g1)
        return (k, i) if mode == "tn" else (i, k)

    def bmap(g0, g1, k):
        _, j = ij(g0, g1)
        return (j, k) if mode == "nt" else (k, j)

    def omap(g0, g1, k):
        return ij(g0, g1)

    ablk = (tk, tm) if mode == "tn" else (tm, tk)
    bblk = (tn, tk) if mode == "nt" else (tk, tn)

    def body(*refs):
        if add is None:
            a_ref, b_ref, o_ref = refs[:3]
            add_ref = None
        else:
            a_ref, b_ref, add_ref, o_ref = refs[:4]
        x = a_ref[...].astype(BF16)
        y = b_ref[...].astype(BF16)
        if mode == "nn":
            p = jnp.dot(x, y, preferred_element_type=F32)
        elif mode == "nt":
            p = lax.dot_general(x, y, NT_DIMS, preferred_element_type=F32)
        else:
            p = jnp.dot(x.T, y, preferred_element_type=F32)
        if nk == 1:
            if add_ref is not None:
                p = p + add_ref[...]
            o_ref[...] = p.astype(out_dtype)
        else:
            acc = refs[-1]
            k = pl.program_id(2)

            @pl.when(k == 0)
            def _():
                acc[...] = p if add_ref is None else p + add_ref[...]

            @pl.when(k > 0)
            def _():
                acc[...] += p

            @pl.when(k == nk - 1)
            def _():
                o_ref[...] = acc[...].astype(out_dtype)

    in_specs = [pl.BlockSpec(ablk, amap), pl.BlockSpec(bblk, bmap)]
    args = [a, b]
    if add is not None:
        in_specs.append(pl.BlockSpec((tm, tn), omap))
        args.append(add)
    grid = (nm, nn, nk) if i_outer else (nn, nm, nk)
    return pl.pallas_call(
        body, grid=grid, in_specs=in_specs, out_specs=pl.BlockSpec((tm, tn), omap),
        out_shape=jax.ShapeDtypeStruct((M, N), out_dtype),
        scratch_shapes=[pltpu.VMEM((tm, tn), F32)] if nk > 1 else [],
        compiler_params=_cp(("parallel", "parallel", "arbitrary")), name=name,
    )(*args)


def _rows(cols, i=0):
    def make(tm):
        return pl.BlockSpec((tm, cols), lambda r: (r, i))
    return make


def _par(cols):
    return pl.BlockSpec((1, cols), lambda r: (0, 0))


def _rms_fwd(x, g, name):
    S, D = x.shape
    tm = _pick(S, ROW_T, 8)

    def body(x_ref, g_ref, o_ref):
        xv = x_ref[...]
        rs = lax.rsqrt(jnp.mean(xv * xv, axis=-1, keepdims=True) + EPS)
        o_ref[...] = (xv * rs * g_ref[...]).astype(BF16)

    return pl.pallas_call(
        body, grid=(S // tm,), in_specs=[_rows(D)(tm), _par(D)], out_specs=_rows(D)(tm),
        out_shape=jax.ShapeDtypeStruct((S, D), BF16), compiler_params=_cp(("parallel",)), name=name,
    )(x, g.reshape(1, D))


def _rms_bwd_math(xv, g, dy):
    rs = lax.rsqrt(jnp.mean(xv * xv, axis=-1, keepdims=True) + EPS)
    xh = xv * rs
    dg = jnp.sum(dy * xh, axis=0, keepdims=True)
    dyg = dy * g
    dx = rs * (dyg - xh * jnp.mean(dyg * xh, axis=-1, keepdims=True))
    return dx, dg


def _accum(ref, val, first):
    @pl.when(first)
    def _():
        ref[...] = val

    @pl.when(jnp.logical_not(first))
    def _():
        ref[...] += val


def _rms_bwd(x, g, dy, dres, name):
    S, D = x.shape
    tm = _pick(S, ROW_T, 8)

    def body(x_ref, g_ref, dy_ref, dres_ref, dx_ref, dg_ref):
        dx, dg = _rms_bwd_math(x_ref[...], g_ref[...], dy_ref[...])
        dx_ref[...] = dres_ref[...] + dx
        _accum(dg_ref, dg, pl.program_id(0) == 0)

    return pl.pallas_call(
        body, grid=(S // tm,), in_specs=[_rows(D)(tm), _par(D), _rows(D)(tm), _rows(D)(tm)],
        out_specs=[_rows(D)(tm), _par(D)],
        out_shape=[jax.ShapeDtypeStruct((S, D), F32), jax.ShapeDtypeStruct((1, D), F32)],
        compiler_params=_cp(("arbitrary",)), name=name,
    )(x, g.reshape(1, D), dy, dres)


def _rope_tables(pos, inv_freq):
    S = pos.shape[0]
    tm = _pick(S, 512, 8)
    half = QK_ROPE // 2
    invf = jnp.concatenate([inv_freq, inv_freq, jnp.zeros((LANES - QK_ROPE,), F32)]).reshape(1, LANES)

    def body(pos_ref, f_ref, c_ref, sa_ref, sb_ref):
        ang = pos_ref[...].astype(F32) * f_ref[...]
        lane = lax.broadcasted_iota(jnp.int32, ang.shape, 1)
        c, s = jnp.cos(ang), jnp.sin(ang)
        c_ref[...] = jnp.where(lane < QK_ROPE, c, 0.0)
        sa_ref[...] = jnp.where(lane < half, -s, 0.0)
        sb_ref[...] = jnp.where((lane >= half) & (lane < QK_ROPE), s, 0.0)

    tab = jax.ShapeDtypeStruct((S, LANES), F32)
    return pl.pallas_call(
        body, grid=(S // tm,), in_specs=[pl.BlockSpec((tm, 1), lambda r: (r, 0)), _par(LANES)],
        out_specs=[_rows(LANES)(tm)] * 3, out_shape=[tab] * 3, compiler_params=_cp(("parallel",)), name="rope_tables",
    )(pos.reshape(S, 1), invf)


def _rope(x, c, sa, sb):
    return x * c + pltpu.roll(x, LANES - QK_ROPE // 2, 1) * sa + pltpu.roll(x, QK_ROPE // 2, 1) * sb


def _rope_t(d, c, sa, sb):
    return d * c + pltpu.roll(d * sa, QK_ROPE // 2, 1) + pltpu.roll(d * sb, LANES - QK_ROPE // 2, 1)


def _latent_fwd(zsm, gq, gkv, tabs, ql, kvl):
    S = zsm.shape[0]
    tm = _pick(S, ROW_T, 8)
    kr_blk = (ql + kvl) // LANES

    def body(q_ref, kv_ref, kr_ref, gq_ref, gkv_ref, c_ref, sa_ref, sb_ref, qn_ref, kvn_ref, kro_ref):
        for src, g_ref, dst in ((q_ref, gq_ref, qn_ref), (kv_ref, gkv_ref, kvn_ref)):
            v = src[...]
            rs = lax.rsqrt(jnp.mean(v * v, axis=-1, keepdims=True) + EPS)
            dst[...] = (v * rs * g_ref[...]).astype(BF16)
        kro_ref[...] = _rope(kr_ref[...], c_ref[...], sa_ref[...], sb_ref[...]).astype(BF16)

    return pl.pallas_call(
        body, grid=(S // tm,),
        in_specs=[_rows(ql, 0)(tm), _rows(kvl, 1)(tm), _rows(LANES, kr_blk)(tm), _par(ql), _par(kvl)] + [_rows(LANES)(tm)] * 3,
        out_specs=[_rows(ql)(tm), _rows(kvl)(tm), _rows(LANES)(tm)],
        out_shape=[jax.ShapeDtypeStruct((S, ql), BF16), jax.ShapeDtypeStruct((S, kvl), BF16), jax.ShapeDtypeStruct((S, LANES), BF16)],
        compiler_params=_cp(("parallel",)), name="latent_fwd",
    )(zsm, zsm, zsm, gq.reshape(1, ql), gkv.reshape(1, kvl), *tabs)


def _latent_bwd(zsm, gq, gkv, tabs, d_qn, d_kvn, dkr, ql, kvl):
    S, W = zsm.shape
    H = dkr.shape[0]
    tm = _pick(S, ROW_T, 8)
    kr_blk = (ql + kvl) // LANES

    def body(q_ref, kv_ref, gq_ref, gkv_ref, c_ref, sa_ref, sb_ref, dqn_ref, dkvn_ref, dkr_ref, dz_ref, dgq_ref, dgkv_ref):
        first = pl.program_id(0) == 0
        dq, dgq = _rms_bwd_math(q_ref[...], gq_ref[...], dqn_ref[...])
        dkv, dgkv = _rms_bwd_math(kv_ref[...], gkv_ref[...], dkvn_ref[...])
        dk = dkr_ref[0]
        for h in range(1, H):
            dk = dk + dkr_ref[h]
        dz_ref[:, 0:ql] = dq.astype(BF16)
        dz_ref[:, ql:ql + kvl] = dkv.astype(BF16)
        dz_ref[:, ql + kvl:] = _rope_t(dk, c_ref[...], sa_ref[...], sb_ref[...]).astype(BF16)
        _accum(dgq_ref, dgq, first)
        _accum(dgkv_ref, dgkv, first)

    return pl.pallas_call(
        body, grid=(S // tm,),
        in_specs=[_rows(ql, 0)(tm), _rows(kvl, 1)(tm), _par(ql), _par(kvl)] + [_rows(LANES)(tm)] * 3
        + [_rows(ql)(tm), _rows(kvl)(tm), pl.BlockSpec((H, tm, LANES), lambda r: (0, r, 0))],
        out_specs=[_rows(W)(tm), _par(ql), _par(kvl)],
        out_shape=[jax.ShapeDtypeStruct((S, W), BF16), jax.ShapeDtypeStruct((1, ql), F32), jax.ShapeDtypeStruct((1, kvl), F32)],
        compiler_params=_cp(("arbitrary",)), name="latent_bwd",
    )(zsm, zsm, gq.reshape(1, ql), gkv.reshape(1, kvl), *tabs, d_qn, d_kvn, dkr)


def _q_rope(q, tabs, transpose, name, gain=1.0):
    S, W = q.shape
    H = W // HEAD_PAD
    tm = _pick(S, ROW_T, 8)
    fn = _rope_t if transpose else _rope

    def body(q_ref, c_ref, sa_ref, sb_ref, o_ref):
        c, sa, sb = c_ref[...], sa_ref[...], sb_ref[...]
        if gain != 1.0:
            c, sa, sb = c * gain, sa * gain, sb * gain
        for h in range(H):
            lo = h * HEAD_PAD
            nope = q_ref[:, lo:lo + QK_NOPE]
            o_ref[:, lo:lo + QK_NOPE] = (nope if gain == 1.0 else nope * gain).astype(BF16)
            o_ref[:, lo + QK_NOPE:lo + HEAD_PAD] = fn(q_ref[:, lo + QK_NOPE:lo + HEAD_PAD], c, sa, sb).astype(BF16)

    return pl.pallas_call(
        body, grid=(S // tm,), in_specs=[_rows(W)(tm)] + [_rows(LANES)(tm)] * 3, out_specs=_rows(W)(tm),
        out_shape=jax.ShapeDtypeStruct((S, W), BF16), compiler_params=_cp(("parallel",)), name=name,
    )(q, *tabs)


def _merge_fwd(zbig, y_mla, y_lru, D):
    S = y_mla.shape[0]
    tm = _pick(S, ROW_T, 8)

    def body(mm_ref, ml_ref, ym_ref, yl_ref, o_ref):
        o_ref[...] = (_sig(mm_ref[...]) * ym_ref[...] + _sig(ml_ref[...]) * yl_ref[...]).astype(BF16)

    return pl.pallas_call(
        body, grid=(S // tm,), in_specs=[_rows(D, 3)(tm), _rows(D, 4)(tm), _rows(D)(tm), _rows(D)(tm)], out_specs=_rows(D)(tm),
        out_shape=jax.ShapeDtypeStruct((S, D), BF16), compiler_params=_cp(("parallel",)), name="merge_fwd",
    )(zbig, zbig, y_mla, y_lru)


def _merge_bwd(zbig, y_mla, y_lru, d_merged, D):
    S = y_mla.shape[0]
    tm = _pick(S, ROW_T, 8)

    def body(mm_ref, ml_ref, ym_ref, yl_ref, d_ref, dym_ref, dyl_ref, dmm_ref, dml_ref):
        d = d_ref[...]
        sm, sl = _sig(mm_ref[...]), _sig(ml_ref[...])
        dym_ref[...] = (d * sm).astype(BF16)
        dyl_ref[...] = (d * sl).astype(BF16)
        dmm_ref[...] = (d * ym_ref[...] * sm * (1.0 - sm)).astype(BF16)
        dml_ref[...] = (d * yl_ref[...] * sl * (1.0 - sl)).astype(BF16)

    o = jax.ShapeDtypeStruct((S, D), BF16)
    return pl.pallas_call(
        body, grid=(S // tm,), in_specs=[_rows(D, 3)(tm), _rows(D, 4)(tm)] + [_rows(D)(tm)] * 3, out_specs=[_rows(D)(tm)] * 4,
        out_shape=[o] * 4, compiler_params=_cp(("parallel",)), name="merge_bwd",
    )(zbig, zbig, y_mla, y_lru, d_merged)


def _ple_fwd(x1, pe, pg):
    S, D = x1.shape
    tm = _pick(S, ROW_T, 8)

    def body(x_ref, pe_ref, pg_ref, o_ref):
        o_ref[...] = x_ref[...] + pe_ref[...] * _sig(pg_ref[...])

    return pl.pallas_call(
        body, grid=(S // tm,), in_specs=[_rows(D)(tm)] * 3, out_specs=_rows(D)(tm),
        out_shape=jax.ShapeDtypeStruct((S, D), F32), compiler_params=_cp(("parallel",)), name="ple_fwd",
    )(x1, pe, pg)


def _ple_bwd(dx2, pe, pg):
    S, D = dx2.shape
    tm = _pick(S, ROW_T, 8)

    def body(d_ref, pe_ref, pg_ref, dpe_ref, dpg_ref):
        d = d_ref[...]
        s = _sig(pg_ref[...])
        dpe_ref[...] = (d * s).astype(BF16)
        dpg_ref[...] = (d * pe_ref[...] * s * (1.0 - s)).astype(BF16)

    o = jax.ShapeDtypeStruct((S, D), BF16)
    return pl.pallas_call(
        body, grid=(S // tm,), in_specs=[_rows(D)(tm)] * 3, out_specs=[_rows(D)(tm)] * 2, out_shape=[o] * 2,
        compiler_params=_cp(("parallel",)), name="ple_bwd",
    )(dx2, pe, pg)


def _loss_head(x, g, target):
    S, D = x.shape
    tm = _pick(S, ROW_T, 8)

    def body(x_ref, g_ref, t_ref, dx_ref, dg_ref, loss_ref):
        first = pl.program_id(0) == 0
        xv, gv = x_ref[...], g_ref[...]
        rs = lax.rsqrt(jnp.mean(xv * xv, axis=-1, keepdims=True) + EPS)
        e = xv * rs * gv - t_ref[...]
        part = 0.5 * jnp.sum(jnp.mean(e * e, axis=-1, keepdims=True), axis=0, keepdims=True)
        dx, dg = _rms_bwd_math(xv, gv, e * (1.0 / D))
        dx_ref[...] = dx
        _accum(dg_ref, dg, first)
        _accum(loss_ref, jnp.broadcast_to(part, (1, LANES)), first)

    return pl.pallas_call(
        body, grid=(S // tm,), in_specs=[_rows(D)(tm), _par(D), _rows(D)(tm)], out_specs=[_rows(D)(tm), _par(D), _par(LANES)],
        out_shape=[jax.ShapeDtypeStruct((S, D), F32), jax.ShapeDtypeStruct((1, D), F32), jax.ShapeDtypeStruct((1, LANES), F32)],
        compiler_params=_cp(("arbitrary",)), name="loss_head",
    )(x, g.reshape(1, D), target)


def _attn_gate_bwd(d_a, o, zbig, H):
    S, W = o.shape
    tm = _pick(S, ROW_T, 8)

    def body(da_ref, o_ref, g_ref, do_ref, dg_ref, dl_ref):
        da, ov, g = da_ref[...], o_ref[...], g_ref[...]
        s = _sig(g)
        do = da * g * s
        do_ref[...] = do.astype(BF16)
        dg_ref[...] = (da * ov * s * (1.0 + g * (1.0 - s))).astype(BF16)
        prod = do * ov
        for h in range(H):
            r = jnp.sum(prod[:, h * V_HEAD:(h + 1) * V_HEAD], axis=-1, keepdims=True)
            dl_ref[h] = jnp.broadcast_to(r, (tm, LANES))

    return pl.pallas_call(
        body, grid=(S // tm,), in_specs=[_rows(W)(tm), _rows(W)(tm), _rows(W, 0)(tm)],
        out_specs=[_rows(W)(tm), _rows(W)(tm), pl.BlockSpec((H, tm, LANES), lambda r: (0, r, 0))],
        out_shape=[jax.ShapeDtypeStruct((S, W), BF16), jax.ShapeDtypeStruct((S, W), BF16), jax.ShapeDtypeStruct((H, S, LANES), F32)],
        compiler_params=_cp(("parallel",)), name="attn_gate_bwd",
    )(d_a, o, zbig)


def _chunk_mask(ts, t, lo, q_rows):
    r = (lax.broadcasted_iota(jnp.int32, (ts, t), 0) + lo) // CHUNK
    c = lax.broadcasted_iota(jnp.int32, (ts, t), 1) // CHUNK
    return (c <= r) if q_rows else (r <= c)


def _attn_tiles(S):
    t = _pick(S, ATT_T)
    ts = _pick(t, ATT_SUB, 8)
    return t, ts, [r * ts for r in range(t // ts)]


QK_SCALE = 1.0 / (QK_NOPE + QK_ROPE) ** 0.5
LOG2E = 1.4426950408889634


def _attn_fwd(q, kv, kr, zbig, H):
    S = q.shape[0]
    t, ts, subs = _attn_tiles(S)
    nq = S // t

    def body(q_ref, kn_ref, v_ref, kr_ref, g_ref, o_ref, a_ref, lse_ref, m_scr, l_scr, acc_scr, s0_scr, s1_scr, p0_scr, p1_scr):
        i = pl.program_id(1)
        m_scr[...] = jnp.full((t, LANES), -1e30, F32)
        l_scr[...] = jnp.zeros((t, LANES), F32)
        acc_scr[...] = jnp.zeros((t, V_HEAD), F32)

        s_buf, p_buf = (s0_scr, s1_scr), (p0_scr, p1_scr)

        def logits(j, slot):
            ks = pl.multiple_of(j * t, t)
            k = jnp.concatenate([kn_ref[pl.ds(ks, t), :], kr_ref[pl.ds(ks, t), :]], axis=1)
            s_buf[slot][...] = lax.dot_general(q_ref[...], k, NT_DIMS, preferred_element_type=F32)

        def values(j, slot):
            ks = pl.multiple_of(jnp.maximum(j, 0) * t, t)
            acc_scr[...] += jnp.dot(p_buf[slot][...], v_ref[pl.ds(ks, t), :], preferred_element_type=F32)

        def softmax(slot, masked):
            s = s_buf[slot][...]
            if masked:
                s = jnp.where(_chunk_mask(t, t, 0, True), s, -1e30)
            m_prev = m_scr[...]
            m_next = jnp.maximum(m_prev, jnp.max(s, axis=1, keepdims=True))
            p = jnp.exp2(s - jnp.tile(m_next, (1, t // LANES)))
            alpha = jnp.exp2(m_prev - m_next)
            l_scr[...] = alpha * l_scr[...] + jnp.sum(p, axis=1, keepdims=True)
            m_scr[...] = m_next
            acc_scr[...] = acc_scr[...] * alpha
            p_buf[slot][...] = p.astype(BF16)

        def step(j, slot, masked=False, more=True):
            if more:
                logits(j + 1, 1 - slot)
            values(j - 1, 1 - slot)
            softmax(slot, masked)

        logits(0, 0)
        p1_scr[...] = jnp.zeros((t, t), BF16)

        def loop(a, carry):
            step(2 * a, 0)
            step(2 * a + 1, 1)
            return carry

        lax.fori_loop(0, i // 2, loop, 0)

        @pl.when(i % 2 == 0)
        def _():
            step(i, 0, masked=True, more=False)
            values(i, 0)

        @pl.when(i % 2 == 1)
        def _():
            step(i - 1, 0)
            step(i, 1, masked=True, more=False)
            values(i, 1)

        l = l_scr[...]
        ov = acc_scr[...] / l
        g = g_ref[...]
        o_ref[...] = ov
        a_ref[...] = (ov * g * _sig(g)).astype(BF16)
        lse_ref[0] = m_scr[...] + jnp.log(l) * LOG2E

    head_col = lambda w, off: pl.BlockSpec((S, w), lambda h, i: (0, 2 * h + off))
    return pl.pallas_call(
        body, grid=(H, nq),
        in_specs=[pl.BlockSpec((t, HEAD_PAD), lambda h, i: (i, h)), head_col(QK_NOPE, 0), head_col(V_HEAD, 1),
                  pl.BlockSpec((S, LANES), lambda h, i: (0, 0)), pl.BlockSpec((t, V_HEAD), lambda h, i: (i, h))],
        out_specs=[pl.BlockSpec((t, V_HEAD), lambda h, i: (i, h)), pl.BlockSpec((t, V_HEAD), lambda h, i: (i, h)),
                   pl.BlockSpec((1, t, LANES), lambda h, i: (h, i, 0))],
        out_shape=[jax.ShapeDtypeStruct((S, H * V_HEAD), F32), jax.ShapeDtypeStruct((S, H * V_HEAD), BF16),
                   jax.ShapeDtypeStruct((H, S, LANES), F32)],
        scratch_shapes=[pltpu.VMEM((t, LANES), F32), pltpu.VMEM((t, LANES), F32), pltpu.VMEM((t, V_HEAD), F32),
                        pltpu.VMEM((t, t), F32), pltpu.VMEM((t, t), F32), pltpu.VMEM((t, t), BF16), pltpu.VMEM((t, t), BF16)],
        compiler_params=_cp(("parallel", "arbitrary")), name="attn_fwd",
    )(q, kv, kv, kr, zbig)


TN_DIMS = (((0,), (0,)), ((), ()))


def _attn_bwd(q, kv, kr, do, lse_row, delta_row, H):
    S = q.shape[0]
    t, _, _ = _attn_tiles(S)
    nk = S // t

    def body(kn_ref, v_ref, kr_ref, q_ref, do_ref, lse_ref, dl_ref, dq_ref, dkv_ref, dkr_ref, dk_scr, dv_scr,
             st0_scr, st1_scr, dp0_scr, dp1_scr):
        j = pl.program_id(1)
        dk_scr[...] = jnp.zeros((t, HEAD_PAD), F32)
        dv_scr[...] = jnp.zeros((t, V_HEAD), F32)

        @pl.when(j == 0)
        def _():
            dq_ref[...] = jnp.zeros((S, HEAD_PAD), F32)

        st_buf, dp_buf = (st0_scr, st1_scr), (dp0_scr, dp1_scr)

        def keys():
            return jnp.concatenate([kn_ref[...], kr_ref[...]], axis=1)

        def rows(i):
            return pl.ds(pl.multiple_of(jnp.minimum(i, nk - 1) * t, t), t)

        def scores(i, slot):
            st_buf[slot][...] = lax.dot_general(keys(), q_ref[rows(i), :], NT_DIMS, preferred_element_type=F32)
            dp_buf[slot][...] = lax.dot_general(v_ref[...], do_ref[rows(i), :], NT_DIMS, preferred_element_type=F32)

        def step(s, slot, masked=False):
            i = j + s
            scores(i + 1, 1 - slot)
            pt = jnp.exp2(st_buf[slot][...] - lse_ref[0, :, rows(i)])
            if masked:
                pt = jnp.where(_chunk_mask(t, t, 0, False), pt, 0.0)
            dst = (pt * (dp_buf[slot][...] - dl_ref[0, :, rows(i)])).astype(BF16)
            dv_scr[...] += jnp.dot(pt.astype(BF16), do_ref[rows(i), :], preferred_element_type=F32)
            dk_scr[...] += jnp.dot(dst, q_ref[rows(i), :], preferred_element_type=F32)
            dq_ref[rows(i), :] += lax.dot_general(dst, keys(), TN_DIMS, preferred_element_type=F32)

        n = nk - j
        scores(j, 0)
        step(0, 0, masked=True)

        def loop(a, carry):
            step(2 * a + 1, 1)
            step(2 * a + 2, 0)
            return carry

        lax.fori_loop(0, (n - 1) // 2, loop, 0)

        @pl.when((n - 1) % 2 == 1)
        def _():
            step(n - 1, 1)

        dk = dk_scr[...] * (1.0 / LOG2E)
        dkv_ref[:, 0:QK_NOPE] = dk[:, 0:QK_NOPE].astype(BF16)
        dkv_ref[:, QK_NOPE:] = dv_scr[...].astype(BF16)
        dkr_ref[0] = dk[:, QK_NOPE:]

    tile_col = lambda w, off: pl.BlockSpec((t, w), lambda h, j: (j, 2 * h + off))
    row = pl.BlockSpec((1, 1, S), lambda h, j: (h, 0, 0))
    return pl.pallas_call(
        body, grid=(H, nk),
        in_specs=[tile_col(QK_NOPE, 0), tile_col(V_HEAD, 1), pl.BlockSpec((t, LANES), lambda h, j: (j, 0)),
                  pl.BlockSpec((S, HEAD_PAD), lambda h, j: (0, h)), pl.BlockSpec((S, V_HEAD), lambda h, j: (0, h)), row, row],
        out_specs=[pl.BlockSpec((S, HEAD_PAD), lambda h, j: (0, h)), pl.BlockSpec((t, HEAD_PAD), lambda h, j: (j, h)),
                   pl.BlockSpec((1, t, LANES), lambda h, j: (h, j, 0))],
        out_shape=[jax.ShapeDtypeStruct((S, H * HEAD_PAD), F32), jax.ShapeDtypeStruct((S, H * HEAD_PAD), BF16),
                   jax.ShapeDtypeStruct((H, S, LANES), F32)],
        scratch_shapes=[pltpu.VMEM((t, HEAD_PAD), F32), pltpu.VMEM((t, V_HEAD), F32)] + [pltpu.VMEM((t, t), F32)] * 4,
        compiler_params=_cp(("arbitrary", "arbitrary")), name="attn_bwd",
    )(kv, kv, kr, q, do, lse_row, delta_row)


def _shift_down(x, prev8, s):
    rx = pltpu.roll(x, s, 0)
    rp = pltpu.roll(prev8, s, 0)
    rows = lax.broadcasted_iota(jnp.int32, rp.shape, 0)
    return jnp.concatenate([jnp.where(rows < s, rp, rx[:8]), rx[8:]], axis=0)


def _shift_up(x, next8, s):
    n = x.shape[0]
    rx = pltpu.roll(x, n - s, 0)
    rn = pltpu.roll(next8, 8 - s, 0)
    rows = lax.broadcasted_iota(jnp.int32, rn.shape, 0)
    return jnp.concatenate([rx[:n - 8], jnp.where(rows >= 8 - s, rn, rx[n - 8:])], axis=0)


def _scan_rows(a, b, up):
    n = a.shape[0]
    rows = lax.broadcasted_iota(jnp.int32, a.shape, 0)
    d = 1
    while d < n:
        keep = (rows < n - d) if up else (rows >= d)
        sh = n - d if up else d
        a_s = jnp.where(keep, pltpu.roll(a, sh, 0), 1.0)
        b_s = jnp.where(keep, pltpu.roll(b, sh, 0), 0.0)
        b = a * b_s + b
        a = a * a_s
        d *= 2
    return a, b


def _log1p(e):
    u = 1.0 + e
    return jnp.where(u == 1.0, e, jnp.log(u) * (e / (u - 1.0)))


def _lru_pre(u, prev8, cw_ref, cb_ref, wr_ref, br_ref, wi_ref, bi_ref, lam_ref):
    us = [u, _shift_down(u, prev8, 1), _shift_down(u, prev8, 2), _shift_down(u, prev8, 3)]
    xc = cb_ref[...] + cw_ref[3:4, :] * us[0] + cw_ref[2:3, :] * us[1] + cw_ref[1:2, :] * us[2] + cw_ref[0:1, :] * us[3]
    x16 = xc.astype(BF16)
    nb = xc.shape[1] // LRU_BLOCK_DIM
    blk = lambda k: slice(k * LRU_BLOCK_DIM, (k + 1) * LRU_BLOCK_DIM)
    pr = jnp.concatenate([jnp.dot(x16[:, blk(k)], wr_ref[k].astype(BF16), preferred_element_type=F32) for k in range(nb)], axis=1)
    pi = jnp.concatenate([jnp.dot(x16[:, blk(k)], wi_ref[k].astype(BF16), preferred_element_type=F32) for k in range(nb)], axis=1)
    r = _sig(pr + br_ref[...])
    i = _sig(pi + bi_ref[...])
    nlam = -lam_ref[...]
    sp = jnp.maximum(nlam, 0.0) + _log1p(jnp.exp(-jnp.abs(nlam)))
    log_a = (-LRU_C * r) * sp
    a = jnp.exp(log_a)
    mult = jnp.sqrt(-jnp.tanh(log_a) * (a * a + 1.0))
    return us, xc, x16, r, i, sp, a, mult


def _lru_specs(D, cb, tt, nT, rev):
    nb = cb // LRU_BLOCK_DIM
    tmap = (lambda t: nT - 1 - t) if rev else (lambda t: t)
    ncb = D // cb

    def tile(piece):
        return pl.BlockSpec((tt, cb), lambda c, t: (tmap(t), piece * ncb + c))

    def halo(piece):
        return pl.BlockSpec((8, cb), lambda c, t: (jnp.maximum(tmap(t) * (tt // 8) - 1, 0), piece * ncb + c))

    par = lambda rows: pl.BlockSpec((rows, cb), lambda c, t: (0, c))
    wblk = pl.BlockSpec((nb, LRU_BLOCK_DIM, LRU_BLOCK_DIM), lambda c, t: (c, 0, 0))
    return tile, halo, par, wblk, tmap


def _lru_fwd(zbig, cw, cbias, wr, br, wi, bi, lam, D):
    S = zbig.shape[0]
    tt, cb = _pick(S, LRU_TT, 8), _pick(D, LRU_CB)
    nT = S // tt
    tile, halo, par, wblk, _ = _lru_specs(D, cb, tt, nT, False)

    def body(u_ref, up_ref, g_ref, cw_ref, cb_ref, wr_ref, br_ref, wi_ref, bi_ref, lam_ref, h_ref, al_ref, carry):
        t = pl.program_id(1)
        prev8 = jnp.where(t > 0, up_ref[...], 0.0)
        _, xc, _, _, i, _, a, mult = _lru_pre(u_ref[...], prev8, cw_ref, cb_ref, wr_ref, br_ref, wi_ref, bi_ref, lam_ref)
        pa, hb = _scan_rows(a, mult * (i * xc), False)
        h0 = jnp.where(t > 0, carry[7:8, :], 0.0)
        h = hb + pa * h0
        h_ref[...] = h
        carry[...] = h[tt - 8:, :]
        g = g_ref[...]
        al_ref[...] = (h * g * _sig(g)).astype(BF16)

    return pl.pallas_call(
        body, grid=(D // cb, nT),
        in_specs=[tile(1), halo(1), tile(2), par(CONV_K), par(1), wblk, par(1), wblk, par(1), par(1)],
        out_specs=[tile(0), tile(0)],
        out_shape=[jax.ShapeDtypeStruct((S, D), F32), jax.ShapeDtypeStruct((S, D), BF16)],
        scratch_shapes=[pltpu.VMEM((8, cb), F32)],
        compiler_params=_cp(("parallel", "arbitrary")), name="lru_fwd",
    )(zbig, zbig, zbig, cw, cbias.reshape(1, D), wr, br.reshape(1, D), wi, bi.reshape(1, D), lam.reshape(1, D))


def _lru_bwd(zbig, h, d_al, cw, cbias, wr, br, wi, bi, lam, D):
    S = zbig.shape[0]
    tt, cb = _pick(S, LRU_TT, 8), _pick(D, LRU_CB)
    nT = S // tt
    nb = cb // LRU_BLOCK_DIM
    tile, halo, par, wblk, tmap = _lru_specs(D, cb, tt, nT, True)

    def body(u_ref, up_ref, g_ref, h_ref, hp_ref, dal_ref, cw_ref, cb_ref, wr_ref, br_ref, wi_ref, bi_ref, lam_ref,
             du_ref, dg_ref, dcw_ref, dcb_ref, dwr_ref, dbr_ref, dwi_ref, dbi_ref, dlam_ref, g_car, a_car, x_car):
        step = pl.program_id(1)
        first = step == 0
        t = nT - 1 - step
        prev8 = jnp.where(t > 0, up_ref[...], 0.0)
        us, xc, x16, r, i, sp, a, mult = _lru_pre(u_ref[...], prev8, cw_ref, cb_ref, wr_ref, br_ref, wi_ref, bi_ref, lam_ref)
        hv = h_ref[...]
        h_m1 = _shift_down(hv, jnp.where(t > 0, hp_ref[...], 0.0), 1)
        g, dal = g_ref[...], dal_ref[...]
        sg = _sig(g)
        dg_ref[...] = (dal * hv * sg * (1.0 + g * (1.0 - sg))).astype(BF16)
        dh = dal * g * sg
        coef = _shift_up(a, jnp.where(first, 0.0, a_car[...]), 1)
        pa, gb = _scan_rows(coef, dh, True)
        G = gb + pa * jnp.where(first, 0.0, g_car[0:1, :])
        g_car[...] = G[:8]
        a_car[...] = a[:8]
        da = G * h_m1
        ixc = i * xc
        dixc = G * mult
        dlog = da * a - (G * ixc) * (a * a) / mult
        dpr = dlog * (-LRU_C * sp) * r * (1.0 - r)
        dpi = dixc * xc * i * (1.0 - i)
        dxc = dixc * i
        dsp = jnp.sum(dlog * (-LRU_C) * r, axis=0, keepdims=True)
        dpr16, dpi16 = dpr.astype(BF16), dpi.astype(BF16)
        blk = lambda k: slice(k * LRU_BLOCK_DIM, (k + 1) * LRU_BLOCK_DIM)
        back = []
        for k in range(nb):
            xk = x16[:, blk(k)].T
            dwr_k = jnp.dot(xk, dpr16[:, blk(k)], preferred_element_type=F32)
            dwi_k = jnp.dot(xk, dpi16[:, blk(k)], preferred_element_type=F32)

            @pl.when(first)
            def _():
                dwr_ref[k] = dwr_k
                dwi_ref[k] = dwi_k

            @pl.when(jnp.logical_not(first))
            def _():
                dwr_ref[k] += dwr_k
                dwi_ref[k] += dwi_k

            back.append(lax.dot_general(dpr16[:, blk(k)], wr_ref[k].astype(BF16), NT_DIMS, preferred_element_type=F32)
                        + lax.dot_general(dpi16[:, blk(k)], wi_ref[k].astype(BF16), NT_DIMS, preferred_element_type=F32))
        dxc = dxc + jnp.concatenate(back, axis=1)
        _accum(dbr_ref, jnp.sum(dpr, axis=0, keepdims=True), first)
        _accum(dbi_ref, jnp.sum(dpi, axis=0, keepdims=True), first)
        _accum(dlam_ref, dsp * (-_sig(-lam_ref[...])), first)
        _accum(dcb_ref, jnp.sum(dxc, axis=0, keepdims=True), first)
        _accum(dcw_ref, jnp.concatenate([jnp.sum(dxc * us[3 - k], axis=0, keepdims=True) for k in range(CONV_K)], axis=0), first)
        nxt = jnp.where(first, 0.0, x_car[...])
        du = cw_ref[3:4, :] * dxc
        for s in range(1, CONV_K):
            du = du + cw_ref[3 - s:4 - s, :] * _shift_up(dxc, nxt, s)
        x_car[...] = dxc[:8]
        du_ref[...] = du.astype(BF16)

    act = jax.ShapeDtypeStruct((S, D), BF16)
    vec = jax.ShapeDtypeStruct((1, D), F32)
    wsh = jax.ShapeDtypeStruct(wr.shape, F32)
    rtile = pl.BlockSpec((tt, cb), lambda c, t: (tmap(t), c))
    rhalo = pl.BlockSpec((8, cb), lambda c, t: (jnp.maximum(tmap(t) * (tt // 8) - 1, 0), c))
    return pl.pallas_call(
        body, grid=(D // cb, nT),
        in_specs=[tile(1), halo(1), tile(2), rtile, rhalo, rtile, par(CONV_K), par(1), wblk, par(1), wblk, par(1), par(1)],
        out_specs=[rtile, rtile, par(CONV_K), par(1), wblk, par(1), wblk, par(1), par(1)],
        out_shape=[act, act, jax.ShapeDtypeStruct((CONV_K, D), F32), vec, wsh, vec, wsh, vec, vec],
        scratch_shapes=[pltpu.VMEM((8, cb), F32)] * 3,
        compiler_params=_cp(("parallel", "arbitrary")), name="lru_bwd",
    )(zbig, zbig, zbig, h, h, d_al, cw, cbias.reshape(1, D), wr, br.reshape(1, D), wi, bi.reshape(1, D), lam.reshape(1, D))


def _adamw(w, g, m, v, name):
    L, R, C = w.shape
    tm = _pick(R, max(8, ((1 << 18) // C) // 8 * 8), 8)
    c1 = 1.0 - ADAM_B1 ** ADAM_STEP
    c2 = 1.0 - ADAM_B2 ** ADAM_STEP

    def body(w_ref, g_ref, m_ref, v_ref, d_ref, mo_ref, vo_ref):
        gv = g_ref[...]
        mn = ADAM_B1 * m_ref[...] + (1.0 - ADAM_B1) * gv
        vn = ADAM_B2 * v_ref[...] + (1.0 - ADAM_B2) * (gv * gv)
        d_ref[...] = -ADAM_LR * ((mn / c1) / (jnp.sqrt(vn / c2) + ADAM_EPS) + ADAM_WD * w_ref[...])
        mo_ref[...] = mn
        vo_ref[...] = vn

    o = jax.ShapeDtypeStruct((L, R, C), F32)
    blk = pl.BlockSpec((1, tm, C), lambda l, r: (l, r, 0))
    return pl.pallas_call(
        body, grid=(L, R // tm), in_specs=[blk] * 4, out_specs=[blk] * 3, out_shape=[o] * 3,
        compiler_params=_cp(("parallel", "parallel")), name=name,
    )(w, g, m, v)


def _sum_slabs(x, name, out_dtype=F32):
    n, R, C = x.shape
    tm = _pick(R, max(8, ((1 << 18) // C) // 8 * 8), 16)

    def body(x_ref, o_ref):
        s = x_ref[0].astype(F32)
        for k in range(1, n):
            s = s + x_ref[k].astype(F32)
        o_ref[...] = s.astype(out_dtype)

    return pl.pallas_call(
        body, grid=(R // tm,), in_specs=[pl.BlockSpec((n, tm, C), lambda r: (0, r, 0))], out_specs=_rows(C)(tm),
        out_shape=jax.ShapeDtypeStruct((R, C), out_dtype), compiler_params=_cp(("parallel",)), name=name,
    )(x)


def _sum_core_halves(g, recv, core, tag):
    n, _, Rh, C = g.shape
    tm = _pick(Rh, max(16, ((1 << 18) // C) // 16 * 16), 16)

    def body(c_ref, g_ref, r_ref, o_ref):
        o_ref[0] = (g_ref[0, 0].astype(F32) + r_ref[0, 0].astype(F32)).astype(BF16)

    return pl.pallas_call(
        body,
        grid_spec=pltpu.PrefetchScalarGridSpec(
            num_scalar_prefetch=1, grid=(n, Rh // tm),
            in_specs=[pl.BlockSpec((1, 1, tm, C), lambda k, r, c_ref: (k, c_ref[0], r, 0)),
                      pl.BlockSpec((1, 1, tm, C), lambda k, r, c_ref: (k, 0, r, 0))],
            out_specs=pl.BlockSpec((1, tm, C), lambda k, r, c_ref: (k, r, 0)),
        ),
        out_shape=jax.ShapeDtypeStruct((n, Rh, C), BF16),
        compiler_params=_cp(("parallel", "parallel")), name="grad_sum_cores" + tag,
    )(core.reshape(1).astype(jnp.int32), g, recv)


ANY = pl.BlockSpec(memory_space=pl.ANY)


def _place():
    x, y, c = lax.axis_index("x"), lax.axis_index("y"), lax.axis_index("c")
    chips = [(1 - x, y), (x, 1 - y), (1 - x, 1 - y)]
    return x, y, c, chips


def _allgather_chips(shard, name):
    R, C = shard.shape
    Rh = R // 2

    def body(x_ref, out_ref, send_sems, recv_sems):
        x, y, c, chips = _place()
        me = 2 * x + y
        sibling = (x, y, 1 - c)

        def half(k, hc):
            return out_ref.at[k, pl.ds(hc * Rh, Rh), :]

        first = [pltpu.make_async_remote_copy(
            src_ref=x_ref.at[pl.ds(c * Rh, Rh), :], dst_ref=half(me, c), send_sem=send_sems.at[j], recv_sem=recv_sems.at[j],
            device_id=(*chip, c), device_id_type=MESH) for j, chip in enumerate(chips)]
        for cp in first:
            cp.start()

        def landed(j, chip, hc):
            k = 2 * chip[0] + chip[1]
            return pltpu.make_async_remote_copy(
                src_ref=half(k, hc), dst_ref=half(k, hc), send_sem=send_sems.at[j], recv_sem=recv_sems.at[j],
                device_id=sibling, device_id_type=MESH)

        passed = []
        for j, chip in enumerate(chips):
            landed(j, chip, c).wait_recv()
            cp = landed(3 + j, chip, c)
            cp.start()
            passed.append(cp)
        for j, chip in enumerate(chips):
            landed(3 + j, chip, 1 - c).wait_recv()
        for cp in first + passed:
            cp.wait_send()

    out = pl.pallas_call(
        body, in_specs=[ANY], out_specs=ANY, out_shape=jax.ShapeDtypeStruct((N_CHIPS, R, C), shard.dtype),
        scratch_shapes=[pltpu.SemaphoreType.DMA((6,)), pltpu.SemaphoreType.DMA((6,))],
        name=name,
    )(shard)
    return lax.dynamic_update_slice(out, shard[None], (2 * lax.axis_index("x") + lax.axis_index("y"), 0, 0))


def _allgather_all(blockv, name):
    R, C = blockv.shape

    def body(x_ref, out_ref, send_sems, recv_sems):
        x, y, c, chips = _place()
        sibling = (x, y, 1 - c)

        def slab(px, py, pc):
            return out_ref.at[4 * px + 2 * py + pc]

        def copy(k, block, to, src=None):
            return pltpu.make_async_remote_copy(
                src_ref=slab(*block) if src is None else src, dst_ref=slab(*block), send_sem=send_sems.at[k],
                recv_sem=recv_sems.at[k], device_id=to, device_id_type=MESH)

        first = [copy(0, (x, y, c), sibling, src=x_ref)]
        first += [copy(1 + j, (x, y, c), (*chip, c), src=x_ref) for j, chip in enumerate(chips)]
        for cp in first:
            cp.start()
        passed = [copy(4 + j, (*chip, c), sibling) for j, chip in enumerate(chips)]
        for j, chip in enumerate(chips):
            copy(1 + j, (*chip, c), (x, y, c)).wait_recv()
            passed[j].start()
        copy(0, (x, y, 1 - c), (x, y, c)).wait_recv()
        for j, chip in enumerate(chips):
            copy(4 + j, (*chip, 1 - c), (x, y, c)).wait_recv()
        for cp in first + passed:
            cp.wait_send()

    out = pl.pallas_call(
        body, in_specs=[ANY], out_specs=ANY, out_shape=jax.ShapeDtypeStruct((N_DEV, R, C), blockv.dtype),
        scratch_shapes=[pltpu.SemaphoreType.DMA((7,)), pltpu.SemaphoreType.DMA((7,))],
        name=name,
    )(blockv)
    me = 4 * lax.axis_index("x") + 2 * lax.axis_index("y") + lax.axis_index("c")
    return lax.dynamic_update_slice(out, blockv[None], (me, 0, 0))


def _swap_cores_half(g, tag):
    n, _, Rh, C = g.shape

    def body(g_ref, out_ref, send_sem, recv_sem):
        x, y, c, _ = _place()
        cp = pltpu.make_async_remote_copy(
            src_ref=g_ref.at[:, pl.ds(1 - c, 1)], dst_ref=out_ref, send_sem=send_sem, recv_sem=recv_sem,
            device_id=(x, y, 1 - c), device_id_type=MESH)
        cp.start()
        cp.wait()

    return pl.pallas_call(
        body, in_specs=[ANY], out_specs=ANY, out_shape=jax.ShapeDtypeStruct((n, 1, Rh, C), g.dtype),
        scratch_shapes=[pltpu.SemaphoreType.DMA, pltpu.SemaphoreType.DMA], name="grad_swap_cores" + tag,
    )(g)


def _alltoall_chips(s, tag):
    n, Rh, C = s.shape

    def body(s_ref, out_ref, send_sems, recv_sems):
        x, y, c, chips = _place()
        me = 2 * x + y
        sent = []
        for j, chip in enumerate(chips):
            k = 2 * chip[0] + chip[1]
            cp = pltpu.make_async_remote_copy(
                src_ref=s_ref.at[k], dst_ref=out_ref.at[me], send_sem=send_sems.at[j], recv_sem=recv_sems.at[j],
                device_id=(*chip, c), device_id_type=MESH)
            cp.start()
            sent.append(cp)
        for j, chip in enumerate(chips):
            k = 2 * chip[0] + chip[1]
            pltpu.make_async_remote_copy(
                src_ref=s_ref.at[k], dst_ref=out_ref.at[k], send_sem=send_sems.at[j], recv_sem=recv_sems.at[j],
                device_id=(*chip, c), device_id_type=MESH).wait_recv()
        for cp in sent:
            cp.wait_send()

    out = pl.pallas_call(
        body, in_specs=[ANY], out_specs=ANY, out_shape=jax.ShapeDtypeStruct((n, Rh, C), s.dtype),
        scratch_shapes=[pltpu.SemaphoreType.DMA((3,)), pltpu.SemaphoreType.DMA((3,))],
        name="grad_alltoall_chips" + tag,
    )(s)
    me = 2 * lax.axis_index("x") + lax.axis_index("y")
    return lax.dynamic_update_slice(out, lax.dynamic_slice_in_dim(s, me, 1, axis=0), (me, 0, 0))


def _join_core_halves(half, tag):
    Rh, C = half.shape

    def body(h_ref, out_ref, send_sem, recv_sem):
        x, y, c, _ = _place()
        cp = pltpu.make_async_remote_copy(
            src_ref=h_ref, dst_ref=out_ref.at[c], send_sem=send_sem, recv_sem=recv_sem,
            device_id=(x, y, 1 - c), device_id_type=MESH)
        cp.start()
        pltpu.make_async_remote_copy(
            src_ref=h_ref, dst_ref=out_ref.at[1 - c], send_sem=send_sem, recv_sem=recv_sem,
            device_id=(x, y, 1 - c), device_id_type=MESH).wait_recv()
        cp.wait_send()

    out = pl.pallas_call(
        body, in_specs=[ANY], out_specs=ANY, out_shape=jax.ShapeDtypeStruct((2, Rh, C), half.dtype),
        scratch_shapes=[pltpu.SemaphoreType.DMA, pltpu.SemaphoreType.DMA], name="grad_join_cores" + tag,
    )(half)
    return lax.dynamic_update_slice(out, half[None], (lax.axis_index("c"), 0, 0))


def _pack(arrays, cols, row_align):
    flat = jnp.concatenate([a.reshape(-1) for a in arrays])
    unit = cols * row_align
    total = -(-flat.size // unit) * unit
    return jnp.pad(flat, (0, total - flat.size)).reshape(total // cols, cols)


def _unpack(buf, shapes):
    flat = buf.reshape(-1)
    out, off = [], 0
    for shp in shapes:
        n = 1
        for d in shp:
            n *= d
        out.append(flat[off:off + n].reshape(shp))
        off += n
    return out


def _layer_fwd(x, p_l, w, tabs, dm):
    D, H, ql, kvl = dm["D"], dm["H"], dm["ql"], dm["kvl"]
    h = _rms_fwd(x, w["attn_norm"], "attn_norm_fwd")
    zbig = _mm(h, w["w_big"], "nn", "in_proj_big")
    zsm = _mm(h, w["w_sm"], "nn", "in_proj_small")
    qn, kvn, kr = _latent_fwd(zsm, w["q_a_norm"], w["kv_a_norm"], tabs, ql, kvl)
    q = _q_rope(_mm(qn, w["w_q"], "nn", "q_proj"), tabs, False, "q_rope_fwd", gain=QK_SCALE * LOG2E)
    kv = _mm(kvn, w["w_kv"], "nn", "kv_proj", out_dtype=BF16)
    o, a_mla, lse = _attn_fwd(q, kv, kr, zbig, H)
    y_mla = _mm(a_mla, w["w_o_mla"], "nn", "o_mla_proj")
    h_lru, a_lru = _lru_fwd(zbig, w["conv_w"], w["conv_b"], w["w_rg"], w["b_rg"], w["w_ig"], w["b_ig"], w["lru_lambda"], D)
    y_lru = _mm(a_lru, w["w_o_lru"], "nn", "o_lru_proj")
    merged = _merge_fwd(zbig, y_mla, y_lru, D)
    x1 = _mm(merged, w["w_out"], "nn", "out_proj", add=x)
    hp = _rms_fwd(x1, w["ple_norm"], "ple_norm_fwd")
    pg = _mm(hp, w["w_ple_gate"], "nn", "ple_gate_proj")
    pe = _mm(p_l, w["w_ple"], "nn", "ple_proj")
    x2 = _ple_fwd(x1, pe, pg)
    res = dict(x=x, h=h, zbig=zbig, zsm=zsm, qn=qn, kvn=kvn, kr=kr, q=q, kv=kv, o=o, a_mla=a_mla, lse=lse, y_mla=y_mla,
               h_lru=h_lru, a_lru=a_lru, y_lru=y_lru, merged=merged, x1=x1, hp=hp, pg=pg, pe=pe, p=p_l)
    return x2, res


def _layer_bwd(dx2, r, w, tabs, dm):
    D, H, ql, kvl = dm["D"], dm["H"], dm["ql"], dm["kvl"]
    S = dx2.shape[0]
    g = {}
    d_pe, d_pg = _ple_bwd(dx2, r["pe"], r["pg"])
    g["w_ple"] = _mm(r["p"], d_pe, "tn", "ple_proj_dw", out_dtype=BF16)
    g["w_ple_gate"] = _mm(r["hp"], d_pg, "tn", "ple_gate_dw", out_dtype=BF16)
    d_hp = _mm(d_pg, w["w_ple_gate"], "nt", "ple_gate_dx")
    dx1, g["ple_norm"] = _rms_bwd(r["x1"], w["ple_norm"], d_hp, dx2, "ple_norm_bwd")
    g["w_out"] = _mm(r["merged"], dx1, "tn", "out_proj_dw", out_dtype=BF16)
    d_merged = _mm(dx1, w["w_out"], "nt", "out_proj_dx")
    d_ym, d_yl, d_mm, d_ml = _merge_bwd(r["zbig"], r["y_mla"], r["y_lru"], d_merged, D)
    g["w_o_mla"] = _mm(r["a_mla"], d_ym, "tn", "o_mla_dw", out_dtype=BF16)
    d_a_mla = _mm(d_ym, w["w_o_mla"], "nt", "o_mla_dx")
    g["w_o_lru"] = _mm(r["a_lru"], d_yl, "tn", "o_lru_dw", out_dtype=BF16)
    d_a_lru = _mm(d_yl, w["w_o_lru"], "nt", "o_lru_dx")
    d_o, d_gm, delta = _attn_gate_bwd(d_a_mla, r["o"], r["zbig"], H)
    dq, dkv, dkr = _attn_bwd(r["q"], r["kv"], r["kr"], d_o, r["lse"][:, :, 0].reshape(H, 1, S), delta[:, :, 0].reshape(H, 1, S), H)
    dq_pre = _q_rope(dq, tabs, True, "q_rope_bwd", gain=QK_SCALE)
    g["w_q"] = _mm(r["qn"], dq_pre, "tn", "q_proj_dw", out_dtype=BF16)
    d_qn = _mm(dq_pre, w["w_q"], "nt", "q_proj_dx")
    g["w_kv"] = _mm(r["kvn"], dkv, "tn", "kv_proj_dw", out_dtype=BF16)
    d_kvn = _mm(dkv, w["w_kv"], "nt", "kv_proj_dx")
    dzsm, g["q_a_norm"], g["kv_a_norm"] = _latent_bwd(r["zsm"], w["q_a_norm"], w["kv_a_norm"], tabs, d_qn, d_kvn, dkr, ql, kvl)
    (d_u, d_gl, g["conv_w"], g["conv_b"], g["w_rg"], g["b_rg"], g["w_ig"], g["b_ig"], g["lru_lambda"]) = _lru_bwd(
        r["zbig"], r["h_lru"], d_a_lru, w["conv_w"], w["conv_b"], w["w_rg"], w["b_rg"], w["w_ig"], w["b_ig"], w["lru_lambda"], D)
    dzbig = jnp.concatenate([d_gm, d_u, d_gl, d_mm, d_ml], axis=1)
    g["w_big"] = _mm(r["h"], dzbig, "tn", "in_proj_big_dw", out_dtype=BF16)
    g["w_sm"] = _mm(r["h"], dzsm, "tn", "in_proj_small_dw", out_dtype=BF16)
    dh = _mm(dzbig, w["w_big"], "nt", "in_proj_big_dx")
    dh = _mm(dzsm, w["w_sm"], "nt", "in_proj_small_dx", add=dh)
    dx, g["attn_norm"] = _rms_bwd(r["x"], w["attn_norm"], dh, dx1, "attn_norm_bwd")
    return dx, g


SHARDED = ("w_in", "w_q_b", "w_kv_b", "w_o_mla", "w_o_lru", "w_out", "w_ple_gate", "w_ple")
COL_SHARDED = ("w_in", "w_q_b", "w_kv_b", "w_ple")
ROWED = ("w_o_mla", "w_o_lru", "w_out", "w_ple_gate")
FLAT = ("w_q_b", "w_kv_b", "w_ple")
REPLICATED = ("attn_norm", "q_a_norm", "kv_a_norm", "conv_b", "w_rg", "b_rg", "w_ig", "b_ig", "lru_lambda", "ple_norm", "final_norm")
WEIGHTS = ("attn_norm", "w_in", "q_a_norm", "w_q_b", "kv_a_norm", "w_kv_b", "conv_w", "conv_b", "w_rg", "b_rg", "w_ig", "b_ig",
           "lru_lambda", "w_o_mla", "w_o_lru", "w_out", "ple_norm", "w_ple_gate", "w_ple", "final_norm")


def kernel(x, p, positions, attn_norm, w_in, q_a_norm, w_q_b, kv_a_norm, w_kv_b, conv_w, conv_b, w_rg, b_rg, w_ig, b_ig, lru_lambda, w_o_mla, w_o_lru, w_out, ple_norm, w_ple_gate, w_ple, final_norm, loss_target, m_attn_norm, m_w_in, m_q_a_norm, m_w_q_b, m_kv_a_norm, m_w_kv_b, m_conv_w, m_conv_b, m_w_rg, m_b_rg, m_w_ig, m_b_ig, m_lru_lambda, m_w_o_mla, m_w_o_lru, m_w_out, m_ple_norm, m_w_ple_gate, m_w_ple, m_final_norm, v_attn_norm, v_w_in, v_q_a_norm, v_w_q_b, v_kv_a_norm, v_w_kv_b, v_conv_w, v_conv_b, v_w_rg, v_b_rg, v_w_ig, v_b_ig, v_lru_lambda, v_w_o_mla, v_w_o_lru, v_w_out, v_ple_norm, v_w_ple_gate, v_w_ple, v_final_norm):
    W = dict(attn_norm=attn_norm, w_in=w_in, q_a_norm=q_a_norm, w_q_b=w_q_b, kv_a_norm=kv_a_norm, w_kv_b=w_kv_b, conv_w=conv_w,
             conv_b=conv_b, w_rg=w_rg, b_rg=b_rg, w_ig=w_ig, b_ig=b_ig, lru_lambda=lru_lambda, w_o_mla=w_o_mla, w_o_lru=w_o_lru,
             w_out=w_out, ple_norm=ple_norm, w_ple_gate=w_ple_gate, w_ple=w_ple, final_norm=final_norm)
    M = dict(attn_norm=m_attn_norm, w_in=m_w_in, q_a_norm=m_q_a_norm, w_q_b=m_w_q_b, kv_a_norm=m_kv_a_norm, w_kv_b=m_w_kv_b,
             conv_w=m_conv_w, conv_b=m_conv_b, w_rg=m_w_rg, b_rg=m_b_rg, w_ig=m_w_ig, b_ig=m_b_ig, lru_lambda=m_lru_lambda,
             w_o_mla=m_w_o_mla, w_o_lru=m_w_o_lru, w_out=m_w_out, ple_norm=m_ple_norm, w_ple_gate=m_w_ple_gate, w_ple=m_w_ple,
             final_norm=m_final_norm)
    V = dict(attn_norm=v_attn_norm, w_in=v_w_in, q_a_norm=v_q_a_norm, w_q_b=v_w_q_b, kv_a_norm=v_kv_a_norm, w_kv_b=v_w_kv_b,
             conv_w=v_conv_w, conv_b=v_conv_b, w_rg=v_w_rg, b_rg=v_b_rg, w_ig=v_w_ig, b_ig=v_b_ig, lru_lambda=v_lru_lambda,
             w_o_mla=v_w_o_mla, w_o_lru=v_w_o_lru, w_out=v_w_out, ple_norm=v_ple_norm, w_ple_gate=v_w_ple_gate, w_ple=v_w_ple,
             final_norm=v_final_norm)
    depth = attn_norm.shape[0]
    S, D = x.shape[1], x.shape[2]
    ql, kvl = q_a_norm.shape[1], kv_a_norm.shape[1]
    H = w_q_b.shape[2] * N_CHIPS // (QK_NOPE + QK_ROPE)
    dm = dict(D=D, H=H, ql=ql, kvl=kvl)
    chip = 2 * lax.axis_index("x") + lax.axis_index("y")
    core = lax.axis_index("c")

    def rest_rows(get):
        rows = [a.reshape(-1, D) for n in ROWED for a in get(n)] + [_pack(get(n), D, 16) for n in FLAT]
        fill = -sum(r.shape[0] for r in rows) % PACK_ROWS
        return rows + ([jnp.zeros((fill, D), rows[0].dtype)] if fill else [])

    def unpack_rest(buf):
        out, off = {}, 0
        for n in ROWED + FLAT:
            rows = -(-W[n].size // (D * 16)) * 16
            part = buf[off:off + rows]
            out[n] = part.reshape(W[n].shape) if n in ROWED else _unpack(part, [W[n].shape])[0]
            off += rows
        return out

    cin = w_in.shape[2]
    got_in = _allgather_chips(w_in.astype(BF16).reshape(depth * D, cin), "w_in_allgather")
    got_rest = _allgather_chips(jnp.concatenate(rest_rows(lambda n: [W[n].astype(BF16)]), axis=0), "weights_allgather")
    slabs = [dict(unpack_rest(got_rest[k]), w_in=got_in[k].reshape(depth, D, cin)) for k in range(N_CHIPS)]
    cw_all = _allgather_chips(_pack([conv_w], LANES, 16), "conv_w_allgather")
    conv_w_full = jnp.concatenate([_unpack(cw_all[k], [conv_w.shape])[0] for k in range(N_CHIPS)], axis=-1)

    n_small = ql + kvl + QK_ROPE
    hpc = H // N_CHIPS
    head_pad = ((0, 0), (0, 0), (0, HEAD_PAD - QK_NOPE - QK_ROPE))
    layers = []
    for l in range(depth):
        cat = lambda n, axis: jnp.concatenate([s[n][l] for s in slabs], axis=axis)
        layers.append(dict(
            w_big=jnp.concatenate([slabs[0]["w_in"][l][:, n_small:]] + [s["w_in"][l] for s in slabs[1:]], axis=1),
            w_sm=jnp.pad(slabs[0]["w_in"][l][:, :n_small], ((0, 0), (0, LANES - QK_ROPE))),
            w_q=jnp.concatenate([jnp.pad(s["w_q_b"][l].reshape(ql, hpc, QK_NOPE + QK_ROPE), head_pad).reshape(ql, hpc * HEAD_PAD)
                                 for s in slabs], axis=1),
            w_kv=cat("w_kv_b", 1), w_o_mla=cat("w_o_mla", 0), w_o_lru=cat("w_o_lru", 0), w_out=cat("w_out", 0),
            w_ple_gate=cat("w_ple_gate", 0), w_ple=cat("w_ple", 1), conv_w=conv_w_full[l],
            **{n: W[n][l] for n in REPLICATED if n != "final_norm"}))

    inv_freq = ROPE_THETA ** (-jnp.arange(0, QK_ROPE, 2, dtype=F32) / QK_ROPE)
    tabs = _rope_tables(positions[0], inv_freq)

    xs = x[0]
    saved = []
    for l in range(depth):
        xs, res = _layer_fwd(xs, p[l, 0], layers[l], tabs, dm)
        saved.append(res)
    dx, g_final_norm, loss_part = _loss_head(xs, final_norm, loss_target[0])
    grads = [None] * depth
    for l in reversed(range(depth)):
        dx, grads[l] = _layer_bwd(dx, saved[l], layers[l], tabs, dm)

    def stack(name):
        return jnp.stack([grads[l][name] for l in range(depth)])

    def shard_of(l, n, k):
        g = grads[l]
        if n == "w_in":
            lo, hi = k * W[n].shape[2], (k + 1) * W[n].shape[2]
            parts = ([g["w_sm"][:, lo:min(hi, n_small)]] if lo < n_small else []) + (
                [g["w_big"][:, max(lo, n_small) - n_small:hi - n_small]] if hi > n_small else [])
            return jnp.concatenate(parts, axis=1)
        if n == "w_q_b":
            return g["w_q"].reshape(ql, H, HEAD_PAD)[:, k * hpc:(k + 1) * hpc, :QK_NOPE + QK_ROPE].reshape(ql, -1)
        mine = {"w_kv_b": "w_kv"}.get(n, n)
        if n in COL_SHARDED:
            return g[mine][:, k * W[n].shape[2]:(k + 1) * W[n].shape[2]]
        return g[mine][k * W[n].shape[1]:(k + 1) * W[n].shape[1], :]

    def reduce_scatter(slab_rows, tag):
        _, R, C = slab_rows.shape
        gp = slab_rows.reshape(N_CHIPS, 2, R // 2, C)
        core_sum = _sum_core_halves(gp, _swap_cores_half(gp, tag), core, tag)
        chip_sum = _sum_slabs(_alltoall_chips(core_sum, tag), "grad_sum_chips" + tag)
        return _join_core_halves(chip_sum, tag).reshape(R, C)

    in_rows = [shard_of(l, "w_in", k) for k in range(N_CHIPS) for l in range(depth)]
    g_in = reduce_scatter(jnp.concatenate(in_rows, axis=0).reshape(N_CHIPS, depth * D, cin), "_w_in")
    rest = []
    for k in range(N_CHIPS):
        rest += rest_rows(lambda n: [shard_of(l, n, k) for l in range(depth)])
    g_rest = reduce_scatter(jnp.concatenate(rest, axis=0).reshape(N_CHIPS, -1, D), "_rest")
    G = dict(unpack_rest(g_rest), w_in=g_in.reshape(w_in.shape))

    rep_shapes = [W[n].shape for n in REPLICATED] + [(depth, CONV_K, D), (LANES,)]
    rep = [stack(n).reshape(W[n].shape) for n in REPLICATED if n != "final_norm"]
    rep += [g_final_norm.reshape(D), stack("conv_w"), loss_part.reshape(LANES)]
    rep_sum = _unpack(_sum_slabs(_allgather_all(_pack(rep, LANES, SMALL_ROWS), "small_grads_allgather"), "small_grads_sum"), rep_shapes)
    for n, gv in zip(REPLICATED, rep_sum):
        G[n] = gv
    cshard = D // N_CHIPS
    G["conv_w"] = lax.dynamic_slice_in_dim(rep_sum[-2], chip * cshard, cshard, axis=2)
    loss = rep_sum[-1][0]

    small = REPLICATED + ("conv_w",)
    small_shapes = [W[n].shape for n in small]
    pk = lambda src: _pack([src[n] for n in small], LANES, SMALL_ROWS)[None]
    upd = _adamw(pk(W), pk(G), pk(M), pk(V), "adamw_small")
    delta, new_m, new_v = ({n: a for n, a in zip(small, _unpack(u, small_shapes))} for u in upd)
    for n in SHARDED:
        delta[n], new_m[n], new_v[n] = _adamw(W[n], G[n], M[n], V[n], "adamw_" + n)

    return (loss, dx.reshape(x.shape), *[G[n] for n in WEIGHTS], *[delta[n] for n in WEIGHTS],
            *[new_m[n] for n in WEIGHTS], *[new_v[n] for n in WEIGHTS])
```

```python
import jax
import jax.numpy as jnp
from jax import lax
from jax.experimental import pallas as pl
from jax.experimental.pallas import tpu as pltpu

F32 = jnp.float32
BF16 = jnp.bfloat16
MESH = pl.DeviceIdType.MESH

CHUNK = 64
QK_NOPE = 128
QK_ROPE = 64
V_HEAD = 128
ROPE_THETA = 10000.0
CONV_K = 4
LRU_C = 8.0
LRU_BLOCK_DIM = 128
EPS = 1e-6
ADAM_LR = 0.001
ADAM_B1 = 0.9
ADAM_B2 = 0.999
ADAM_EPS = 1e-08
ADAM_WD = 0.01
ADAM_STEP = 10

N_CHIPS = 4
N_DEV = 8
LANES = 128
HEAD_PAD = 256
VMEM_LIMIT = 48 * 1024 * 1024

MM_TILE_BYTES = 8 * 1024 * 1024
ATT_T = 512
ROW_T = 256
LRU_TT = 512
LRU_CB = 512
PACK_ROWS = 1024
SMALL_ROWS = 512

NT_DIMS = (((1,), (1,)), ((), ()))


def _cp(sem):
    return pltpu.CompilerParams(dimension_semantics=sem, vmem_limit_bytes=VMEM_LIMIT)


def _pick(n, pref, align=LANES):
    if n <= pref:
        return n
    t = pref - pref % align
    while t >= align:
        if n % t == 0:
            return t
        t -= align
    return n


def _sig(x):
    return 1.0 / (1.0 + jnp.exp(-x))


def _mm(a, b, mode, name, out_dtype=F32, add=None, tm=None, tn=1024, tk=2048):
    if mode == "nn":
        (M, K), (_, N) = a.shape, b.shape
    elif mode == "nt":
        (M, K), (N, _) = a.shape, b.shape
    else:
        (K, M), (_, N) = a.shape, b.shape
    if tm is None:
        tm = 1024 if a.dtype.itemsize == 2 and b.dtype.itemsize == 2 else 512
    tm, tn = _pick(M, tm), _pick(N, tn)
    while tk > 512 and tk * (tm * a.dtype.itemsize + tn * b.dtype.itemsize) > MM_TILE_BYTES:
        tk //= 2
    tk = _pick(K, tk)
    nm, nn, nk = M // tm, N // tn, K // tk
    i_outer = nm * b.size * b.dtype.itemsize <= nn * a.size * a.dtype.itemsize

    def ij(g0, g1):
        return (g0, g1) if i_outer else (g1, g0)

    def amap(g0, g1, k):
        i, _ = ij(g0, g1)
        return (k, i) if mode == "tn" else (i, k)

    def bmap(g0, g1, k):
        _, j = ij(g0, g1)
        return (j, k) if mode == "nt" else (k, j)

    def omap(g0, g1, k):
        return ij(g0, g1)

    ablk = (tk, tm) if mode == "tn" else (tm, tk)
    bblk = (tn, tk) if mode == "nt" else (tk, tn)

    def body(*refs):
        if add is None:
            a_ref, b_ref, o_ref = refs[:3]
            add_ref = None
        else:
            a_ref, b_ref, add_ref, o_ref = refs[:4]
        x = a_ref[...].astype(BF16)
        y = b_ref[...].astype(BF16)
        if mode == "nn":
            p = jnp.dot(x, y, preferred_element_type=F32)
        elif mode == "nt":
            p = lax.dot_general(x, y, NT_DIMS, preferred_element_type=F32)
        else:
            p = jnp.dot(x.T, y, preferred_element_type=F32)
        if nk == 1:
            if add_ref is not None:
                p = p + add_ref[...]
            o_ref[...] = p.astype(out_dtype)
        else:
            acc = refs[-1]
            k = pl.program_id(2)

            @pl.when(k == 0)
            def _():
                acc[...] = p if add_ref is None else p + add_ref[...]

            @pl.when(k > 0)
            def _():
                acc[...] += p

            @pl.when(k == nk - 1)
            def _():
                o_ref[...] = acc[...].astype(out_dtype)

    in_specs = [pl.BlockSpec(ablk, amap), pl.BlockSpec(bblk, bmap)]
    args = [a, b]
    if add is not None:
        in_specs.append(pl.BlockSpec((tm, tn), omap))
        args.append(add)
    grid = (nm, nn, nk) if i_outer else (nn, nm, nk)
    return pl.pallas_call(
        body, grid=grid, in_specs=in_specs, out_specs=pl.BlockSpec((tm, tn), omap),
        out_shape=jax.ShapeDtypeStruct((M, N), out_dtype),
        scratch_shapes=[pltpu.VMEM((tm, tn), F32)] if nk > 1 else [],
        compiler_params=_cp(("parallel", "parallel", "arbitrary")), name=name,
    )(*args)


def _rows(cols, i=0):
    def make(tm):
        return pl.BlockSpec((tm, cols), lambda r: (r, i))
    return make


def _par(cols):
    return pl.BlockSpec((1, cols), lambda r: (0, 0))


def _rms_fwd(x, g, name):
    S, D = x.shape
    tm = _pick(S, ROW_T, 8)

    def body(x_ref, g_ref, o_ref):
        xv = x_ref[...]
        rs = lax.rsqrt(jnp.mean(xv * xv, axis=-1, keepdims=True) + EPS)
        o_ref[...] = (xv * rs * g_ref[...]).astype(BF16)

    return pl.pallas_call(
        body, grid=(S // tm,), in_specs=[_rows(D)(tm), _par(D)], out_specs=_rows(D)(tm),
        out_shape=jax.ShapeDtypeStruct((S, D), BF16), compiler_params=_cp(("parallel",)), name=name,
    )(x, g.reshape(1, D))


def _rms_bwd_math(xv, g, dy):
    rs = lax.rsqrt(jnp.mean(xv * xv, axis=-1, keepdims=True) + EPS)
    xh = xv * rs
    dg = jnp.sum(dy * xh, axis=0, keepdims=True)
    dyg = dy * g
    dx = rs * (dyg - xh * jnp.mean(dyg * xh, axis=-1, keepdims=True))
    return dx, dg


def _accum(ref, val, first):
    @pl.when(first)
    def _():
        ref[...] = val

    @pl.when(jnp.logical_not(first))
    def _():
        ref[...] += val


def _rms_bwd(x, g, dy, dres, name, with_bf16):
    S, D = x.shape
    tm = _pick(S, ROW_T, 8)

    def body(x_ref, g_ref, dy_ref, dres_ref, dx_ref, *rest):
        dx, dg = _rms_bwd_math(x_ref[...], g_ref[...], dy_ref[...])
        dx = dres_ref[...] + dx
        dx_ref[...] = dx
        if with_bf16:
            rest[0][...] = dx.astype(BF16)
        _accum(rest[-1], dg, pl.program_id(0) == 0)

    extra = [jax.ShapeDtypeStruct((S, D), BF16)] if with_bf16 else []
    return pl.pallas_call(
        body, grid=(S // tm,), in_specs=[_rows(D)(tm), _par(D), _rows(D)(tm), _rows(D)(tm)],
        out_specs=[_rows(D)(tm)] * (1 + len(extra)) + [_par(D)],
        out_shape=[jax.ShapeDtypeStruct((S, D), F32)] + extra + [jax.ShapeDtypeStruct((1, D), F32)],
        compiler_params=_cp(("arbitrary",)), name=name,
    )(x, g.reshape(1, D), dy, dres)


def _rope_tables(pos, inv_freq):
    S = pos.shape[0]
    tm = _pick(S, 512, 8)
    half = QK_ROPE // 2
    invf = jnp.concatenate([inv_freq, inv_freq, jnp.zeros((LANES - QK_ROPE,), F32)]).reshape(1, LANES)

    def body(pos_ref, f_ref, c_ref, sa_ref, sb_ref):
        ang = pos_ref[...].astype(F32) * f_ref[...]
        lane = lax.broadcasted_iota(jnp.int32, ang.shape, 1)
        c, s = jnp.cos(ang), jnp.sin(ang)
        c_ref[...] = jnp.where(lane < QK_ROPE, c, 0.0)
        sa_ref[...] = jnp.where(lane < half, -s, 0.0)
        sb_ref[...] = jnp.where((lane >= half) & (lane < QK_ROPE), s, 0.0)

    tab = jax.ShapeDtypeStruct((S, LANES), F32)
    return pl.pallas_call(
        body, grid=(S // tm,), in_specs=[pl.BlockSpec((tm, 1), lambda r: (r, 0)), _par(LANES)],
        out_specs=[_rows(LANES)(tm)] * 3, out_shape=[tab] * 3, compiler_params=_cp(("parallel",)), name="rope_tables",
    )(pos.reshape(S, 1), invf)


def _rope(x, c, sa, sb):
    return x * c + pltpu.roll(x, LANES - QK_ROPE // 2, 1) * sa + pltpu.roll(x, QK_ROPE // 2, 1) * sb


def _rope_t(d, c, sa, sb):
    return d * c + pltpu.roll(d * sa, QK_ROPE // 2, 1) + pltpu.roll(d * sb, LANES - QK_ROPE // 2, 1)


def _latent_fwd(zsm, gq, gkv, tabs, ql, kvl):
    S = zsm.shape[0]
    tm = _pick(S, ROW_T, 8)
    kr_blk = (ql + kvl) // LANES

    def body(q_ref, kv_ref, kr_ref, gq_ref, gkv_ref, c_ref, sa_ref, sb_ref, qn_ref, kvn_ref, kro_ref):
        for src, g_ref, dst in ((q_ref, gq_ref, qn_ref), (kv_ref, gkv_ref, kvn_ref)):
            v = src[...]
            rs = lax.rsqrt(jnp.mean(v * v, axis=-1, keepdims=True) + EPS)
            dst[...] = (v * rs * g_ref[...]).astype(BF16)
        kro_ref[...] = _rope(kr_ref[...], c_ref[...], sa_ref[...], sb_ref[...]).astype(BF16)

    return pl.pallas_call(
        body, grid=(S // tm,),
        in_specs=[_rows(ql, 0)(tm), _rows(kvl, 1)(tm), _rows(LANES, kr_blk)(tm), _par(ql), _par(kvl)] + [_rows(LANES)(tm)] * 3,
        out_specs=[_rows(ql)(tm), _rows(kvl)(tm), _rows(LANES)(tm)],
        out_shape=[jax.ShapeDtypeStruct((S, ql), BF16), jax.ShapeDtypeStruct((S, kvl), BF16), jax.ShapeDtypeStruct((S, LANES), BF16)],
        compiler_params=_cp(("parallel",)), name="latent_fwd",
    )(zsm, zsm, zsm, gq.reshape(1, ql), gkv.reshape(1, kvl), *tabs)


def _latent_bwd(zsm, gq, gkv, tabs, d_qn, d_kvn, dkr, ql, kvl):
    S, W = zsm.shape
    H = dkr.shape[0]
    tm = _pick(S, ROW_T, 8)
    kr_blk = (ql + kvl) // LANES

    def body(q_ref, kv_ref, gq_ref, gkv_ref, c_ref, sa_ref, sb_ref, dqn_ref, dkvn_ref, dkr_ref, dz_ref, dgq_ref, dgkv_ref):
        first = pl.program_id(0) == 0
        dq, dgq = _rms_bwd_math(q_ref[...], gq_ref[...], dqn_ref[...])
        dkv, dgkv = _rms_bwd_math(kv_ref[...], gkv_ref[...], dkvn_ref[...])
        dk = dkr_ref[0]
        for h in range(1, H):
            dk = dk + dkr_ref[h]
        dz_ref[:, 0:ql] = dq.astype(BF16)
        dz_ref[:, ql:ql + kvl] = dkv.astype(BF16)
        dz_ref[:, ql + kvl:] = _rope_t(dk, c_ref[...], sa_ref[...], sb_ref[...]).astype(BF16)
        _accum(dgq_ref, dgq, first)
        _accum(dgkv_ref, dgkv, first)

    return pl.pallas_call(
        body, grid=(S // tm,),
        in_specs=[_rows(ql, 0)(tm), _rows(kvl, 1)(tm), _par(ql), _par(kvl)] + [_rows(LANES)(tm)] * 3
        + [_rows(ql)(tm), _rows(kvl)(tm), pl.BlockSpec((H, tm, LANES), lambda r: (0, r, 0))],
        out_specs=[_rows(W)(tm), _par(ql), _par(kvl)],
        out_shape=[jax.ShapeDtypeStruct((S, W), BF16), jax.ShapeDtypeStruct((1, ql), F32), jax.ShapeDtypeStruct((1, kvl), F32)],
        compiler_params=_cp(("arbitrary",)), name="latent_bwd",
    )(zsm, zsm, gq.reshape(1, ql), gkv.reshape(1, kvl), *tabs, d_qn, d_kvn, dkr)


def _q_rope(q, tabs, transpose, name, gain=1.0):
    S, W = q.shape
    H = W // HEAD_PAD
    tm = _pick(S, ROW_T, 8)
    fn = _rope_t if transpose else _rope

    def body(q_ref, c_ref, sa_ref, sb_ref, o_ref):
        c, sa, sb = c_ref[...], sa_ref[...], sb_ref[...]
        if gain != 1.0:
            c, sa, sb = c * gain, sa * gain, sb * gain
        for h in range(H):
            lo = h * HEAD_PAD
            nope = q_ref[:, lo:lo + QK_NOPE]
            o_ref[:, lo:lo + QK_NOPE] = (nope if gain == 1.0 else nope * gain).astype(BF16)
            o_ref[:, lo + QK_NOPE:lo + HEAD_PAD] = fn(q_ref[:, lo + QK_NOPE:lo + HEAD_PAD], c, sa, sb).astype(BF16)

    return pl.pallas_call(
        body, grid=(S // tm,), in_specs=[_rows(W)(tm)] + [_rows(LANES)(tm)] * 3, out_specs=_rows(W)(tm),
        out_shape=jax.ShapeDtypeStruct((S, W), BF16), compiler_params=_cp(("parallel",)), name=name,
    )(q, *tabs)


def _merge_fwd(zbig, y_mla, y_lru, D):
    S = y_mla.shape[0]
    tm = _pick(S, ROW_T, 8)

    def body(mm_ref, ml_ref, ym_ref, yl_ref, o_ref):
        o_ref[...] = (_sig(mm_ref[...]) * ym_ref[...] + _sig(ml_ref[...]) * yl_ref[...]).astype(BF16)

    return pl.pallas_call(
        body, grid=(S // tm,), in_specs=[_rows(D, 3)(tm), _rows(D, 4)(tm), _rows(D)(tm), _rows(D)(tm)], out_specs=_rows(D)(tm),
        out_shape=jax.ShapeDtypeStruct((S, D), BF16), compiler_params=_cp(("parallel",)), name="merge_fwd",
    )(zbig, zbig, y_mla, y_lru)


def _merge_bwd(zbig, y_mla, y_lru, d_merged, D):
    S = y_mla.shape[0]
    tm = _pick(S, ROW_T, 8)

    def body(mm_ref, ml_ref, ym_ref, yl_ref, d_ref, dym_ref, dyl_ref, dmm_ref, dml_ref):
        d = d_ref[...]
        sm, sl = _sig(mm_ref[...]), _sig(ml_ref[...])
        dym_ref[...] = (d * sm).astype(BF16)
        dyl_ref[...] = (d * sl).astype(BF16)
        dmm_ref[...] = (d * ym_ref[...] * sm * (1.0 - sm)).astype(BF16)
        dml_ref[...] = (d * yl_ref[...] * sl * (1.0 - sl)).astype(BF16)

    o = jax.ShapeDtypeStruct((S, D), BF16)
    return pl.pallas_call(
        body, grid=(S // tm,), in_specs=[_rows(D, 3)(tm), _rows(D, 4)(tm)] + [_rows(D)(tm)] * 3, out_specs=[_rows(D)(tm)] * 4,
        out_shape=[o] * 4, compiler_params=_cp(("parallel",)), name="merge_bwd",
    )(zbig, zbig, y_mla, y_lru, d_merged)


def _ple_fwd(x1, pe, pg):
    S, D = x1.shape
    tm = _pick(S, ROW_T, 8)

    def body(x_ref, pe_ref, pg_ref, o_ref):
        o_ref[...] = x_ref[...] + pe_ref[...] * _sig(pg_ref[...])

    return pl.pallas_call(
        body, grid=(S // tm,), in_specs=[_rows(D)(tm)] * 3, out_specs=_rows(D)(tm),
        out_shape=jax.ShapeDtypeStruct((S, D), F32), compiler_params=_cp(("parallel",)), name="ple_fwd",
    )(x1, pe, pg)


def _ple_bwd(dx2, pe, pg):
    S, D = dx2.shape
    tm = _pick(S, ROW_T, 8)

    def body(d_ref, pe_ref, pg_ref, dpe_ref, dpg_ref):
        d = d_ref[...]
        s = _sig(pg_ref[...])
        dpe_ref[...] = (d * s).astype(BF16)
        dpg_ref[...] = (d * pe_ref[...] * s * (1.0 - s)).astype(BF16)

    o = jax.ShapeDtypeStruct((S, D), BF16)
    return pl.pallas_call(
        body, grid=(S // tm,), in_specs=[_rows(D)(tm)] * 3, out_specs=[_rows(D)(tm)] * 2, out_shape=[o] * 2,
        compiler_params=_cp(("parallel",)), name="ple_bwd",
    )(dx2, pe, pg)


def _loss_head(x, g, target):
    S, D = x.shape
    tm = _pick(S, ROW_T, 8)

    def body(x_ref, g_ref, t_ref, dx_ref, dg_ref, loss_ref):
        first = pl.program_id(0) == 0
        xv, gv = x_ref[...], g_ref[...]
        rs = lax.rsqrt(jnp.mean(xv * xv, axis=-1, keepdims=True) + EPS)
        e = xv * rs * gv - t_ref[...]
        part = 0.5 * jnp.sum(jnp.mean(e * e, axis=-1, keepdims=True), axis=0, keepdims=True)
        dx, dg = _rms_bwd_math(xv, gv, e * (1.0 / D))
        dx_ref[...] = dx
        _accum(dg_ref, dg, first)
        _accum(loss_ref, jnp.broadcast_to(part, (1, LANES)), first)

    return pl.pallas_call(
        body, grid=(S // tm,), in_specs=[_rows(D)(tm), _par(D), _rows(D)(tm)], out_specs=[_rows(D)(tm), _par(D), _par(LANES)],
        out_shape=[jax.ShapeDtypeStruct((S, D), F32), jax.ShapeDtypeStruct((1, D), F32), jax.ShapeDtypeStruct((1, LANES), F32)],
        compiler_params=_cp(("arbitrary",)), name="loss_head",
    )(x, g.reshape(1, D), target)


def _attn_gate_bwd(d_a, o, zbig, H):
    S, W = o.shape
    tm = _pick(S, ROW_T, 8)

    def body(da_ref, o_ref, g_ref, do_ref, dg_ref, dl_ref):
        da, ov, g = da_ref[...], o_ref[...], g_ref[...]
        s = _sig(g)
        do = da * g * s
        do_ref[...] = do.astype(BF16)
        dg_ref[...] = (da * ov * s * (1.0 + g * (1.0 - s))).astype(BF16)
        prod = do * ov
        for h in range(H):
            r = jnp.sum(prod[:, h * V_HEAD:(h + 1) * V_HEAD], axis=-1, keepdims=True)
            dl_ref[h] = jnp.broadcast_to(r, (tm, LANES))

    return pl.pallas_call(
        body, grid=(S // tm,), in_specs=[_rows(W)(tm), _rows(W)(tm), _rows(W, 0)(tm)],
        out_specs=[_rows(W)(tm), _rows(W)(tm), pl.BlockSpec((H, tm, LANES), lambda r: (0, r, 0))],
        out_shape=[jax.ShapeDtypeStruct((S, W), BF16), jax.ShapeDtypeStruct((S, W), BF16), jax.ShapeDtypeStruct((H, S, LANES), F32)],
        compiler_params=_cp(("parallel",)), name="attn_gate_bwd",
    )(d_a, o, zbig)


def _chunk_mask(ts, t, lo, q_rows):
    r = (lax.broadcasted_iota(jnp.int32, (ts, t), 0) + lo) // CHUNK
    c = lax.broadcasted_iota(jnp.int32, (ts, t), 1) // CHUNK
    return (c <= r) if q_rows else (r <= c)


QK_SCALE = 1.0 / (QK_NOPE + QK_ROPE) ** 0.5
LOG2E = 1.4426950408889634


def _attn_fwd(q, kv, kr, zbig, H):
    S = q.shape[0]
    t = _pick(S, ATT_T)
    nq = S // t

    def body(q_ref, kn_ref, v_ref, kr_ref, g_ref, o_ref, a_ref, lse_ref, m_scr, l_scr, acc_scr, s0_scr, s1_scr, p0_scr, p1_scr):
        i = pl.program_id(1)
        m_scr[...] = jnp.full((t, LANES), -1e30, F32)
        l_scr[...] = jnp.zeros((t, LANES), F32)
        acc_scr[...] = jnp.zeros((t, V_HEAD), F32)

        s_buf, p_buf = (s0_scr, s1_scr), (p0_scr, p1_scr)

        def logits(j, slot):
            ks = pl.multiple_of(j * t, t)
            k = jnp.concatenate([kn_ref[pl.ds(ks, t), :], kr_ref[pl.ds(ks, t), :]], axis=1)
            s_buf[slot][...] = lax.dot_general(q_ref[...], k, NT_DIMS, preferred_element_type=F32)

        def values(j, slot):
            ks = pl.multiple_of(jnp.maximum(j, 0) * t, t)
            acc_scr[...] += jnp.dot(p_buf[slot][...], v_ref[pl.ds(ks, t), :], preferred_element_type=F32)

        def softmax(slot, masked):
            s = s_buf[slot][...]
            if masked:
                s = jnp.where(_chunk_mask(t, t, 0, True), s, -1e30)
            m_prev = m_scr[...]
            m_next = jnp.maximum(m_prev, jnp.max(s, axis=1, keepdims=True))
            p = jnp.exp2(s - jnp.tile(m_next, (1, t // LANES)))
            alpha = jnp.exp2(m_prev - m_next)
            l_scr[...] = alpha * l_scr[...] + jnp.sum(p, axis=1, keepdims=True)
            m_scr[...] = m_next
            acc_scr[...] = acc_scr[...] * alpha
            p_buf[slot][...] = p.astype(BF16)

        def step(j, slot, masked=False, more=True):
            if more:
                logits(j + 1, 1 - slot)
            values(j - 1, 1 - slot)
            softmax(slot, masked)

        logits(0, 0)
        p1_scr[...] = jnp.zeros((t, t), BF16)

        def loop(a, carry):
            step(2 * a, 0)
            step(2 * a + 1, 1)
            return carry

        lax.fori_loop(0, i // 2, loop, 0)

        @pl.when(i % 2 == 0)
        def _():
            step(i, 0, masked=True, more=False)
            values(i, 0)

        @pl.when(i % 2 == 1)
        def _():
            step(i - 1, 0)
            step(i, 1, masked=True, more=False)
            values(i, 1)

        l = l_scr[...]
        ov = acc_scr[...] / l
        g = g_ref[...]
        o_ref[...] = ov
        a_ref[...] = (ov * g * _sig(g)).astype(BF16)
        lse_ref[0] = m_scr[...] + jnp.log(l) * LOG2E

    head_col = lambda w, off: pl.BlockSpec((S, w), lambda h, i: (0, 2 * h + off))
    return pl.pallas_call(
        body, grid=(H, nq),
        in_specs=[pl.BlockSpec((t, HEAD_PAD), lambda h, i: (i, h)), head_col(QK_NOPE, 0), head_col(V_HEAD, 1),
                  pl.BlockSpec((S, LANES), lambda h, i: (0, 0)), pl.BlockSpec((t, V_HEAD), lambda h, i: (i, h))],
        out_specs=[pl.BlockSpec((t, V_HEAD), lambda h, i: (i, h)), pl.BlockSpec((t, V_HEAD), lambda h, i: (i, h)),
                   pl.BlockSpec((1, t, LANES), lambda h, i: (h, i, 0))],
        out_shape=[jax.ShapeDtypeStruct((S, H * V_HEAD), F32), jax.ShapeDtypeStruct((S, H * V_HEAD), BF16),
                   jax.ShapeDtypeStruct((H, S, LANES), F32)],
        scratch_shapes=[pltpu.VMEM((t, LANES), F32), pltpu.VMEM((t, LANES), F32), pltpu.VMEM((t, V_HEAD), F32),
                        pltpu.VMEM((t, t), F32), pltpu.VMEM((t, t), F32), pltpu.VMEM((t, t), BF16), pltpu.VMEM((t, t), BF16)],
        compiler_params=_cp(("parallel", "arbitrary")), name="attn_fwd",
    )(q, kv, kv, kr, zbig)


TN_DIMS = (((0,), (0,)), ((), ()))


def _attn_bwd(q, kv, kr, do, lse_row, delta_row, H):
    S = q.shape[0]
    t = _pick(S, ATT_T)
    nk = S // t

    def body(kn_ref, v_ref, kr_ref, q_ref, do_ref, lse_ref, dl_ref, dq_ref, dkv_ref, dkr_ref, dk_scr, dv_scr,
             st0_scr, st1_scr, dp0_scr, dp1_scr):
        j = pl.program_id(1)
        dk_scr[...] = jnp.zeros((t, HEAD_PAD), F32)
        dv_scr[...] = jnp.zeros((t, V_HEAD), F32)

        @pl.when(j == 0)
        def _():
            dq_ref[...] = jnp.zeros((S, HEAD_PAD), F32)

        st_buf, dp_buf = (st0_scr, st1_scr), (dp0_scr, dp1_scr)

        def keys():
            return jnp.concatenate([kn_ref[...], kr_ref[...]], axis=1)

        def rows(i):
            return pl.ds(pl.multiple_of(jnp.minimum(i, nk - 1) * t, t), t)

        def scores(i, slot):
            st_buf[slot][...] = lax.dot_general(keys(), q_ref[rows(i), :], NT_DIMS, preferred_element_type=F32)
            dp_buf[slot][...] = lax.dot_general(v_ref[...], do_ref[rows(i), :], NT_DIMS, preferred_element_type=F32)

        def step(s, slot, masked=False):
            i = j + s
            scores(i + 1, 1 - slot)
            pt = jnp.exp2(st_buf[slot][...] - lse_ref[0, :, rows(i)])
            if masked:
                pt = jnp.where(_chunk_mask(t, t, 0, False), pt, 0.0)
            dst = (pt * (dp_buf[slot][...] - dl_ref[0, :, rows(i)])).astype(BF16)
            dv_scr[...] += jnp.dot(pt.astype(BF16), do_ref[rows(i), :], preferred_element_type=F32)
            dk_scr[...] += jnp.dot(dst, q_ref[rows(i), :], preferred_element_type=F32)
            dq_ref[rows(i), :] += lax.dot_general(dst, keys(), TN_DIMS, preferred_element_type=F32)

        n = nk - j
        scores(j, 0)
        step(0, 0, masked=True)

        def loop(a, carry):
            step(2 * a + 1, 1)
            step(2 * a + 2, 0)
            return carry

        lax.fori_loop(0, (n - 1) // 2, loop, 0)

        @pl.when((n - 1) % 2 == 1)
        def _():
            step(n - 1, 1)

        dk = dk_scr[...] * (1.0 / LOG2E)
        dkv_ref[:, 0:QK_NOPE] = dk[:, 0:QK_NOPE].astype(BF16)
        dkv_ref[:, QK_NOPE:] = dv_scr[...].astype(BF16)
        dkr_ref[0] = dk[:, QK_NOPE:]

    tile_col = lambda w, off: pl.BlockSpec((t, w), lambda h, j: (j, 2 * h + off))
    row = pl.BlockSpec((1, 1, S), lambda h, j: (h, 0, 0))
    return pl.pallas_call(
        body, grid=(H, nk),
        in_specs=[tile_col(QK_NOPE, 0), tile_col(V_HEAD, 1), pl.BlockSpec((t, LANES), lambda h, j: (j, 0)),
                  pl.BlockSpec((S, HEAD_PAD), lambda h, j: (0, h)), pl.BlockSpec((S, V_HEAD), lambda h, j: (0, h)), row, row],
        out_specs=[pl.BlockSpec((S, HEAD_PAD), lambda h, j: (0, h)), pl.BlockSpec((t, HEAD_PAD), lambda h, j: (j, h)),
                   pl.BlockSpec((1, t, LANES), lambda h, j: (h, j, 0))],
        out_shape=[jax.ShapeDtypeStruct((S, H * HEAD_PAD), F32), jax.ShapeDtypeStruct((S, H * HEAD_PAD), BF16),
                   jax.ShapeDtypeStruct((H, S, LANES), F32)],
        scratch_shapes=[pltpu.VMEM((t, HEAD_PAD), F32), pltpu.VMEM((t, V_HEAD), F32)] + [pltpu.VMEM((t, t), F32)] * 4,
        compiler_params=_cp(("arbitrary", "arbitrary")), name="attn_bwd",
    )(kv, kv, kr, q, do, lse_row, delta_row)


def _shift_down(x, prev8, s):
    rx = pltpu.roll(x, s, 0)
    rp = pltpu.roll(prev8, s, 0)
    rows = lax.broadcasted_iota(jnp.int32, rp.shape, 0)
    return jnp.concatenate([jnp.where(rows < s, rp, rx[:8]), rx[8:]], axis=0)


def _shift_up(x, next8, s):
    n = x.shape[0]
    rx = pltpu.roll(x, n - s, 0)
    rn = pltpu.roll(next8, 8 - s, 0)
    rows = lax.broadcasted_iota(jnp.int32, rn.shape, 0)
    return jnp.concatenate([rx[:n - 8], jnp.where(rows >= 8 - s, rn, rx[n - 8:])], axis=0)


def _scan_rows(a, b, up):
    n = a.shape[0]
    rows = lax.broadcasted_iota(jnp.int32, a.shape, 0)
    d = 1
    while d < n:
        keep = (rows < n - d) if up else (rows >= d)
        sh = n - d if up else d
        a_s = jnp.where(keep, pltpu.roll(a, sh, 0), 1.0)
        b_s = jnp.where(keep, pltpu.roll(b, sh, 0), 0.0)
        b = a * b_s + b
        a = a * a_s
        d *= 2
    return a, b


def _log1p(e):
    u = 1.0 + e
    return jnp.where(u == 1.0, e, jnp.log(u) * (e / (u - 1.0)))


def _lru_pre(u, prev8, cw_ref, cb_ref, wr_ref, br_ref, wi_ref, bi_ref, lam_ref):
    us = [u, _shift_down(u, prev8, 1), _shift_down(u, prev8, 2), _shift_down(u, prev8, 3)]
    xc = cb_ref[...] + cw_ref[3:4, :] * us[0] + cw_ref[2:3, :] * us[1] + cw_ref[1:2, :] * us[2] + cw_ref[0:1, :] * us[3]
    x16 = xc.astype(BF16)
    nb = xc.shape[1] // LRU_BLOCK_DIM
    blk = lambda k: slice(k * LRU_BLOCK_DIM, (k + 1) * LRU_BLOCK_DIM)
    pr = jnp.concatenate([jnp.dot(x16[:, blk(k)], wr_ref[k].astype(BF16), preferred_element_type=F32) for k in range(nb)], axis=1)
    pi = jnp.concatenate([jnp.dot(x16[:, blk(k)], wi_ref[k].astype(BF16), preferred_element_type=F32) for k in range(nb)], axis=1)
    r = _sig(pr + br_ref[...])
    i = _sig(pi + bi_ref[...])
    nlam = -lam_ref[...]
    sp = jnp.maximum(nlam, 0.0) + _log1p(jnp.exp(-jnp.abs(nlam)))
    log_a = (-LRU_C * r) * sp
    a = jnp.exp(log_a)
    mult = jnp.sqrt(-jnp.tanh(log_a) * (a * a + 1.0))
    return us, xc, x16, r, i, sp, a, mult


def _lru_specs(D, cb, tt, nT, rev):
    nb = cb // LRU_BLOCK_DIM
    tmap = (lambda t: nT - 1 - t) if rev else (lambda t: t)
    ncb = D // cb

    def tile(piece):
        return pl.BlockSpec((tt, cb), lambda c, t: (tmap(t), piece * ncb + c))

    def halo(piece):
        return pl.BlockSpec((8, cb), lambda c, t: (jnp.maximum(tmap(t) * (tt // 8) - 1, 0), piece * ncb + c))

    par = lambda rows: pl.BlockSpec((rows, cb), lambda c, t: (0, c))
    wblk = pl.BlockSpec((nb, LRU_BLOCK_DIM, LRU_BLOCK_DIM), lambda c, t: (c, 0, 0))
    return tile, halo, par, wblk, tmap


def _lru_fwd(zbig, cw, cbias, wr, br, wi, bi, lam, D):
    S = zbig.shape[0]
    tt, cb = _pick(S, LRU_TT, 8), _pick(D, LRU_CB)
    nT = S // tt
    tile, halo, par, wblk, _ = _lru_specs(D, cb, tt, nT, False)

    def body(u_ref, up_ref, g_ref, cw_ref, cb_ref, wr_ref, br_ref, wi_ref, bi_ref, lam_ref, h_ref, al_ref, carry):
        t = pl.program_id(1)
        prev8 = jnp.where(t > 0, up_ref[...], 0.0)
        _, xc, _, _, i, _, a, mult = _lru_pre(u_ref[...], prev8, cw_ref, cb_ref, wr_ref, br_ref, wi_ref, bi_ref, lam_ref)
        pa, hb = _scan_rows(a, mult * (i * xc), False)
        h0 = jnp.where(t > 0, carry[7:8, :], 0.0)
        h = hb + pa * h0
        h_ref[...] = h
        carry[...] = h[tt - 8:, :]
        g = g_ref[...]
        al_ref[...] = (h * g * _sig(g)).astype(BF16)

    return pl.pallas_call(
        body, grid=(D // cb, nT),
        in_specs=[tile(1), halo(1), tile(2), par(CONV_K), par(1), wblk, par(1), wblk, par(1), par(1)],
        out_specs=[tile(0), tile(0)],
        out_shape=[jax.ShapeDtypeStruct((S, D), F32), jax.ShapeDtypeStruct((S, D), BF16)],
        scratch_shapes=[pltpu.VMEM((8, cb), F32)],
        compiler_params=_cp(("parallel", "arbitrary")), name="lru_fwd",
    )(zbig, zbig, zbig, cw, cbias.reshape(1, D), wr, br.reshape(1, D), wi, bi.reshape(1, D), lam.reshape(1, D))


def _lru_bwd(zbig, h, d_al, cw, cbias, wr, br, wi, bi, lam, D):
    S = zbig.shape[0]
    tt, cb = _pick(S, LRU_TT, 8), _pick(D, LRU_CB)
    nT = S // tt
    nb = cb // LRU_BLOCK_DIM
    tile, halo, par, wblk, tmap = _lru_specs(D, cb, tt, nT, True)

    def body(u_ref, up_ref, g_ref, h_ref, hp_ref, dal_ref, cw_ref, cb_ref, wr_ref, br_ref, wi_ref, bi_ref, lam_ref,
             du_ref, dg_ref, dcw_ref, dcb_ref, dwr_ref, dbr_ref, dwi_ref, dbi_ref, dlam_ref, g_car, a_car, x_car):
        step = pl.program_id(1)
        first = step == 0
        t = nT - 1 - step
        prev8 = jnp.where(t > 0, up_ref[...], 0.0)
        us, xc, x16, r, i, sp, a, mult = _lru_pre(u_ref[...], prev8, cw_ref, cb_ref, wr_ref, br_ref, wi_ref, bi_ref, lam_ref)
        hv = h_ref[...]
        h_m1 = _shift_down(hv, jnp.where(t > 0, hp_ref[...], 0.0), 1)
        g, dal = g_ref[...], dal_ref[...]
        sg = _sig(g)
        dg_ref[...] = (dal * hv * sg * (1.0 + g * (1.0 - sg))).astype(BF16)
        dh = dal * g * sg
        coef = _shift_up(a, jnp.where(first, 0.0, a_car[...]), 1)
        pa, gb = _scan_rows(coef, dh, True)
        G = gb + pa * jnp.where(first, 0.0, g_car[0:1, :])
        g_car[...] = G[:8]
        a_car[...] = a[:8]
        da = G * h_m1
        ixc = i * xc
        dixc = G * mult
        dlog = da * a - (G * ixc) * (a * a) / mult
        dpr = dlog * (-LRU_C * sp) * r * (1.0 - r)
        dpi = dixc * xc * i * (1.0 - i)
        dxc = dixc * i
        dsp = jnp.sum(dlog * (-LRU_C) * r, axis=0, keepdims=True)
        dpr16, dpi16 = dpr.astype(BF16), dpi.astype(BF16)
        blk = lambda k: slice(k * LRU_BLOCK_DIM, (k + 1) * LRU_BLOCK_DIM)
        back = []
        for k in range(nb):
            xk = x16[:, blk(k)].T
            dwr_k = jnp.dot(xk, dpr16[:, blk(k)], preferred_element_type=F32)
            dwi_k = jnp.dot(xk, dpi16[:, blk(k)], preferred_element_type=F32)

            @pl.when(first)
            def _():
                dwr_ref[k] = dwr_k
                dwi_ref[k] = dwi_k

            @pl.when(jnp.logical_not(first))
            def _():
                dwr_ref[k] += dwr_k
                dwi_ref[k] += dwi_k

            back.append(lax.dot_general(dpr16[:, blk(k)], wr_ref[k].astype(BF16), NT_DIMS, preferred_element_type=F32)
                        + lax.dot_general(dpi16[:, blk(k)], wi_ref[k].astype(BF16), NT_DIMS, preferred_element_type=F32))
        dxc = dxc + jnp.concatenate(back, axis=1)
        _accum(dbr_ref, jnp.sum(dpr, axis=0, keepdims=True), first)
        _accum(dbi_ref, jnp.sum(dpi, axis=0, keepdims=True), first)
        _accum(dlam_ref, dsp * (-_sig(-lam_ref[...])), first)
        _accum(dcb_ref, jnp.sum(dxc, axis=0, keepdims=True), first)
        _accum(dcw_ref, jnp.concatenate([jnp.sum(dxc * us[3 - k], axis=0, keepdims=True) for k in range(CONV_K)], axis=0), first)
        nxt = jnp.where(first, 0.0, x_car[...])
        du = cw_ref[3:4, :] * dxc
        for s in range(1, CONV_K):
            du = du + cw_ref[3 - s:4 - s, :] * _shift_up(dxc, nxt, s)
        x_car[...] = dxc[:8]
        du_ref[...] = du.astype(BF16)

    act = jax.ShapeDtypeStruct((S, D), BF16)
    vec = jax.ShapeDtypeStruct((1, D), F32)
    wsh = jax.ShapeDtypeStruct(wr.shape, F32)
    rtile = pl.BlockSpec((tt, cb), lambda c, t: (tmap(t), c))
    rhalo = pl.BlockSpec((8, cb), lambda c, t: (jnp.maximum(tmap(t) * (tt // 8) - 1, 0), c))
    return pl.pallas_call(
        body, grid=(D // cb, nT),
        in_specs=[tile(1), halo(1), tile(2), rtile, rhalo, rtile, par(CONV_K), par(1), wblk, par(1), wblk, par(1), par(1)],
        out_specs=[rtile, rtile, par(CONV_K), par(1), wblk, par(1), wblk, par(1), par(1)],
        out_shape=[act, act, jax.ShapeDtypeStruct((CONV_K, D), F32), vec, wsh, vec, wsh, vec, vec],
        scratch_shapes=[pltpu.VMEM((8, cb), F32)] * 3,
        compiler_params=_cp(("parallel", "arbitrary")), name="lru_bwd",
    )(zbig, zbig, zbig, h, h, d_al, cw, cbias.reshape(1, D), wr, br.reshape(1, D), wi, bi.reshape(1, D), lam.reshape(1, D))


def _adamw(w, g, m, v, name):
    L, R, C = w.shape
    tm = _pick(R, max(8, ((1 << 18) // C) // 8 * 8), 8)
    c1 = 1.0 - ADAM_B1 ** ADAM_STEP
    c2 = 1.0 - ADAM_B2 ** ADAM_STEP

    def body(w_ref, g_ref, m_ref, v_ref, d_ref, mo_ref, vo_ref):
        gv = g_ref[...]
        mn = ADAM_B1 * m_ref[...] + (1.0 - ADAM_B1) * gv
        vn = ADAM_B2 * v_ref[...] + (1.0 - ADAM_B2) * (gv * gv)
        d_ref[...] = -ADAM_LR * ((mn / c1) / (jnp.sqrt(vn / c2) + ADAM_EPS) + ADAM_WD * w_ref[...])
        mo_ref[...] = mn
        vo_ref[...] = vn

    o = jax.ShapeDtypeStruct((L, R, C), F32)
    blk = pl.BlockSpec((1, tm, C), lambda l, r: (l, r, 0))
    return pl.pallas_call(
        body, grid=(L, R // tm), in_specs=[blk] * 4, out_specs=[blk] * 3, out_shape=[o] * 3,
        compiler_params=_cp(("parallel", "parallel")), name=name,
    )(w, g, m, v)


def _sum_slabs(x, name, out_dtype=F32):
    n, R, C = x.shape
    tm = _pick(R, max(8, ((1 << 18) // C) // 8 * 8), 16)

    def body(x_ref, o_ref):
        s = x_ref[0].astype(F32)
        for k in range(1, n):
            s = s + x_ref[k].astype(F32)
        o_ref[...] = s.astype(out_dtype)

    return pl.pallas_call(
        body, grid=(R // tm,), in_specs=[pl.BlockSpec((n, tm, C), lambda r: (0, r, 0))], out_specs=_rows(C)(tm),
        out_shape=jax.ShapeDtypeStruct((R, C), out_dtype), compiler_params=_cp(("parallel",)), name=name,
    )(x)


def _sum_core_halves(g, recv, core, tag):
    n, _, Rh, C = g.shape
    tm = _pick(Rh, max(16, ((1 << 18) // C) // 16 * 16), 16)

    def body(c_ref, g_ref, r_ref, o_ref):
        o_ref[0] = (g_ref[0, 0].astype(F32) + r_ref[0, 0].astype(F32)).astype(BF16)

    return pl.pallas_call(
        body,
        grid_spec=pltpu.PrefetchScalarGridSpec(
            num_scalar_prefetch=1, grid=(n, Rh // tm),
            in_specs=[pl.BlockSpec((1, 1, tm, C), lambda k, r, c_ref: (k, c_ref[0], r, 0)),
                      pl.BlockSpec((1, 1, tm, C), lambda k, r, c_ref: (k, 0, r, 0))],
            out_specs=pl.BlockSpec((1, tm, C), lambda k, r, c_ref: (k, r, 0)),
        ),
        out_shape=jax.ShapeDtypeStruct((n, Rh, C), BF16),
        compiler_params=_cp(("parallel", "parallel")), name="grad_sum_cores" + tag,
    )(core.reshape(1).astype(jnp.int32), g, recv)


ANY = pl.BlockSpec(memory_space=pl.ANY)


def _place():
    x, y, c = lax.axis_index("x"), lax.axis_index("y"), lax.axis_index("c")
    chips = [(1 - x, y), (x, 1 - y), (1 - x, 1 - y)]
    return x, y, c, chips


def _allgather_chips(shard, name):
    R, C = shard.shape
    Rh = R // 2

    def body(x_ref, out_ref, send_sems, recv_sems):
        x, y, c, chips = _place()
        me = 2 * x + y
        sibling = (x, y, 1 - c)

        def half(k, hc):
            return out_ref.at[k, pl.ds(hc * Rh, Rh), :]

        first = [pltpu.make_async_remote_copy(
            src_ref=x_ref.at[pl.ds(c * Rh, Rh), :], dst_ref=half(me, c), send_sem=send_sems.at[j], recv_sem=recv_sems.at[j],
            device_id=(*chip, c), device_id_type=MESH) for j, chip in enumerate(chips)]
        for cp in first:
            cp.start()

        def landed(j, chip, hc):
            k = 2 * chip[0] + chip[1]
            return pltpu.make_async_remote_copy(
                src_ref=half(k, hc), dst_ref=half(k, hc), send_sem=send_sems.at[j], recv_sem=recv_sems.at[j],
                device_id=sibling, device_id_type=MESH)

        passed = []
        for j, chip in enumerate(chips):
            landed(j, chip, c).wait_recv()
            cp = landed(3 + j, chip, c)
            cp.start()
            passed.append(cp)
        for j, chip in enumerate(chips):
            landed(3 + j, chip, 1 - c).wait_recv()
        for cp in first + passed:
            cp.wait_send()

    out = pl.pallas_call(
        body, in_specs=[ANY], out_specs=ANY, out_shape=jax.ShapeDtypeStruct((N_CHIPS, R, C), shard.dtype),
        scratch_shapes=[pltpu.SemaphoreType.DMA((6,)), pltpu.SemaphoreType.DMA((6,))],
        name=name,
    )(shard)
    return lax.dynamic_update_slice(out, shard[None], (2 * lax.axis_index("x") + lax.axis_index("y"), 0, 0))


def _allgather_all(blockv, name):
    R, C = blockv.shape

    def body(x_ref, out_ref, send_sems, recv_sems):
        x, y, c, chips = _place()
        sibling = (x, y, 1 - c)

        def slab(px, py, pc):
            return out_ref.at[4 * px + 2 * py + pc]

        def copy(k, block, to, src=None):
            return pltpu.make_async_remote_copy(
                src_ref=slab(*block) if src is None else src, dst_ref=slab(*block), send_sem=send_sems.at[k],
                recv_sem=recv_sems.at[k], device_id=to, device_id_type=MESH)

        first = [copy(0, (x, y, c), sibling, src=x_ref)]
        first += [copy(1 + j, (x, y, c), (*chip, c), src=x_ref) for j, chip in enumerate(chips)]
        for cp in first:
            cp.start()
        passed = [copy(4 + j, (*chip, c), sibling) for j, chip in enumerate(chips)]
        for j, chip in enumerate(chips):
            copy(1 + j, (*chip, c), (x, y, c)).wait_recv()
            passed[j].start()
        copy(0, (x, y, 1 - c), (x, y, c)).wait_recv()
        for j, chip in enumerate(chips):
            copy(4 + j, (*chip, 1 - c), (x, y, c)).wait_recv()
        for cp in first + passed:
            cp.wait_send()

    out = pl.pallas_call(
        body, in_specs=[ANY], out_specs=ANY, out_shape=jax.ShapeDtypeStruct((N_DEV, R, C), blockv.dtype),
        scratch_shapes=[pltpu.SemaphoreType.DMA((7,)), pltpu.SemaphoreType.DMA((7,))],
        name=name,
    )(blockv)
    me = 4 * lax.axis_index("x") + 2 * lax.axis_index("y") + lax.axis_index("c")
    return lax.dynamic_update_slice(out, blockv[None], (me, 0, 0))


def _swap_cores_half(g, tag):
    n, _, Rh, C = g.shape

    def body(g_ref, out_ref, send_sem, recv_sem):
        x, y, c, _ = _place()
        cp = pltpu.make_async_remote_copy(
            src_ref=g_ref.at[:, pl.ds(1 - c, 1)], dst_ref=out_ref, send_sem=send_sem, recv_sem=recv_sem,
            device_id=(x, y, 1 - c), device_id_type=MESH)
        cp.start()
        cp.wait()

    return pl.pallas_call(
        body, in_specs=[ANY], out_specs=ANY, out_shape=jax.ShapeDtypeStruct((n, 1, Rh, C), g.dtype),
        scratch_shapes=[pltpu.SemaphoreType.DMA, pltpu.SemaphoreType.DMA], name="grad_swap_cores" + tag,
    )(g)


def _alltoall_chips(s, tag):
    n, Rh, C = s.shape

    def body(s_ref, out_ref, send_sems, recv_sems):
        x, y, c, chips = _place()
        me = 2 * x + y
        sent = []
        for j, chip in enumerate(chips):
            k = 2 * chip[0] + chip[1]
            cp = pltpu.make_async_remote_copy(
                src_ref=s_ref.at[k], dst_ref=out_ref.at[me], send_sem=send_sems.at[j], recv_sem=recv_sems.at[j],
                device_id=(*chip, c), device_id_type=MESH)
            cp.start()
            sent.append(cp)
        for j, chip in enumerate(chips):
            k = 2 * chip[0] + chip[1]
            pltpu.make_async_remote_copy(
                src_ref=s_ref.at[k], dst_ref=out_ref.at[k], send_sem=send_sems.at[j], recv_sem=recv_sems.at[j],
                device_id=(*chip, c), device_id_type=MESH).wait_recv()
        for cp in sent:
            cp.wait_send()

    out = pl.pallas_call(
        body, in_specs=[ANY], out_specs=ANY, out_shape=jax.ShapeDtypeStruct((n, Rh, C), s.dtype),
        scratch_shapes=[pltpu.SemaphoreType.DMA((3,)), pltpu.SemaphoreType.DMA((3,))],
        name="grad_alltoall_chips" + tag,
    )(s)
    me = 2 * lax.axis_index("x") + lax.axis_index("y")
    return lax.dynamic_update_slice(out, lax.dynamic_slice_in_dim(s, me, 1, axis=0), (me, 0, 0))


def _join_core_halves(half, tag):
    Rh, C = half.shape

    def body(h_ref, out_ref, send_sem, recv_sem):
        x, y, c, _ = _place()
        cp = pltpu.make_async_remote_copy(
            src_ref=h_ref, dst_ref=out_ref.at[c], send_sem=send_sem, recv_sem=recv_sem,
            device_id=(x, y, 1 - c), device_id_type=MESH)
        cp.start()
        pltpu.make_async_remote_copy(
            src_ref=h_ref, dst_ref=out_ref.at[1 - c], send_sem=send_sem, recv_sem=recv_sem,
            device_id=(x, y, 1 - c), device_id_type=MESH).wait_recv()
        cp.wait_send()

    out = pl.pallas_call(
        body, in_specs=[ANY], out_specs=ANY, out_shape=jax.ShapeDtypeStruct((2, Rh, C), half.dtype),
        scratch_shapes=[pltpu.SemaphoreType.DMA, pltpu.SemaphoreType.DMA], name="grad_join_cores" + tag,
    )(half)
    return lax.dynamic_update_slice(out, half[None], (lax.axis_index("c"), 0, 0))


def _pack(arrays, cols, row_align):
    flat = jnp.concatenate([a.reshape(-1) for a in arrays])
    unit = cols * row_align
    total = -(-flat.size // unit) * unit
    return jnp.pad(flat, (0, total - flat.size)).reshape(total // cols, cols)


def _unpack(buf, shapes):
    flat = buf.reshape(-1)
    out, off = [], 0
    for shp in shapes:
        n = 1
        for d in shp:
            n *= d
        out.append(flat[off:off + n].reshape(shp))
        off += n
    return out


def _layer_fwd(x, p_l, w, tabs, dm):
    D, H, ql, kvl = dm["D"], dm["H"], dm["ql"], dm["kvl"]
    h = _rms_fwd(x, w["attn_norm"], "attn_norm_fwd")
    zbig = _mm(h, w["w_big"], "nn", "in_proj_big")
    zsm = _mm(h, w["w_sm"], "nn", "in_proj_small")
    qn, kvn, kr = _latent_fwd(zsm, w["q_a_norm"], w["kv_a_norm"], tabs, ql, kvl)
    q = _q_rope(_mm(qn, w["w_q"], "nn", "q_proj"), tabs, False, "q_rope_fwd", gain=QK_SCALE * LOG2E)
    kv = _mm(kvn, w["w_kv"], "nn", "kv_proj", out_dtype=BF16)
    o, a_mla, lse = _attn_fwd(q, kv, kr, zbig, H)
    y_mla = _mm(a_mla, w["w_o_mla"], "nn", "o_mla_proj")
    h_lru, a_lru = _lru_fwd(zbig, w["conv_w"], w["conv_b"], w["w_rg"], w["b_rg"], w["w_ig"], w["b_ig"], w["lru_lambda"], D)
    y_lru = _mm(a_lru, w["w_o_lru"], "nn", "o_lru_proj")
    merged = _merge_fwd(zbig, y_mla, y_lru, D)
    x1 = _mm(merged, w["w_out"], "nn", "out_proj", add=x)
    hp = _rms_fwd(x1, w["ple_norm"], "ple_norm_fwd")
    pg = _mm(hp, w["w_ple_gate"], "nn", "ple_gate_proj")
    pe = _mm(p_l, w["w_ple"], "nn", "ple_proj")
    x2 = _ple_fwd(x1, pe, pg)
    res = dict(x=x, h=h, zbig=zbig, zsm=zsm, qn=qn, kvn=kvn, kr=kr, q=q, kv=kv, o=o, a_mla=a_mla, lse=lse, y_mla=y_mla,
               h_lru=h_lru, a_lru=a_lru, y_lru=y_lru, merged=merged, x1=x1, hp=hp, pg=pg, pe=pe, p=p_l)
    return x2, res


def _layer_bwd(dx2, r, w, tabs, dm):
    D, H, ql, kvl = dm["D"], dm["H"], dm["ql"], dm["kvl"]
    S = dx2.shape[0]
    g = {}
    d_pe, d_pg = _ple_bwd(dx2, r["pe"], r["pg"])
    g["w_ple"] = _mm(r["p"], d_pe, "tn", "ple_proj_dw", out_dtype=BF16)
    g["w_ple_gate"] = _mm(r["hp"], d_pg, "tn", "ple_gate_dw", out_dtype=BF16)
    d_hp = _mm(d_pg, w["w_ple_gate"], "nt", "ple_gate_dx")
    dx1, dx1_16, g["ple_norm"] = _rms_bwd(r["x1"], w["ple_norm"], d_hp, dx2, "ple_norm_bwd", True)
    g["w_out"] = _mm(r["merged"], dx1_16, "tn", "out_proj_dw", out_dtype=BF16)
    d_merged = _mm(dx1_16, w["w_out"], "nt", "out_proj_dx")
    d_ym, d_yl, d_mm, d_ml = _merge_bwd(r["zbig"], r["y_mla"], r["y_lru"], d_merged, D)
    g["w_o_mla"] = _mm(r["a_mla"], d_ym, "tn", "o_mla_dw", out_dtype=BF16)
    d_a_mla = _mm(d_ym, w["w_o_mla"], "nt", "o_mla_dx")
    g["w_o_lru"] = _mm(r["a_lru"], d_yl, "tn", "o_lru_dw", out_dtype=BF16)
    d_a_lru = _mm(d_yl, w["w_o_lru"], "nt", "o_lru_dx")
    d_o, d_gm, delta = _attn_gate_bwd(d_a_mla, r["o"], r["zbig"], H)
    dq, dkv, dkr = _attn_bwd(r["q"], r["kv"], r["kr"], d_o, r["lse"][:, :, 0].reshape(H, 1, S), delta[:, :, 0].reshape(H, 1, S), H)
    dq_pre = _q_rope(dq, tabs, True, "q_rope_bwd", gain=QK_SCALE)
    g["w_q"] = _mm(r["qn"], dq_pre, "tn", "q_proj_dw", out_dtype=BF16)
    d_qn = _mm(dq_pre, w["w_q"], "nt", "q_proj_dx")
    g["w_kv"] = _mm(r["kvn"], dkv, "tn", "kv_proj_dw", out_dtype=BF16)
    d_kvn = _mm(dkv, w["w_kv"], "nt", "kv_proj_dx")
    dzsm, g["q_a_norm"], g["kv_a_norm"] = _latent_bwd(r["zsm"], w["q_a_norm"], w["kv_a_norm"], tabs, d_qn, d_kvn, dkr, ql, kvl)
    (d_u, d_gl, g["conv_w"], g["conv_b"], g["w_rg"], g["b_rg"], g["w_ig"], g["b_ig"], g["lru_lambda"]) = _lru_bwd(
        r["zbig"], r["h_lru"], d_a_lru, w["conv_w"], w["conv_b"], w["w_rg"], w["b_rg"], w["w_ig"], w["b_ig"], w["lru_lambda"], D)
    dzbig = jnp.concatenate([d_gm, d_u, d_gl, d_mm, d_ml], axis=1)
    g["w_big"] = _mm(r["h"], dzbig, "tn", "in_proj_big_dw", out_dtype=BF16)
    g["w_sm"] = _mm(r["h"], dzsm, "tn", "in_proj_small_dw", out_dtype=BF16)
    dh = _mm(dzbig, w["w_big"], "nt", "in_proj_big_dx")
    dh = _mm(dzsm, w["w_sm"], "nt", "in_proj_small_dx", add=dh)
    dx, g["attn_norm"] = _rms_bwd(r["x"], w["attn_norm"], dh, dx1, "attn_norm_bwd", False)
    return dx, g


SHARDED = ("w_in", "w_q_b", "w_kv_b", "w_o_mla", "w_o_lru", "w_out", "w_ple_gate", "w_ple")
COL_SHARDED = ("w_in", "w_q_b", "w_kv_b", "w_ple")
ROWED = ("w_o_mla", "w_o_lru", "w_out", "w_ple_gate")
FLAT = ("w_q_b", "w_kv_b", "w_ple")
REPLICATED = ("attn_norm", "q_a_norm", "kv_a_norm", "conv_b", "w_rg", "b_rg", "w_ig", "b_ig", "lru_lambda", "ple_norm", "final_norm")
WEIGHTS = ("attn_norm", "w_in", "q_a_norm", "w_q_b", "kv_a_norm", "w_kv_b", "conv_w", "conv_b", "w_rg", "b_rg", "w_ig", "b_ig",
           "lru_lambda", "w_o_mla", "w_o_lru", "w_out", "ple_norm", "w_ple_gate", "w_ple", "final_norm")


def kernel(x, p, positions, attn_norm, w_in, q_a_norm, w_q_b, kv_a_norm, w_kv_b, conv_w, conv_b, w_rg, b_rg, w_ig, b_ig, lru_lambda, w_o_mla, w_o_lru, w_out, ple_norm, w_ple_gate, w_ple, final_norm, loss_target, m_attn_norm, m_w_in, m_q_a_norm, m_w_q_b, m_kv_a_norm, m_w_kv_b, m_conv_w, m_conv_b, m_w_rg, m_b_rg, m_w_ig, m_b_ig, m_lru_lambda, m_w_o_mla, m_w_o_lru, m_w_out, m_ple_norm, m_w_ple_gate, m_w_ple, m_final_norm, v_attn_norm, v_w_in, v_q_a_norm, v_w_q_b, v_kv_a_norm, v_w_kv_b, v_conv_w, v_conv_b, v_w_rg, v_b_rg, v_w_ig, v_b_ig, v_lru_lambda, v_w_o_mla, v_w_o_lru, v_w_out, v_ple_norm, v_w_ple_gate, v_w_ple, v_final_norm):
    W = dict(attn_norm=attn_norm, w_in=w_in, q_a_norm=q_a_norm, w_q_b=w_q_b, kv_a_norm=kv_a_norm, w_kv_b=w_kv_b, conv_w=conv_w,
             conv_b=conv_b, w_rg=w_rg, b_rg=b_rg, w_ig=w_ig, b_ig=b_ig, lru_lambda=lru_lambda, w_o_mla=w_o_mla, w_o_lru=w_o_lru,
             w_out=w_out, ple_norm=ple_norm, w_ple_gate=w_ple_gate, w_ple=w_ple, final_norm=final_norm)
    M = dict(attn_norm=m_attn_norm, w_in=m_w_in, q_a_norm=m_q_a_norm, w_q_b=m_w_q_b, kv_a_norm=m_kv_a_norm, w_kv_b=m_w_kv_b,
             conv_w=m_conv_w, conv_b=m_conv_b, w_rg=m_w_rg, b_rg=m_b_rg, w_ig=m_w_ig, b_ig=m_b_ig, lru_lambda=m_lru_lambda,
             w_o_mla=m_w_o_mla, w_o_lru=m_w_o_lru, w_out=m_w_out, ple_norm=m_ple_norm, w_ple_gate=m_w_ple_gate, w_ple=m_w_ple,
             final_norm=m_final_norm)
    V = dict(attn_norm=v_attn_norm, w_in=v_w_in, q_a_norm=v_q_a_norm, w_q_b=v_w_q_b, kv_a_norm=v_kv_a_norm, w_kv_b=v_w_kv_b,
             conv_w=v_conv_w, conv_b=v_conv_b, w_rg=v_w_rg, b_rg=v_b_rg, w_ig=v_w_ig, b_ig=v_b_ig, lru_lambda=v_lru_lambda,
             w_o_mla=v_w_o_mla, w_o_lru=v_w_o_lru, w_out=v_w_out, ple_norm=v_ple_norm, w_ple_gate=v_w_ple_gate, w_ple=v_w_ple,
             final_norm=v_final_norm)
    depth = attn_norm.shape[0]
    S, D = x.shape[1], x.shape[2]
    ql, kvl = q_a_norm.shape[1], kv_a_norm.shape[1]
    H = w_q_b.shape[2] * N_CHIPS // (QK_NOPE + QK_ROPE)
    dm = dict(D=D, H=H, ql=ql, kvl=kvl)
    chip = 2 * lax.axis_index("x") + lax.axis_index("y")
    core = lax.axis_index("c")

    def rest_rows(get):
        rows = [a.reshape(-1, D) for n in ROWED for a in get(n)] + [_pack(get(n), D, 16) for n in FLAT]
        fill = -sum(r.shape[0] for r in rows) % PACK_ROWS
        return rows + ([jnp.zeros((fill, D), rows[0].dtype)] if fill else [])

    def unpack_rest(buf):
        out, off = {}, 0
        for n in ROWED + FLAT:
            rows = -(-W[n].size // (D * 16)) * 16
            part = buf[off:off + rows]
            out[n] = part.reshape(W[n].shape) if n in ROWED else _unpack(part, [W[n].shape])[0]
            off += rows
        return out

    cin = w_in.shape[2]
    got_in = _allgather_chips(w_in.astype(BF16).reshape(depth * D, cin), "w_in_allgather")
    got_rest = _allgather_chips(jnp.concatenate(rest_rows(lambda n: [W[n].astype(BF16)]), axis=0), "weights_allgather")
    slabs = [dict(unpack_rest(got_rest[k]), w_in=got_in[k].reshape(depth, D, cin)) for k in range(N_CHIPS)]
    cw_all = _allgather_chips(_pack([conv_w], LANES, 16), "conv_w_allgather")
    conv_w_full = jnp.concatenate([_unpack(cw_all[k], [conv_w.shape])[0] for k in range(N_CHIPS)], axis=-1)

    n_small = ql + kvl + QK_ROPE
    hpc = H // N_CHIPS
    head_pad = ((0, 0), (0, 0), (0, HEAD_PAD - QK_NOPE - QK_ROPE))
    layers = []
    for l in range(depth):
        cat = lambda n, axis: jnp.concatenate([s[n][l] for s in slabs], axis=axis)
        layers.append(dict(
            w_big=jnp.concatenate([slabs[0]["w_in"][l][:, n_small:]] + [s["w_in"][l] for s in slabs[1:]], axis=1),
            w_sm=jnp.pad(slabs[0]["w_in"][l][:, :n_small], ((0, 0), (0, LANES - QK_ROPE))),
            w_q=jnp.concatenate([jnp.pad(s["w_q_b"][l].reshape(ql, hpc, QK_NOPE + QK_ROPE), head_pad).reshape(ql, hpc * HEAD_PAD)
                                 for s in slabs], axis=1),
            w_kv=cat("w_kv_b", 1), w_o_mla=cat("w_o_mla", 0), w_o_lru=cat("w_o_lru", 0), w_out=cat("w_out", 0),
            w_ple_gate=cat("w_ple_gate", 0), w_ple=cat("w_ple", 1), conv_w=conv_w_full[l],
            **{n: W[n][l] for n in REPLICATED if n != "final_norm"}))

    inv_freq = ROPE_THETA ** (-jnp.arange(0, QK_ROPE, 2, dtype=F32) / QK_ROPE)
    tabs = _rope_tables(positions[0], inv_freq)

    xs = x[0]
    saved = []
    for l in range(depth):
        xs, res = _layer_fwd(xs, p[l, 0], layers[l], tabs, dm)
        saved.append(res)
    dx, g_final_norm, loss_part = _loss_head(xs, final_norm, loss_target[0])
    grads = [None] * depth
    for l in reversed(range(depth)):
        dx, grads[l] = _layer_bwd(dx, saved[l], layers[l], tabs, dm)

    def stack(name):
        return jnp.stack([grads[l][name] for l in range(depth)])

    def shard_of(l, n, k):
        g = grads[l]
        if n == "w_in":
            lo, hi = k * W[n].shape[2], (k + 1) * W[n].shape[2]
            parts = ([g["w_sm"][:, lo:min(hi, n_small)]] if lo < n_small else []) + (
                [g["w_big"][:, max(lo, n_small) - n_small:hi - n_small]] if hi > n_small else [])
            return jnp.concatenate(parts, axis=1)
        if n == "w_q_b":
            return g["w_q"].reshape(ql, H, HEAD_PAD)[:, k * hpc:(k + 1) * hpc, :QK_NOPE + QK_ROPE].reshape(ql, -1)
        mine = {"w_kv_b": "w_kv"}.get(n, n)
        if n in COL_SHARDED:
            return g[mine][:, k * W[n].shape[2]:(k + 1) * W[n].shape[2]]
        return g[mine][k * W[n].shape[1]:(k + 1) * W[n].shape[1], :]

    def reduce_scatter(slab_rows, tag):
        _, R, C = slab_rows.shape
        gp = slab_rows.reshape(N_CHIPS, 2, R // 2, C)
        core_sum = _sum_core_halves(gp, _swap_cores_half(gp, tag), core, tag)
        chip_sum = _sum_slabs(_alltoall_chips(core_sum, tag), "grad_sum_chips" + tag)
        return _join_core_halves(chip_sum, tag).reshape(R, C)

    in_rows = [shard_of(l, "w_in", k) for k in range(N_CHIPS) for l in range(depth)]
    g_in = reduce_scatter(jnp.concatenate(in_rows, axis=0).reshape(N_CHIPS, depth * D, cin), "_w_in")
    rest = []
    for k in range(N_CHIPS):
        rest += rest_rows(lambda n: [shard_of(l, n, k) for l in range(depth)])
    g_rest = reduce_scatter(jnp.concatenate(rest, axis=0).reshape(N_CHIPS, -1, D), "_rest")
    G = dict(unpack_rest(g_rest), w_in=g_in.reshape(w_in.shape))

    rep_shapes = [W[n].shape for n in REPLICATED] + [(depth, CONV_K, D), (LANES,)]
    rep = [stack(n).reshape(W[n].shape) for n in REPLICATED if n != "final_norm"]
    rep += [g_final_norm.reshape(D), stack("conv_w"), loss_part.reshape(LANES)]
    rep_sum = _unpack(_sum_slabs(_allgather_all(_pack(rep, LANES, SMALL_ROWS), "small_grads_allgather"), "small_grads_sum"), rep_shapes)
    for n, gv in zip(REPLICATED, rep_sum):
        G[n] = gv
    cshard = D // N_CHIPS
    G["conv_w"] = lax.dynamic_slice_in_dim(rep_sum[-2], chip * cshard, cshard, axis=2)
    loss = rep_sum[-1][0]

    small = REPLICATED + ("conv_w",)
    small_shapes = [W[n].shape for n in small]
    pk = lambda src: _pack([src[n] for n in small], LANES, SMALL_ROWS)[None]
    upd = _adamw(pk(W), pk(G), pk(M), pk(V), "adamw_small")
    delta, new_m, new_v = ({n: a for n, a in zip(small, _unpack(u, small_shapes))} for u in upd)
    for n in SHARDED:
        delta[n], new_m[n], new_v[n] = _adamw(W[n], G[n], M[n], V[n], "adamw_" + n)

    return (loss, dx.reshape(x.shape), *[G[n] for n in WEIGHTS], *[delta[n] for n in WEIGHTS],
            *[new_m[n] for n in WEIGHTS], *[new_v[n] for n in WEIGHTS])
```

```python
import jax
import jax.numpy as jnp
from jax import lax
from jax.experimental import pallas as pl
from jax.experimental.pallas import tpu as pltpu

F32 = jnp.float32
BF16 = jnp.bfloat16
MESH = pl.DeviceIdType.MESH

CHUNK = 64
QK_NOPE = 128
QK_ROPE = 64
V_HEAD = 128
ROPE_THETA = 10000.0
CONV_K = 4
LRU_C = 8.0
LRU_BLOCK_DIM = 128
EPS = 1e-6
ADAM_LR = 0.001
ADAM_B1 = 0.9
ADAM_B2 = 0.999
ADAM_EPS = 1e-08
ADAM_WD = 0.01
ADAM_STEP = 10

N_CHIPS = 4
N_DEV = 8
LANES = 128
HEAD_PAD = 256
VMEM_LIMIT = 48 * 1024 * 1024

MM_TILE_BYTES = 8 * 1024 * 1024
ATT_T = 512
ATT_T_FWD = 1024
ROW_T = 256
LRU_TT = 512
LRU_CB = 512
PACK_ROWS = 1024
SMALL_ROWS = 512

NT_DIMS = (((1,), (1,)), ((), ()))


def _cp(sem):
    return pltpu.CompilerParams(dimension_semantics=sem, vmem_limit_bytes=VMEM_LIMIT)


def _pick(n, pref, align=LANES):
    if n <= pref:
        return n
    t = pref - pref % align
    while t >= align:
        if n % t == 0:
            return t
        t -= align
    return n


def _sig(x):
    return 1.0 / (1.0 + jnp.exp(-x))


def _mm(a, b, mode, name, out_dtype=F32, add=None, tm=None, tn=1024, tk=2048):
    if mode == "nn":
        (M, K), (_, N) = a.shape, b.shape
    elif mode == "nt":
        (M, K), (N, _) = a.shape, b.shape
    else:
        (K, M), (_, N) = a.shape, b.shape
    if tm is None:
        tm = 1024 if a.dtype.itemsize == 2 and b.dtype.itemsize == 2 else 512
    tm, tn = _pick(M, tm), _pick(N, tn)
    while tk > 512 and tk * (tm * a.dtype.itemsize + tn * b.dtype.itemsize) > MM_TILE_BYTES:
        tk //= 2
    tk = _pick(K, tk)
    nm, nn, nk = M // tm, N // tn, K // tk
    i_outer = nm * b.size * b.dtype.itemsize <= nn * a.size * a.dtype.itemsize

    def ij(g0, g1):
        return (g0, g1) if i_outer else (g1, g0)

    def amap(g0, g1, k):
        i, _ = ij(g0, g1)
        return (k, i) if mode == "tn" else (i, k)

    def bmap(g0, g1, k):
        _, j = ij(g0, g1)
        return (j, k) if mode == "nt" else (k, j)

    def omap(g0, g1, k):
        return ij(g0, g1)

    ablk = (tk, tm) if mode == "tn" else (tm, tk)
    bblk = (tn, tk) if mode == "nt" else (tk, tn)

    def body(*refs):
        if add is None:
            a_ref, b_ref, o_ref = refs[:3]
            add_ref = None
        else:
            a_ref, b_ref, add_ref, o_ref = refs[:4]
        x = a_ref[...].astype(BF16)
        y = b_ref[...].astype(BF16)
        if mode == "nn":
            p = jnp.dot(x, y, preferred_element_type=F32)
        elif mode == "nt":
            p = lax.dot_general(x, y, NT_DIMS, preferred_element_type=F32)
        else:
            p = jnp.dot(x.T, y, preferred_element_type=F32)
        if nk == 1:
            if add_ref is not None:
                p = p + add_ref[...]
            o_ref[...] = p.astype(out_dtype)
        else:
            acc = refs[-1]
            k = pl.program_id(2)

            @pl.when(k == 0)
            def _():
                acc[...] = p if add_ref is None else p + add_ref[...]

            @pl.when(k > 0)
            def _():
                acc[...] += p

            @pl.when(k == nk - 1)
            def _():
                o_ref[...] = acc[...].astype(out_dtype)

    in_specs = [pl.BlockSpec(ablk, amap), pl.BlockSpec(bblk, bmap)]
    args = [a, b]
    if add is not None:
        in_specs.append(pl.BlockSpec((tm, tn), omap))
        args.append(add)
    grid = (nm, nn, nk) if i_outer else (nn, nm, nk)
    return pl.pallas_call(
        body, grid=grid, in_specs=in_specs, out_specs=pl.BlockSpec((tm, tn), omap),
        out_shape=jax.ShapeDtypeStruct((M, N), out_dtype),
        scratch_shapes=[pltpu.VMEM((tm, tn), F32)] if nk > 1 else [],
        compiler_params=_cp(("parallel", "parallel", "arbitrary")), name=name,
    )(*args)


def _rows(cols, i=0):
    def make(tm):
        return pl.BlockSpec((tm, cols), lambda r: (r, i))
    return make


def _par(cols):
    return pl.BlockSpec((1, cols), lambda r: (0, 0))


def _rms_fwd(x, g, name):
    S, D = x.shape
    tm = _pick(S, ROW_T, 8)

    def body(x_ref, g_ref, o_ref):
        xv = x_ref[...]
        rs = lax.rsqrt(jnp.mean(xv * xv, axis=-1, keepdims=True) + EPS)
        o_ref[...] = (xv * rs * g_ref[...]).astype(BF16)

    return pl.pallas_call(
        body, grid=(S // tm,), in_specs=[_rows(D)(tm), _par(D)], out_specs=_rows(D)(tm),
        out_shape=jax.ShapeDtypeStruct((S, D), BF16), compiler_params=_cp(("parallel",)), name=name,
    )(x, g.reshape(1, D))


def _rms_bwd_math(xv, g, dy):
    rs = lax.rsqrt(jnp.mean(xv * xv, axis=-1, keepdims=True) + EPS)
    xh = xv * rs
    dg = jnp.sum(dy * xh, axis=0, keepdims=True)
    dyg = dy * g
    dx = rs * (dyg - xh * jnp.mean(dyg * xh, axis=-1, keepdims=True))
    return dx, dg


def _accum(ref, val, first):
    @pl.when(first)
    def _():
        ref[...] = val

    @pl.when(jnp.logical_not(first))
    def _():
        ref[...] += val


def _rms_bwd(x, g, dy, dres, name, with_bf16):
    S, D = x.shape
    tm = _pick(S, ROW_T, 8)

    def body(x_ref, g_ref, dy_ref, dres_ref, dx_ref, *rest):
        dx, dg = _rms_bwd_math(x_ref[...], g_ref[...], dy_ref[...])
        dx = dres_ref[...] + dx
        dx_ref[...] = dx
        if with_bf16:
            rest[0][...] = dx.astype(BF16)
        _accum(rest[-1], dg, pl.program_id(0) == 0)

    extra = [jax.ShapeDtypeStruct((S, D), BF16)] if with_bf16 else []
    return pl.pallas_call(
        body, grid=(S // tm,), in_specs=[_rows(D)(tm), _par(D), _rows(D)(tm), _rows(D)(tm)],
        out_specs=[_rows(D)(tm)] * (1 + len(extra)) + [_par(D)],
        out_shape=[jax.ShapeDtypeStruct((S, D), F32)] + extra + [jax.ShapeDtypeStruct((1, D), F32)],
        compiler_params=_cp(("arbitrary",)), name=name,
    )(x, g.reshape(1, D), dy, dres)


def _rope_tables(pos, inv_freq):
    S = pos.shape[0]
    tm = _pick(S, 512, 8)
    half = QK_ROPE // 2
    invf = jnp.concatenate([inv_freq, inv_freq, jnp.zeros((LANES - QK_ROPE,), F32)]).reshape(1, LANES)

    def body(pos_ref, f_ref, c_ref, sa_ref, sb_ref):
        ang = pos_ref[...].astype(F32) * f_ref[...]
        lane = lax.broadcasted_iota(jnp.int32, ang.shape, 1)
        c, s = jnp.cos(ang), jnp.sin(ang)
        c_ref[...] = jnp.where(lane < QK_ROPE, c, 0.0)
        sa_ref[...] = jnp.where(lane < half, -s, 0.0)
        sb_ref[...] = jnp.where((lane >= half) & (lane < QK_ROPE), s, 0.0)

    tab = jax.ShapeDtypeStruct((S, LANES), F32)
    return pl.pallas_call(
        body, grid=(S // tm,), in_specs=[pl.BlockSpec((tm, 1), lambda r: (r, 0)), _par(LANES)],
        out_specs=[_rows(LANES)(tm)] * 3, out_shape=[tab] * 3, compiler_params=_cp(("parallel",)), name="rope_tables",
    )(pos.reshape(S, 1), invf)


def _rope(x, c, sa, sb):
    return x * c + pltpu.roll(x, LANES - QK_ROPE // 2, 1) * sa + pltpu.roll(x, QK_ROPE // 2, 1) * sb


def _rope_t(d, c, sa, sb):
    return d * c + pltpu.roll(d * sa, QK_ROPE // 2, 1) + pltpu.roll(d * sb, LANES - QK_ROPE // 2, 1)


def _latent_fwd(zsm, gq, gkv, tabs, ql, kvl):
    S = zsm.shape[0]
    tm = _pick(S, ROW_T, 8)
    kr_blk = (ql + kvl) // LANES

    def body(q_ref, kv_ref, kr_ref, gq_ref, gkv_ref, c_ref, sa_ref, sb_ref, qn_ref, kvn_ref, kro_ref):
        for src, g_ref, dst in ((q_ref, gq_ref, qn_ref), (kv_ref, gkv_ref, kvn_ref)):
            v = src[...]
            rs = lax.rsqrt(jnp.mean(v * v, axis=-1, keepdims=True) + EPS)
            dst[...] = (v * rs * g_ref[...]).astype(BF16)
        kro_ref[...] = _rope(kr_ref[...], c_ref[...], sa_ref[...], sb_ref[...]).astype(BF16)

    return pl.pallas_call(
        body, grid=(S // tm,),
        in_specs=[_rows(ql, 0)(tm), _rows(kvl, 1)(tm), _rows(LANES, kr_blk)(tm), _par(ql), _par(kvl)] + [_rows(LANES)(tm)] * 3,
        out_specs=[_rows(ql)(tm), _rows(kvl)(tm), _rows(LANES)(tm)],
        out_shape=[jax.ShapeDtypeStruct((S, ql), BF16), jax.ShapeDtypeStruct((S, kvl), BF16), jax.ShapeDtypeStruct((S, LANES), BF16)],
        compiler_params=_cp(("parallel",)), name="latent_fwd",
    )(zsm, zsm, zsm, gq.reshape(1, ql), gkv.reshape(1, kvl), *tabs)


def _latent_bwd(zsm, gq, gkv, tabs, d_qn, d_kvn, dkr, ql, kvl):
    S, W = zsm.shape
    H = dkr.shape[0]
    tm = _pick(S, ROW_T, 8)
    kr_blk = (ql + kvl) // LANES

    def body(q_ref, kv_ref, gq_ref, gkv_ref, c_ref, sa_ref, sb_ref, dqn_ref, dkvn_ref, dkr_ref, dz_ref, dgq_ref, dgkv_ref):
        first = pl.program_id(0) == 0
        dq, dgq = _rms_bwd_math(q_ref[...], gq_ref[...], dqn_ref[...])
        dkv, dgkv = _rms_bwd_math(kv_ref[...], gkv_ref[...], dkvn_ref[...])
        dk = dkr_ref[0]
        for h in range(1, H):
            dk = dk + dkr_ref[h]
        dz_ref[:, 0:ql] = dq.astype(BF16)
        dz_ref[:, ql:ql + kvl] = dkv.astype(BF16)
        dz_ref[:, ql + kvl:] = _rope_t(dk, c_ref[...], sa_ref[...], sb_ref[...]).astype(BF16)
        _accum(dgq_ref, dgq, first)
        _accum(dgkv_ref, dgkv, first)

    return pl.pallas_call(
        body, grid=(S // tm,),
        in_specs=[_rows(ql, 0)(tm), _rows(kvl, 1)(tm), _par(ql), _par(kvl)] + [_rows(LANES)(tm)] * 3
        + [_rows(ql)(tm), _rows(kvl)(tm), pl.BlockSpec((H, tm, LANES), lambda r: (0, r, 0))],
        out_specs=[_rows(W)(tm), _par(ql), _par(kvl)],
        out_shape=[jax.ShapeDtypeStruct((S, W), BF16), jax.ShapeDtypeStruct((1, ql), F32), jax.ShapeDtypeStruct((1, kvl), F32)],
        compiler_params=_cp(("arbitrary",)), name="latent_bwd",
    )(zsm, zsm, gq.reshape(1, ql), gkv.reshape(1, kvl), *tabs, d_qn, d_kvn, dkr)


def _q_rope(q, tabs, transpose, name, gain=1.0):
    S, W = q.shape
    H = W // HEAD_PAD
    tm = _pick(S, ROW_T, 8)
    fn = _rope_t if transpose else _rope

    def body(q_ref, c_ref, sa_ref, sb_ref, o_ref):
        c, sa, sb = c_ref[...], sa_ref[...], sb_ref[...]
        if gain != 1.0:
            c, sa, sb = c * gain, sa * gain, sb * gain
        for h in range(H):
            lo = h * HEAD_PAD
            nope = q_ref[:, lo:lo + QK_NOPE]
            o_ref[:, lo:lo + QK_NOPE] = (nope if gain == 1.0 else nope * gain).astype(BF16)
            o_ref[:, lo + QK_NOPE:lo + HEAD_PAD] = fn(q_ref[:, lo + QK_NOPE:lo + HEAD_PAD], c, sa, sb).astype(BF16)

    return pl.pallas_call(
        body, grid=(S // tm,), in_specs=[_rows(W)(tm)] + [_rows(LANES)(tm)] * 3, out_specs=_rows(W)(tm),
        out_shape=jax.ShapeDtypeStruct((S, W), BF16), compiler_params=_cp(("parallel",)), name=name,
    )(q, *tabs)


def _merge_fwd(zbig, y_mla, y_lru, D):
    S = y_mla.shape[0]
    tm = _pick(S, ROW_T, 8)

    def body(mm_ref, ml_ref, ym_ref, yl_ref, o_ref):
        o_ref[...] = (_sig(mm_ref[...]) * ym_ref[...] + _sig(ml_ref[...]) * yl_ref[...]).astype(BF16)

    return pl.pallas_call(
        body, grid=(S // tm,), in_specs=[_rows(D, 3)(tm), _rows(D, 4)(tm), _rows(D)(tm), _rows(D)(tm)], out_specs=_rows(D)(tm),
        out_shape=jax.ShapeDtypeStruct((S, D), BF16), compiler_params=_cp(("parallel",)), name="merge_fwd",
    )(zbig, zbig, y_mla, y_lru)


def _merge_bwd(zbig, y_mla, y_lru, d_merged, D):
    S = y_mla.shape[0]
    tm = _pick(S, ROW_T, 8)

    def body(mm_ref, ml_ref, ym_ref, yl_ref, d_ref, dym_ref, dyl_ref, dmm_ref, dml_ref):
        d = d_ref[...]
        sm, sl = _sig(mm_ref[...]), _sig(ml_ref[...])
        dym_ref[...] = (d * sm).astype(BF16)
        dyl_ref[...] = (d * sl).astype(BF16)
        dmm_ref[...] = (d * ym_ref[...] * sm * (1.0 - sm)).astype(BF16)
        dml_ref[...] = (d * yl_ref[...] * sl * (1.0 - sl)).astype(BF16)

    o = jax.ShapeDtypeStruct((S, D), BF16)
    return pl.pallas_call(
        body, grid=(S // tm,), in_specs=[_rows(D, 3)(tm), _rows(D, 4)(tm)] + [_rows(D)(tm)] * 3, out_specs=[_rows(D)(tm)] * 4,
        out_shape=[o] * 4, compiler_params=_cp(("parallel",)), name="merge_bwd",
    )(zbig, zbig, y_mla, y_lru, d_merged)


def _ple_fwd(x1, pe, pg):
    S, D = x1.shape
    tm = _pick(S, ROW_T, 8)

    def body(x_ref, pe_ref, pg_ref, o_ref):
        o_ref[...] = x_ref[...] + pe_ref[...] * _sig(pg_ref[...])

    return pl.pallas_call(
        body, grid=(S // tm,), in_specs=[_rows(D)(tm)] * 3, out_specs=_rows(D)(tm),
        out_shape=jax.ShapeDtypeStruct((S, D), F32), compiler_params=_cp(("parallel",)), name="ple_fwd",
    )(x1, pe, pg)


def _ple_bwd(dx2, pe, pg):
    S, D = dx2.shape
    tm = _pick(S, ROW_T, 8)

    def body(d_ref, pe_ref, pg_ref, dpe_ref, dpg_ref):
        d = d_ref[...]
        s = _sig(pg_ref[...])
        dpe_ref[...] = (d * s).astype(BF16)
        dpg_ref[...] = (d * pe_ref[...] * s * (1.0 - s)).astype(BF16)

    o = jax.ShapeDtypeStruct((S, D), BF16)
    return pl.pallas_call(
        body, grid=(S // tm,), in_specs=[_rows(D)(tm)] * 3, out_specs=[_rows(D)(tm)] * 2, out_shape=[o] * 2,
        compiler_params=_cp(("parallel",)), name="ple_bwd",
    )(dx2, pe, pg)


def _loss_head(x, g, target):
    S, D = x.shape
    tm = _pick(S, ROW_T, 8)

    def body(x_ref, g_ref, t_ref, dx_ref, dg_ref, loss_ref):
        first = pl.program_id(0) == 0
        xv, gv = x_ref[...], g_ref[...]
        rs = lax.rsqrt(jnp.mean(xv * xv, axis=-1, keepdims=True) + EPS)
        e = xv * rs * gv - t_ref[...]
        part = 0.5 * jnp.sum(jnp.mean(e * e, axis=-1, keepdims=True), axis=0, keepdims=True)
        dx, dg = _rms_bwd_math(xv, gv, e * (1.0 / D))
        dx_ref[...] = dx
        _accum(dg_ref, dg, first)
        _accum(loss_ref, jnp.broadcast_to(part, (1, LANES)), first)

    return pl.pallas_call(
        body, grid=(S // tm,), in_specs=[_rows(D)(tm), _par(D), _rows(D)(tm)], out_specs=[_rows(D)(tm), _par(D), _par(LANES)],
        out_shape=[jax.ShapeDtypeStruct((S, D), F32), jax.ShapeDtypeStruct((1, D), F32), jax.ShapeDtypeStruct((1, LANES), F32)],
        compiler_params=_cp(("arbitrary",)), name="loss_head",
    )(x, g.reshape(1, D), target)


def _attn_gate_bwd(d_a, o, zbig, H):
    S, W = o.shape
    tm = _pick(S, ROW_T, 8)

    def body(da_ref, o_ref, g_ref, do_ref, dg_ref, dl_ref):
        da, ov, g = da_ref[...], o_ref[...], g_ref[...]
        s = _sig(g)
        do = da * g * s
        do_ref[...] = do.astype(BF16)
        dg_ref[...] = (da * ov * s * (1.0 + g * (1.0 - s))).astype(BF16)
        prod = do * ov
        for h in range(H):
            r = jnp.sum(prod[:, h * V_HEAD:(h + 1) * V_HEAD], axis=-1, keepdims=True)
            dl_ref[h] = jnp.broadcast_to(r, (tm, LANES))

    return pl.pallas_call(
        body, grid=(S // tm,), in_specs=[_rows(W)(tm), _rows(W)(tm), _rows(W, 0)(tm)],
        out_specs=[_rows(W)(tm), _rows(W)(tm), pl.BlockSpec((H, tm, LANES), lambda r: (0, r, 0))],
        out_shape=[jax.ShapeDtypeStruct((S, W), BF16), jax.ShapeDtypeStruct((S, W), BF16), jax.ShapeDtypeStruct((H, S, LANES), F32)],
        compiler_params=_cp(("parallel",)), name="attn_gate_bwd",
    )(d_a, o, zbig)


def _chunk_mask(ts, t, lo, q_rows):
    r = (lax.broadcasted_iota(jnp.int32, (ts, t), 0) + lo) // CHUNK
    c = lax.broadcasted_iota(jnp.int32, (ts, t), 1) // CHUNK
    return (c <= r) if q_rows else (r <= c)


QK_SCALE = 1.0 / (QK_NOPE + QK_ROPE) ** 0.5
LOG2E = 1.4426950408889634


def _attn_fwd(q, kv, kr, zbig, H):
    S = q.shape[0]
    t = _pick(S, ATT_T_FWD)
    nq = S // t

    def body(q_ref, kn_ref, v_ref, kr_ref, g_ref, o_ref, a_ref, lse_ref, m_scr, l_scr, acc_scr, s0_scr, s1_scr, p0_scr, p1_scr):
        i = pl.program_id(1)
        m_scr[...] = jnp.full((t, LANES), -1e30, F32)
        l_scr[...] = jnp.zeros((t, LANES), F32)
        acc_scr[...] = jnp.zeros((t, V_HEAD), F32)

        s_buf, p_buf = (s0_scr, s1_scr), (p0_scr, p1_scr)

        def logits(j, slot):
            ks = pl.multiple_of(j * t, t)
            k = jnp.concatenate([kn_ref[pl.ds(ks, t), :], kr_ref[pl.ds(ks, t), :]], axis=1)
            s_buf[slot][...] = lax.dot_general(q_ref[...], k, NT_DIMS, preferred_element_type=F32)

        def values(j, slot):
            ks = pl.multiple_of(jnp.maximum(j, 0) * t, t)
            acc_scr[...] += jnp.dot(p_buf[slot][...], v_ref[pl.ds(ks, t), :], preferred_element_type=F32)

        def softmax(slot, masked):
            s = s_buf[slot][...]
            if masked:
                s = jnp.where(_chunk_mask(t, t, 0, True), s, -1e30)
            m_prev = m_scr[...]
            m_next = jnp.maximum(m_prev, jnp.max(s, axis=1, keepdims=True))
            p = jnp.exp2(s - jnp.tile(m_next, (1, t // LANES)))
            alpha = jnp.exp2(m_prev - m_next)
            l_scr[...] = alpha * l_scr[...] + jnp.sum(p, axis=1, keepdims=True)
            m_scr[...] = m_next
            acc_scr[...] = acc_scr[...] * alpha
            p_buf[slot][...] = p.astype(BF16)

        def step(j, slot, masked=False, more=True):
            if more:
                logits(j + 1, 1 - slot)
            values(j - 1, 1 - slot)
            softmax(slot, masked)

        logits(0, 0)
        p1_scr[...] = jnp.zeros((t, t), BF16)

        def loop(a, carry):
            step(2 * a, 0)
            step(2 * a + 1, 1)
            return carry

        lax.fori_loop(0, i // 2, loop, 0)

        @pl.when(i % 2 == 0)
        def _():
            step(i, 0, masked=True, more=False)
            values(i, 0)

        @pl.when(i % 2 == 1)
        def _():
            step(i - 1, 0)
            step(i, 1, masked=True, more=False)
            values(i, 1)

        l = l_scr[...]
        ov = acc_scr[...] / l
        g = g_ref[...]
        o_ref[...] = ov
        a_ref[...] = (ov * g * _sig(g)).astype(BF16)
        lse_ref[0] = m_scr[...] + jnp.log(l) * LOG2E

    head_col = lambda w, off: pl.BlockSpec((S, w), lambda h, i: (0, 2 * h + off))
    return pl.pallas_call(
        body, grid=(H, nq),
        in_specs=[pl.BlockSpec((t, HEAD_PAD), lambda h, i: (i, h)), head_col(QK_NOPE, 0), head_col(V_HEAD, 1),
                  pl.BlockSpec((S, LANES), lambda h, i: (0, 0)), pl.BlockSpec((t, V_HEAD), lambda h, i: (i, h))],
        out_specs=[pl.BlockSpec((t, V_HEAD), lambda h, i: (i, h)), pl.BlockSpec((t, V_HEAD), lambda h, i: (i, h)),
                   pl.BlockSpec((1, t, LANES), lambda h, i: (h, i, 0))],
        out_shape=[jax.ShapeDtypeStruct((S, H * V_HEAD), F32), jax.ShapeDtypeStruct((S, H * V_HEAD), BF16),
                   jax.ShapeDtypeStruct((H, S, LANES), F32)],
        scratch_shapes=[pltpu.VMEM((t, LANES), F32), pltpu.VMEM((t, LANES), F32), pltpu.VMEM((t, V_HEAD), F32),
                        pltpu.VMEM((t, t), F32), pltpu.VMEM((t, t), F32), pltpu.VMEM((t, t), BF16), pltpu.VMEM((t, t), BF16)],
        compiler_params=_cp(("parallel", "arbitrary")), name="attn_fwd",
    )(q, kv, kv, kr, zbig)


TN_DIMS = (((0,), (0,)), ((), ()))


def _attn_bwd(q, kv, kr, do, lse_row, delta_row, H):
    S = q.shape[0]
    t = _pick(S, ATT_T)
    nk = S // t

    def body(kn_ref, v_ref, kr_ref, q_ref, do_ref, lse_ref, dl_ref, dq_ref, dkv_ref, dkr_ref, dk_scr, dv_scr,
             st0_scr, st1_scr, dp0_scr, dp1_scr):
        j = pl.program_id(1)
        dk_scr[...] = jnp.zeros((t, HEAD_PAD), F32)
        dv_scr[...] = jnp.zeros((t, V_HEAD), F32)

        @pl.when(j == 0)
        def _():
            dq_ref[...] = jnp.zeros((S, HEAD_PAD), F32)

        st_buf, dp_buf = (st0_scr, st1_scr), (dp0_scr, dp1_scr)

        def keys():
            return jnp.concatenate([kn_ref[...], kr_ref[...]], axis=1)

        def rows(i):
            return pl.ds(pl.multiple_of(jnp.minimum(i, nk - 1) * t, t), t)

        def scores(i, slot):
            st_buf[slot][...] = lax.dot_general(keys(), q_ref[rows(i), :], NT_DIMS, preferred_element_type=F32)
            dp_buf[slot][...] = lax.dot_general(v_ref[...], do_ref[rows(i), :], NT_DIMS, preferred_element_type=F32)

        def step(s, slot, masked=False):
            i = j + s
            scores(i + 1, 1 - slot)
            pt = jnp.exp2(st_buf[slot][...] - lse_ref[0, :, rows(i)])
            if masked:
                pt = jnp.where(_chunk_mask(t, t, 0, False), pt, 0.0)
            dst = (pt * (dp_buf[slot][...] - dl_ref[0, :, rows(i)])).astype(BF16)
            dv_scr[...] += jnp.dot(pt.astype(BF16), do_ref[rows(i), :], preferred_element_type=F32)
            dk_scr[...] += jnp.dot(dst, q_ref[rows(i), :], preferred_element_type=F32)
            dq_ref[rows(i), :] += lax.dot_general(dst, keys(), TN_DIMS, preferred_element_type=F32)

        n = nk - j
        scores(j, 0)
        step(0, 0, masked=True)

        def loop(a, carry):
            step(2 * a + 1, 1)
            step(2 * a + 2, 0)
            return carry

        lax.fori_loop(0, (n - 1) // 2, loop, 0)

        @pl.when((n - 1) % 2 == 1)
        def _():
            step(n - 1, 1)

        dk = dk_scr[...] * (1.0 / LOG2E)
        dkv_ref[:, 0:QK_NOPE] = dk[:, 0:QK_NOPE].astype(BF16)
        dkv_ref[:, QK_NOPE:] = dv_scr[...].astype(BF16)
        dkr_ref[0] = dk[:, QK_NOPE:]

    tile_col = lambda w, off: pl.BlockSpec((t, w), lambda h, j: (j, 2 * h + off))
    row = pl.BlockSpec((1, 1, S), lambda h, j: (h, 0, 0))
    return pl.pallas_call(
        body, grid=(H, nk),
        in_specs=[tile_col(QK_NOPE, 0), tile_col(V_HEAD, 1), pl.BlockSpec((t, LANES), lambda h, j: (j, 0)),
                  pl.BlockSpec((S, HEAD_PAD), lambda h, j: (0, h)), pl.BlockSpec((S, V_HEAD), lambda h, j: (0, h)), row, row],
        out_specs=[pl.BlockSpec((S, HEAD_PAD), lambda h, j: (0, h)), pl.BlockSpec((t, HEAD_PAD), lambda h, j: (j, h)),
                   pl.BlockSpec((1, t, LANES), lambda h, j: (h, j, 0))],
        out_shape=[jax.ShapeDtypeStruct((S, H * HEAD_PAD), F32), jax.ShapeDtypeStruct((S, H * HEAD_PAD), BF16),
                   jax.ShapeDtypeStruct((H, S, LANES), F32)],
        scratch_shapes=[pltpu.VMEM((t, HEAD_PAD), F32), pltpu.VMEM((t, V_HEAD), F32)] + [pltpu.VMEM((t, t), F32)] * 4,
        compiler_params=_cp(("arbitrary", "arbitrary")), name="attn_bwd",
    )(kv, kv, kr, q, do, lse_row, delta_row)


def _shift_down(x, prev8, s):
    rx = pltpu.roll(x, s, 0)
    rp = pltpu.roll(prev8, s, 0)
    rows = lax.broadcasted_iota(jnp.int32, rp.shape, 0)
    return jnp.concatenate([jnp.where(rows < s, rp, rx[:8]), rx[8:]], axis=0)


def _shift_up(x, next8, s):
    n = x.shape[0]
    rx = pltpu.roll(x, n - s, 0)
    rn = pltpu.roll(next8, 8 - s, 0)
    rows = lax.broadcasted_iota(jnp.int32, rn.shape, 0)
    return jnp.concatenate([rx[:n - 8], jnp.where(rows >= 8 - s, rn, rx[n - 8:])], axis=0)


def _scan_rows(a, b, up):
    n = a.shape[0]
    rows = lax.broadcasted_iota(jnp.int32, a.shape, 0)
    d = 1
    while d < n:
        keep = (rows < n - d) if up else (rows >= d)
        sh = n - d if up else d
        a_s = jnp.where(keep, pltpu.roll(a, sh, 0), 1.0)
        b_s = jnp.where(keep, pltpu.roll(b, sh, 0), 0.0)
        b = a * b_s + b
        a = a * a_s
        d *= 2
    return a, b


def _log1p(e):
    u = 1.0 + e
    return jnp.where(u == 1.0, e, jnp.log(u) * (e / (u - 1.0)))


def _lru_pre(u, prev8, cw_ref, cb_ref, wr_ref, br_ref, wi_ref, bi_ref, lam_ref):
    us = [u, _shift_down(u, prev8, 1), _shift_down(u, prev8, 2), _shift_down(u, prev8, 3)]
    xc = cb_ref[...] + cw_ref[3:4, :] * us[0] + cw_ref[2:3, :] * us[1] + cw_ref[1:2, :] * us[2] + cw_ref[0:1, :] * us[3]
    x16 = xc.astype(BF16)
    nb = xc.shape[1] // LRU_BLOCK_DIM
    blk = lambda k: slice(k * LRU_BLOCK_DIM, (k + 1) * LRU_BLOCK_DIM)
    pr = jnp.concatenate([jnp.dot(x16[:, blk(k)], wr_ref[k].astype(BF16), preferred_element_type=F32) for k in range(nb)], axis=1)
    pi = jnp.concatenate([jnp.dot(x16[:, blk(k)], wi_ref[k].astype(BF16), preferred_element_type=F32) for k in range(nb)], axis=1)
    r = _sig(pr + br_ref[...])
    i = _sig(pi + bi_ref[...])
    nlam = -lam_ref[...]
    sp = jnp.maximum(nlam, 0.0) + _log1p(jnp.exp(-jnp.abs(nlam)))
    log_a = (-LRU_C * r) * sp
    a = jnp.exp(log_a)
    mult = jnp.sqrt(-jnp.tanh(log_a) * (a * a + 1.0))
    return us, xc, x16, r, i, sp, a, mult


def _lru_specs(D, cb, tt, nT, rev):
    nb = cb // LRU_BLOCK_DIM
    tmap = (lambda t: nT - 1 - t) if rev else (lambda t: t)
    ncb = D // cb

    def tile(piece):
        return pl.BlockSpec((tt, cb), lambda c, t: (tmap(t), piece * ncb + c))

    def halo(piece):
        return pl.BlockSpec((8, cb), lambda c, t: (jnp.maximum(tmap(t) * (tt // 8) - 1, 0), piece * ncb + c))

    par = lambda rows: pl.BlockSpec((rows, cb), lambda c, t: (0, c))
    wblk = pl.BlockSpec((nb, LRU_BLOCK_DIM, LRU_BLOCK_DIM), lambda c, t: (c, 0, 0))
    return tile, halo, par, wblk, tmap


def _lru_fwd(zbig, cw, cbias, wr, br, wi, bi, lam, D):
    S = zbig.shape[0]
    tt, cb = _pick(S, LRU_TT, 8), _pick(D, LRU_CB)
    nT = S // tt
    tile, halo, par, wblk, _ = _lru_specs(D, cb, tt, nT, False)

    def body(u_ref, up_ref, g_ref, cw_ref, cb_ref, wr_ref, br_ref, wi_ref, bi_ref, lam_ref, h_ref, al_ref, carry):
        t = pl.program_id(1)
        prev8 = jnp.where(t > 0, up_ref[...], 0.0)
        _, xc, _, _, i, _, a, mult = _lru_pre(u_ref[...], prev8, cw_ref, cb_ref, wr_ref, br_ref, wi_ref, bi_ref, lam_ref)
        pa, hb = _scan_rows(a, mult * (i * xc), False)
        h0 = jnp.where(t > 0, carry[7:8, :], 0.0)
        h = hb + pa * h0
        h_ref[...] = h
        carry[...] = h[tt - 8:, :]
        g = g_ref[...]
        al_ref[...] = (h * g * _sig(g)).astype(BF16)

    return pl.pallas_call(
        body, grid=(D // cb, nT),
        in_specs=[tile(1), halo(1), tile(2), par(CONV_K), par(1), wblk, par(1), wblk, par(1), par(1)],
        out_specs=[tile(0), tile(0)],
        out_shape=[jax.ShapeDtypeStruct((S, D), F32), jax.ShapeDtypeStruct((S, D), BF16)],
        scratch_shapes=[pltpu.VMEM((8, cb), F32)],
        compiler_params=_cp(("parallel", "arbitrary")), name="lru_fwd",
    )(zbig, zbig, zbig, cw, cbias.reshape(1, D), wr, br.reshape(1, D), wi, bi.reshape(1, D), lam.reshape(1, D))


def _lru_bwd(zbig, h, d_al, cw, cbias, wr, br, wi, bi, lam, D):
    S = zbig.shape[0]
    tt, cb = _pick(S, LRU_TT, 8), _pick(D, LRU_CB)
    nT = S // tt
    nb = cb // LRU_BLOCK_DIM
    tile, halo, par, wblk, tmap = _lru_specs(D, cb, tt, nT, True)

    def body(u_ref, up_ref, g_ref, h_ref, hp_ref, dal_ref, cw_ref, cb_ref, wr_ref, br_ref, wi_ref, bi_ref, lam_ref,
             du_ref, dg_ref, dcw_ref, dcb_ref, dwr_ref, dbr_ref, dwi_ref, dbi_ref, dlam_ref, g_car, a_car, x_car):
        step = pl.program_id(1)
        first = step == 0
        t = nT - 1 - step
        prev8 = jnp.where(t > 0, up_ref[...], 0.0)
        us, xc, x16, r, i, sp, a, mult = _lru_pre(u_ref[...], prev8, cw_ref, cb_ref, wr_ref, br_ref, wi_ref, bi_ref, lam_ref)
        hv = h_ref[...]
        h_m1 = _shift_down(hv, jnp.where(t > 0, hp_ref[...], 0.0), 1)
        g, dal = g_ref[...], dal_ref[...]
        sg = _sig(g)
        dg_ref[...] = (dal * hv * sg * (1.0 + g * (1.0 - sg))).astype(BF16)
        dh = dal * g * sg
        coef = _shift_up(a, jnp.where(first, 0.0, a_car[...]), 1)
        pa, gb = _scan_rows(coef, dh, True)
        G = gb + pa * jnp.where(first, 0.0, g_car[0:1, :])
        g_car[...] = G[:8]
        a_car[...] = a[:8]
        da = G * h_m1
        ixc = i * xc
        dixc = G * mult
        dlog = da * a - (G * ixc) * (a * a) / mult
        dpr = dlog * (-LRU_C * sp) * r * (1.0 - r)
        dpi = dixc * xc * i * (1.0 - i)
        dxc = dixc * i
        dsp = jnp.sum(dlog * (-LRU_C) * r, axis=0, keepdims=True)
        dpr16, dpi16 = dpr.astype(BF16), dpi.astype(BF16)
        blk = lambda k: slice(k * LRU_BLOCK_DIM, (k + 1) * LRU_BLOCK_DIM)
        back = []
        for k in range(nb):
            xk = x16[:, blk(k)].T
            dwr_k = jnp.dot(xk, dpr16[:, blk(k)], preferred_element_type=F32)
            dwi_k = jnp.dot(xk, dpi16[:, blk(k)], preferred_element_type=F32)

            @pl.when(first)
            def _():
                dwr_ref[k] = dwr_k
                dwi_ref[k] = dwi_k

            @pl.when(jnp.logical_not(first))
            def _():
                dwr_ref[k] += dwr_k
                dwi_ref[k] += dwi_k

            back.append(lax.dot_general(dpr16[:, blk(k)], wr_ref[k].astype(BF16), NT_DIMS, preferred_element_type=F32)
                        + lax.dot_general(dpi16[:, blk(k)], wi_ref[k].astype(BF16), NT_DIMS, preferred_element_type=F32))
        dxc = dxc + jnp.concatenate(back, axis=1)
        _accum(dbr_ref, jnp.sum(dpr, axis=0, keepdims=True), first)
        _accum(dbi_ref, jnp.sum(dpi, axis=0, keepdims=True), first)
        _accum(dlam_ref, dsp * (-_sig(-lam_ref[...])), first)
        _accum(dcb_ref, jnp.sum(dxc, axis=0, keepdims=True), first)
        _accum(dcw_ref, jnp.concatenate([jnp.sum(dxc * us[3 - k], axis=0, keepdims=True) for k in range(CONV_K)], axis=0), first)
        nxt = jnp.where(first, 0.0, x_car[...])
        du = cw_ref[3:4, :] * dxc
        for s in range(1, CONV_K):
            du = du + cw_ref[3 - s:4 - s, :] * _shift_up(dxc, nxt, s)
        x_car[...] = dxc[:8]
        du_ref[...] = du.astype(BF16)

    act = jax.ShapeDtypeStruct((S, D), BF16)
    vec = jax.ShapeDtypeStruct((1, D), F32)
    wsh = jax.ShapeDtypeStruct(wr.shape, F32)
    rtile = pl.BlockSpec((tt, cb), lambda c, t: (tmap(t), c))
    rhalo = pl.BlockSpec((8, cb), lambda c, t: (jnp.maximum(tmap(t) * (tt // 8) - 1, 0), c))
    return pl.pallas_call(
        body, grid=(D // cb, nT),
        in_specs=[tile(1), halo(1), tile(2), rtile, rhalo, rtile, par(CONV_K), par(1), wblk, par(1), wblk, par(1), par(1)],
        out_specs=[rtile, rtile, par(CONV_K), par(1), wblk, par(1), wblk, par(1), par(1)],
        out_shape=[act, act, jax.ShapeDtypeStruct((CONV_K, D), F32), vec, wsh, vec, wsh, vec, vec],
        scratch_shapes=[pltpu.VMEM((8, cb), F32)] * 3,
        compiler_params=_cp(("parallel", "arbitrary")), name="lru_bwd",
    )(zbig, zbig, zbig, h, h, d_al, cw, cbias.reshape(1, D), wr, br.reshape(1, D), wi, bi.reshape(1, D), lam.reshape(1, D))


def _adamw(w, g, m, v, name):
    L, R, C = w.shape
    tm = _pick(R, max(8, ((1 << 18) // C) // 8 * 8), 8)
    c1 = 1.0 - ADAM_B1 ** ADAM_STEP
    c2 = 1.0 - ADAM_B2 ** ADAM_STEP

    def body(w_ref, g_ref, m_ref, v_ref, d_ref, mo_ref, vo_ref):
        gv = g_ref[...]
        mn = ADAM_B1 * m_ref[...] + (1.0 - ADAM_B1) * gv
        vn = ADAM_B2 * v_ref[...] + (1.0 - ADAM_B2) * (gv * gv)
        d_ref[...] = -ADAM_LR * ((mn / c1) / (jnp.sqrt(vn / c2) + ADAM_EPS) + ADAM_WD * w_ref[...])
        mo_ref[...] = mn
        vo_ref[...] = vn

    o = jax.ShapeDtypeStruct((L, R, C), F32)
    blk = pl.BlockSpec((1, tm, C), lambda l, r: (l, r, 0))
    return pl.pallas_call(
        body, grid=(L, R // tm), in_specs=[blk] * 4, out_specs=[blk] * 3, out_shape=[o] * 3,
        compiler_params=_cp(("parallel", "parallel")), name=name,
    )(w, g, m, v)


def _sum_slabs(x, name, out_dtype=F32):
    n, R, C = x.shape
    tm = _pick(R, max(8, ((1 << 18) // C) // 8 * 8), 16)

    def body(x_ref, o_ref):
        s = x_ref[0].astype(F32)
        for k in range(1, n):
            s = s + x_ref[k].astype(F32)
        o_ref[...] = s.astype(out_dtype)

    return pl.pallas_call(
        body, grid=(R // tm,), in_specs=[pl.BlockSpec((n, tm, C), lambda r: (0, r, 0))], out_specs=_rows(C)(tm),
        out_shape=jax.ShapeDtypeStruct((R, C), out_dtype), compiler_params=_cp(("parallel",)), name=name,
    )(x)


def _sum_core_halves(g, recv, core, tag):
    n, _, Rh, C = g.shape
    tm = _pick(Rh, max(16, ((1 << 18) // C) // 16 * 16), 16)

    def body(c_ref, g_ref, r_ref, o_ref):
        o_ref[0] = (g_ref[0, 0].astype(F32) + r_ref[0, 0].astype(F32)).astype(BF16)

    return pl.pallas_call(
        body,
        grid_spec=pltpu.PrefetchScalarGridSpec(
            num_scalar_prefetch=1, grid=(n, Rh // tm),
            in_specs=[pl.BlockSpec((1, 1, tm, C), lambda k, r, c_ref: (k, c_ref[0], r, 0)),
                      pl.BlockSpec((1, 1, tm, C), lambda k, r, c_ref: (k, 0, r, 0))],
            out_specs=pl.BlockSpec((1, tm, C), lambda k, r, c_ref: (k, r, 0)),
        ),
        out_shape=jax.ShapeDtypeStruct((n, Rh, C), BF16),
        compiler_params=_cp(("parallel", "parallel")), name="grad_sum_cores" + tag,
    )(core.reshape(1).astype(jnp.int32), g, recv)


ANY = pl.BlockSpec(memory_space=pl.ANY)


def _place():
    x, y, c = lax.axis_index("x"), lax.axis_index("y"), lax.axis_index("c")
    chips = [(1 - x, y), (x, 1 - y), (1 - x, 1 - y)]
    return x, y, c, chips


def _allgather_chips(shard, name):
    R, C = shard.shape
    Rh = R // 2

    def body(x_ref, out_ref, send_sems, recv_sems):
        x, y, c, chips = _place()
        me = 2 * x + y
        sibling = (x, y, 1 - c)

        def half(k, hc):
            return out_ref.at[k, pl.ds(hc * Rh, Rh), :]

        first = [pltpu.make_async_remote_copy(
            src_ref=x_ref.at[pl.ds(c * Rh, Rh), :], dst_ref=half(me, c), send_sem=send_sems.at[j], recv_sem=recv_sems.at[j],
            device_id=(*chip, c), device_id_type=MESH) for j, chip in enumerate(chips)]
        for cp in first:
            cp.start()

        def landed(j, chip, hc):
            k = 2 * chip[0] + chip[1]
            return pltpu.make_async_remote_copy(
                src_ref=half(k, hc), dst_ref=half(k, hc), send_sem=send_sems.at[j], recv_sem=recv_sems.at[j],
                device_id=sibling, device_id_type=MESH)

        passed = []
        for j, chip in enumerate(chips):
            landed(j, chip, c).wait_recv()
            cp = landed(3 + j, chip, c)
            cp.start()
            passed.append(cp)
        for j, chip in enumerate(chips):
            landed(3 + j, chip, 1 - c).wait_recv()
        for cp in first + passed:
            cp.wait_send()

    out = pl.pallas_call(
        body, in_specs=[ANY], out_specs=ANY, out_shape=jax.ShapeDtypeStruct((N_CHIPS, R, C), shard.dtype),
        scratch_shapes=[pltpu.SemaphoreType.DMA((6,)), pltpu.SemaphoreType.DMA((6,))],
        name=name,
    )(shard)
    return lax.dynamic_update_slice(out, shard[None], (2 * lax.axis_index("x") + lax.axis_index("y"), 0, 0))


def _allgather_all(blockv, name):
    R, C = blockv.shape

    def body(x_ref, out_ref, send_sems, recv_sems):
        x, y, c, chips = _place()
        sibling = (x, y, 1 - c)

        def slab(px, py, pc):
            return out_ref.at[4 * px + 2 * py + pc]

        def copy(k, block, to, src=None):
            return pltpu.make_async_remote_copy(
                src_ref=slab(*block) if src is None else src, dst_ref=slab(*block), send_sem=send_sems.at[k],
                recv_sem=recv_sems.at[k], device_id=to, device_id_type=MESH)

        first = [copy(0, (x, y, c), sibling, src=x_ref)]
        first += [copy(1 + j, (x, y, c), (*chip, c), src=x_ref) for j, chip in enumerate(chips)]
        for cp in first:
            cp.start()
        passed = [copy(4 + j, (*chip, c), sibling) for j, chip in enumerate(chips)]
        for j, chip in enumerate(chips):
            copy(1 + j, (*chip, c), (x, y, c)).wait_recv()
            passed[j].start()
        copy(0, (x, y, 1 - c), (x, y, c)).wait_recv()
        for j, chip in enumerate(chips):
            copy(4 + j, (*chip, 1 - c), (x, y, c)).wait_recv()
        for cp in first + passed:
            cp.wait_send()

    out = pl.pallas_call(
        body, in_specs=[ANY], out_specs=ANY, out_shape=jax.ShapeDtypeStruct((N_DEV, R, C), blockv.dtype),
        scratch_shapes=[pltpu.SemaphoreType.DMA((7,)), pltpu.SemaphoreType.DMA((7,))],
        name=name,
    )(blockv)
    me = 4 * lax.axis_index("x") + 2 * lax.axis_index("y") + lax.axis_index("c")
    return lax.dynamic_update_slice(out, blockv[None], (me, 0, 0))


def _swap_cores_half(g, tag):
    n, _, Rh, C = g.shape

    def body(g_ref, out_ref, send_sem, recv_sem):
        x, y, c, _ = _place()
        cp = pltpu.make_async_remote_copy(
            src_ref=g_ref.at[:, pl.ds(1 - c, 1)], dst_ref=out_ref, send_sem=send_sem, recv_sem=recv_sem,
            device_id=(x, y, 1 - c), device_id_type=MESH)
        cp.start()
        cp.wait()

    return pl.pallas_call(
        body, in_specs=[ANY], out_specs=ANY, out_shape=jax.ShapeDtypeStruct((n, 1, Rh, C), g.dtype),
        scratch_shapes=[pltpu.SemaphoreType.DMA, pltpu.SemaphoreType.DMA], name="grad_swap_cores" + tag,
    )(g)


def _alltoall_chips(s, tag):
    n, Rh, C = s.shape

    def body(s_ref, out_ref, send_sems, recv_sems):
        x, y, c, chips = _place()
        me = 2 * x + y
        sent = []
        for j, chip in enumerate(chips):
            k = 2 * chip[0] + chip[1]
            cp = pltpu.make_async_remote_copy(
                src_ref=s_ref.at[k], dst_ref=out_ref.at[me], send_sem=send_sems.at[j], recv_sem=recv_sems.at[j],
                device_id=(*chip, c), device_id_type=MESH)
            cp.start()
            sent.append(cp)
        for j, chip in enumerate(chips):
            k = 2 * chip[0] + chip[1]
            pltpu.make_async_remote_copy(
                src_ref=s_ref.at[k], dst_ref=out_ref.at[k], send_sem=send_sems.at[j], recv_sem=recv_sems.at[j],
                device_id=(*chip, c), device_id_type=MESH).wait_recv()
        for cp in sent:
            cp.wait_send()

    out = pl.pallas_call(
        body, in_specs=[ANY], out_specs=ANY, out_shape=jax.ShapeDtypeStruct((n, Rh, C), s.dtype),
        scratch_shapes=[pltpu.SemaphoreType.DMA((3,)), pltpu.SemaphoreType.DMA((3,))],
        name="grad_alltoall_chips" + tag,
    )(s)
    me = 2 * lax.axis_index("x") + lax.axis_index("y")
    return lax.dynamic_update_slice(out, lax.dynamic_slice_in_dim(s, me, 1, axis=0), (me, 0, 0))


def _join_core_halves(half, tag):
    Rh, C = half.shape

    def body(h_ref, out_ref, send_sem, recv_sem):
        x, y, c, _ = _place()
        cp = pltpu.make_async_remote_copy(
            src_ref=h_ref, dst_ref=out_ref.at[c], send_sem=send_sem, recv_sem=recv_sem,
            device_id=(x, y, 1 - c), device_id_type=MESH)
        cp.start()
        pltpu.make_async_remote_copy(
            src_ref=h_ref, dst_ref=out_ref.at[1 - c], send_sem=send_sem, recv_sem=recv_sem,
            device_id=(x, y, 1 - c), device_id_type=MESH).wait_recv()
        cp.wait_send()

    out = pl.pallas_call(
        body, in_specs=[ANY], out_specs=ANY, out_shape=jax.ShapeDtypeStruct((2, Rh, C), half.dtype),
        scratch_shapes=[pltpu.SemaphoreType.DMA, pltpu.SemaphoreType.DMA], name="grad_join_cores" + tag,
    )(half)
    return lax.dynamic_update_slice(out, half[None], (lax.axis_index("c"), 0, 0))


def _pack(arrays, cols, row_align):
    flat = jnp.concatenate([a.reshape(-1) for a in arrays])
    unit = cols * row_align
    total = -(-flat.size // unit) * unit
    return jnp.pad(flat, (0, total - flat.size)).reshape(total // cols, cols)


def _unpack(buf, shapes):
    flat = buf.reshape(-1)
    out, off = [], 0
    for shp in shapes:
        n = 1
        for d in shp:
            n *= d
        out.append(flat[off:off + n].reshape(shp))
        off += n
    return out


def _layer_fwd(x, p_l, w, tabs, dm):
    D, H, ql, kvl = dm["D"], dm["H"], dm["ql"], dm["kvl"]
    h = _rms_fwd(x, w["attn_norm"], "attn_norm_fwd")
    zbig = _mm(h, w["w_big"], "nn", "in_proj_big")
    zsm = _mm(h, w["w_sm"], "nn", "in_proj_small")
    qn, kvn, kr = _latent_fwd(zsm, w["q_a_norm"], w["kv_a_norm"], tabs, ql, kvl)
    q = _q_rope(_mm(qn, w["w_q"], "nn", "q_proj"), tabs, False, "q_rope_fwd", gain=QK_SCALE * LOG2E)
    kv = _mm(kvn, w["w_kv"], "nn", "kv_proj", out_dtype=BF16)
    o, a_mla, lse = _attn_fwd(q, kv, kr, zbig, H)
    y_mla = _mm(a_mla, w["w_o_mla"], "nn", "o_mla_proj")
    h_lru, a_lru = _lru_fwd(zbig, w["conv_w"], w["conv_b"], w["w_rg"], w["b_rg"], w["w_ig"], w["b_ig"], w["lru_lambda"], D)
    y_lru = _mm(a_lru, w["w_o_lru"], "nn", "o_lru_proj")
    merged = _merge_fwd(zbig, y_mla, y_lru, D)
    x1 = _mm(merged, w["w_out"], "nn", "out_proj", add=x)
    hp = _rms_fwd(x1, w["ple_norm"], "ple_norm_fwd")
    pg = _mm(hp, w["w_ple_gate"], "nn", "ple_gate_proj")
    pe = _mm(p_l, w["w_ple"], "nn", "ple_proj")
    x2 = _ple_fwd(x1, pe, pg)
    res = dict(x=x, h=h, zbig=zbig, zsm=zsm, qn=qn, kvn=kvn, kr=kr, q=q, kv=kv, o=o, a_mla=a_mla, lse=lse, y_mla=y_mla,
               h_lru=h_lru, a_lru=a_lru, y_lru=y_lru, merged=merged, x1=x1, hp=hp, pg=pg, pe=pe, p=p_l)
    return x2, res


def _layer_bwd(dx2, r, w, tabs, dm):
    D, H, ql, kvl = dm["D"], dm["H"], dm["ql"], dm["kvl"]
    S = dx2.shape[0]
    g = {}
    d_pe, d_pg = _ple_bwd(dx2, r["pe"], r["pg"])
    g["w_ple"] = _mm(r["p"], d_pe, "tn", "ple_proj_dw", out_dtype=BF16)
    g["w_ple_gate"] = _mm(r["hp"], d_pg, "tn", "ple_gate_dw", out_dtype=BF16)
    d_hp = _mm(d_pg, w["w_ple_gate"], "nt", "ple_gate_dx")
    dx1, dx1_16, g["ple_norm"] = _rms_bwd(r["x1"], w["ple_norm"], d_hp, dx2, "ple_norm_bwd", True)
    g["w_out"] = _mm(r["merged"], dx1_16, "tn", "out_proj_dw", out_dtype=BF16)
    d_merged = _mm(dx1_16, w["w_out"], "nt", "out_proj_dx")
    d_ym, d_yl, d_mm, d_ml = _merge_bwd(r["zbig"], r["y_mla"], r["y_lru"], d_merged, D)
    g["w_o_mla"] = _mm(r["a_mla"], d_ym, "tn", "o_mla_dw", out_dtype=BF16)
    d_a_mla = _mm(d_ym, w["w_o_mla"], "nt", "o_mla_dx")
    g["w_o_lru"] = _mm(r["a_lru"], d_yl, "tn", "o_lru_dw", out_dtype=BF16)
    d_a_lru = _mm(d_yl, w["w_o_lru"], "nt", "o_lru_dx")
    d_o, d_gm, delta = _attn_gate_bwd(d_a_mla, r["o"], r["zbig"], H)
    dq, dkv, dkr = _attn_bwd(r["q"], r["kv"], r["kr"], d_o, r["lse"][:, :, 0].reshape(H, 1, S), delta[:, :, 0].reshape(H, 1, S), H)
    dq_pre = _q_rope(dq, tabs, True, "q_rope_bwd", gain=QK_SCALE)
    g["w_q"] = _mm(r["qn"], dq_pre, "tn", "q_proj_dw", out_dtype=BF16)
    d_qn = _mm(dq_pre, w["w_q"], "nt", "q_proj_dx")
    g["w_kv"] = _mm(r["kvn"], dkv, "tn", "kv_proj_dw", out_dtype=BF16)
    d_kvn = _mm(dkv, w["w_kv"], "nt", "kv_proj_dx")
    dzsm, g["q_a_norm"], g["kv_a_norm"] = _latent_bwd(r["zsm"], w["q_a_norm"], w["kv_a_norm"], tabs, d_qn, d_kvn, dkr, ql, kvl)
    (d_u, d_gl, g["conv_w"], g["conv_b"], g["w_rg"], g["b_rg"], g["w_ig"], g["b_ig"], g["lru_lambda"]) = _lru_bwd(
        r["zbig"], r["h_lru"], d_a_lru, w["conv_w"], w["conv_b"], w["w_rg"], w["b_rg"], w["w_ig"], w["b_ig"], w["lru_lambda"], D)
    dzbig = jnp.concatenate([d_gm, d_u, d_gl, d_mm, d_ml], axis=1)
    g["w_big"] = _mm(r["h"], dzbig, "tn", "in_proj_big_dw", out_dtype=BF16)
    g["w_sm"] = _mm(r["h"], dzsm, "tn", "in_proj_small_dw", out_dtype=BF16)
    dh = _mm(dzbig, w["w_big"], "nt", "in_proj_big_dx")
    dh = _mm(dzsm, w["w_sm"], "nt", "in_proj_small_dx", add=dh)
    dx, g["attn_norm"] = _rms_bwd(r["x"], w["attn_norm"], dh, dx1, "attn_norm_bwd", False)
    return dx, g


SHARDED = ("w_in", "w_q_b", "w_kv_b", "w_o_mla", "w_o_lru", "w_out", "w_ple_gate", "w_ple")
COL_SHARDED = ("w_in", "w_q_b", "w_kv_b", "w_ple")
ROWED = ("w_o_mla", "w_o_lru", "w_out", "w_ple_gate")
FLAT = ("w_q_b", "w_kv_b", "w_ple")
REPLICATED = ("attn_norm", "q_a_norm", "kv_a_norm", "conv_b", "w_rg", "b_rg", "w_ig", "b_ig", "lru_lambda", "ple_norm", "final_norm")
WEIGHTS = ("attn_norm", "w_in", "q_a_norm", "w_q_b", "kv_a_norm", "w_kv_b", "conv_w", "conv_b", "w_rg", "b_rg", "w_ig", "b_ig",
           "lru_lambda", "w_o_mla", "w_o_lru", "w_out", "ple_norm", "w_ple_gate", "w_ple", "final_norm")


def kernel(x, p, positions, attn_norm, w_in, q_a_norm, w_q_b, kv_a_norm, w_kv_b, conv_w, conv_b, w_rg, b_rg, w_ig, b_ig, lru_lambda, w_o_mla, w_o_lru, w_out, ple_norm, w_ple_gate, w_ple, final_norm, loss_target, m_attn_norm, m_w_in, m_q_a_norm, m_w_q_b, m_kv_a_norm, m_w_kv_b, m_conv_w, m_conv_b, m_w_rg, m_b_rg, m_w_ig, m_b_ig, m_lru_lambda, m_w_o_mla, m_w_o_lru, m_w_out, m_ple_norm, m_w_ple_gate, m_w_ple, m_final_norm, v_attn_norm, v_w_in, v_q_a_norm, v_w_q_b, v_kv_a_norm, v_w_kv_b, v_conv_w, v_conv_b, v_w_rg, v_b_rg, v_w_ig, v_b_ig, v_lru_lambda, v_w_o_mla, v_w_o_lru, v_w_out, v_ple_norm, v_w_ple_gate, v_w_ple, v_final_norm):
    W = dict(attn_norm=attn_norm, w_in=w_in, q_a_norm=q_a_norm, w_q_b=w_q_b, kv_a_norm=kv_a_norm, w_kv_b=w_kv_b, conv_w=conv_w,
             conv_b=conv_b, w_rg=w_rg, b_rg=b_rg, w_ig=w_ig, b_ig=b_ig, lru_lambda=lru_lambda, w_o_mla=w_o_mla, w_o_lru=w_o_lru,
             w_out=w_out, ple_norm=ple_norm, w_ple_gate=w_ple_gate, w_ple=w_ple, final_norm=final_norm)
    M = dict(attn_norm=m_attn_norm, w_in=m_w_in, q_a_norm=m_q_a_norm, w_q_b=m_w_q_b, kv_a_norm=m_kv_a_norm, w_kv_b=m_w_kv_b,
             conv_w=m_conv_w, conv_b=m_conv_b, w_rg=m_w_rg, b_rg=m_b_rg, w_ig=m_w_ig, b_ig=m_b_ig, lru_lambda=m_lru_lambda,
             w_o_mla=m_w_o_mla, w_o_lru=m_w_o_lru, w_out=m_w_out, ple_norm=m_ple_norm, w_ple_gate=m_w_ple_gate, w_ple=m_w_ple,
             final_norm=m_final_norm)
    V = dict(attn_norm=v_attn_norm, w_in=v_w_in, q_a_norm=v_q_a_norm, w_q_b=v_w_q_b, kv_a_norm=v_kv_a_norm, w_kv_b=v_w_kv_b,
             conv_w=v_conv_w, conv_b=v_conv_b, w_rg=v_w_rg, b_rg=v_b_rg, w_ig=v_w_ig, b_ig=v_b_ig, lru_lambda=v_lru_lambda,
             w_o_mla=v_w_o_mla, w_o_lru=v_w_o_lru, w_out=v_w_out, ple_norm=v_ple_norm, w_ple_gate=v_w_ple_gate, w_ple=v_w_ple,
             final_norm=v_final_norm)
    depth = attn_norm.shape[0]
    S, D = x.shape[1], x.shape[2]
    ql, kvl = q_a_norm.shape[1], kv_a_norm.shape[1]
    H = w_q_b.shape[2] * N_CHIPS // (QK_NOPE + QK_ROPE)
    dm = dict(D=D, H=H, ql=ql, kvl=kvl)
    chip = 2 * lax.axis_index("x") + lax.axis_index("y")
    core = lax.axis_index("c")

    def rest_rows(get):
        rows = [a.reshape(-1, D) for n in ROWED for a in get(n)] + [_pack(get(n), D, 16) for n in FLAT]
        fill = -sum(r.shape[0] for r in rows) % PACK_ROWS
        return rows + ([jnp.zeros((fill, D), rows[0].dtype)] if fill else [])

    def unpack_rest(buf):
        out, off = {}, 0
        for n in ROWED + FLAT:
            rows = -(-W[n].size // (D * 16)) * 16
            part = buf[off:off + rows]
            out[n] = part.reshape(W[n].shape) if n in ROWED else _unpack(part, [W[n].shape])[0]
            off += rows
        return out

    cin = w_in.shape[2]
    got_in = _allgather_chips(w_in.astype(BF16).reshape(depth * D, cin), "w_in_allgather")
    got_rest = _allgather_chips(jnp.concatenate(rest_rows(lambda n: [W[n].astype(BF16)]), axis=0), "weights_allgather")
    slabs = [dict(unpack_rest(got_rest[k]), w_in=got_in[k].reshape(depth, D, cin)) for k in range(N_CHIPS)]
    cw_all = _allgather_chips(_pack([conv_w], LANES, 16), "conv_w_allgather")
    conv_w_full = jnp.concatenate([_unpack(cw_all[k], [conv_w.shape])[0] for k in range(N_CHIPS)], axis=-1)

    n_small = ql + kvl + QK_ROPE
    hpc = H // N_CHIPS
    head_pad = ((0, 0), (0, 0), (0, HEAD_PAD - QK_NOPE - QK_ROPE))
    layers = []
    for l in range(depth):
        cat = lambda n, axis: jnp.concatenate([s[n][l] for s in slabs], axis=axis)
        layers.append(dict(
            w_big=jnp.concatenate([slabs[0]["w_in"][l][:, n_small:]] + [s["w_in"][l] for s in slabs[1:]], axis=1),
            w_sm=jnp.pad(slabs[0]["w_in"][l][:, :n_small], ((0, 0), (0, LANES - QK_ROPE))),
            w_q=jnp.concatenate([jnp.pad(s["w_q_b"][l].reshape(ql, hpc, QK_NOPE + QK_ROPE), head_pad).reshape(ql, hpc * HEAD_PAD)
                                 for s in slabs], axis=1),
            w_kv=cat("w_kv_b", 1), w_o_mla=cat("w_o_mla", 0), w_o_lru=cat("w_o_lru", 0), w_out=cat("w_out", 0),
            w_ple_gate=cat("w_ple_gate", 0), w_ple=cat("w_ple", 1), conv_w=conv_w_full[l],
            **{n: W[n][l] for n in REPLICATED if n != "final_norm"}))

    inv_freq = ROPE_THETA ** (-jnp.arange(0, QK_ROPE, 2, dtype=F32) / QK_ROPE)
    tabs = _rope_tables(positions[0], inv_freq)

    xs = x[0]
    saved = []
    for l in range(depth):
        xs, res = _layer_fwd(xs, p[l, 0], layers[l], tabs, dm)
        saved.append(res)
    dx, g_final_norm, loss_part = _loss_head(xs, final_norm, loss_target[0])
    grads = [None] * depth
    for l in reversed(range(depth)):
        dx, grads[l] = _layer_bwd(dx, saved[l], layers[l], tabs, dm)

    def stack(name):
        return jnp.stack([grads[l][name] for l in range(depth)])

    def shard_of(l, n, k):
        g = grads[l]
        if n == "w_in":
            lo, hi = k * W[n].shape[2], (k + 1) * W[n].shape[2]
            parts = ([g["w_sm"][:, lo:min(hi, n_small)]] if lo < n_small else []) + (
                [g["w_big"][:, max(lo, n_small) - n_small:hi - n_small]] if hi > n_small else [])
            return jnp.concatenate(parts, axis=1)
        if n == "w_q_b":
            return g["w_q"].reshape(ql, H, HEAD_PAD)[:, k * hpc:(k + 1) * hpc, :QK_NOPE + QK_ROPE].reshape(ql, -1)
        mine = {"w_kv_b": "w_kv"}.get(n, n)
        if n in COL_SHARDED:
            return g[mine][:, k * W[n].shape[2]:(k + 1) * W[n].shape[2]]
        return g[mine][k * W[n].shape[1]:(k + 1) * W[n].shape[1], :]

    def reduce_scatter(slab_rows, tag):
        _, R, C = slab_rows.shape
        gp = slab_rows.reshape(N_CHIPS, 2, R // 2, C)
        core_sum = _sum_core_halves(gp, _swap_cores_half(gp, tag), core, tag)
        chip_sum = _sum_slabs(_alltoall_chips(core_sum, tag), "grad_sum_chips" + tag)
        return _join_core_halves(chip_sum, tag).reshape(R, C)

    in_rows = [shard_of(l, "w_in", k) for k in range(N_CHIPS) for l in range(depth)]
    g_in = reduce_scatter(jnp.concatenate(in_rows, axis=0).reshape(N_CHIPS, depth * D, cin), "_w_in")
    rest = []
    for k in range(N_CHIPS):
        rest += rest_rows(lambda n: [shard_of(l, n, k) for l in range(depth)])
    g_rest = reduce_scatter(jnp.concatenate(rest, axis=0).reshape(N_CHIPS, -1, D), "_rest")
    G = dict(unpack_rest(g_rest), w_in=g_in.reshape(w_in.shape))

    rep_shapes = [W[n].shape for n in REPLICATED] + [(depth, CONV_K, D), (LANES,)]
    rep = [stack(n).reshape(W[n].shape) for n in REPLICATED if n != "final_norm"]
    rep += [g_final_norm.reshape(D), stack("conv_w"), loss_part.reshape(LANES)]
    rep_sum = _unpack(_sum_slabs(_allgather_all(_pack(rep, LANES, SMALL_ROWS), "small_grads_allgather"), "small_grads_sum"), rep_shapes)
    for n, gv in zip(REPLICATED, rep_sum):
        G[n] = gv
    cshard = D // N_CHIPS
    G["conv_w"] = lax.dynamic_slice_in_dim(rep_sum[-2], chip * cshard, cshard, axis=2)
    loss = rep_sum[-1][0]

    small = REPLICATED + ("conv_w",)
    small_shapes = [W[n].shape for n in small]
    pk = lambda src: _pack([src[n] for n in small], LANES, SMALL_ROWS)[None]
    upd = _adamw(pk(W), pk(G), pk(M), pk(V), "adamw_small")
    delta, new_m, new_v = ({n: a for n, a in zip(small, _unpack(u, small_shapes))} for u in upd)
    for n in SHARDED:
        delta[n], new_m[n], new_v[n] = _adamw(W[n], G[n], M[n], V[n], "adamw_" + n)

    return (loss, dx.reshape(x.shape), *[G[n] for n in WEIGHTS], *[delta[n] for n in WEIGHTS],
            *[new_m[n] for n in WEIGHTS], *[new_v[n] for n in WEIGHTS])
```
